```python
import math
import jax, jax.numpy as jnp
from jax import lax
import numpy as np

D_MODEL = 1024
BATCH = 8
SEQ = 2048
DEPTH = 1

HEAD_DIM = 64
DIL_CONFIGS = ((128, 1), (512, 4), (2048, 16))
N_DIL_GROUPS = len(DIL_CONFIGS)
DIL_HEADS_PER_GROUP = 4
N_DIL_HEADS = N_DIL_GROUPS * DIL_HEADS_PER_GROUP
N_FOX_HEADS = 8
BLOCK = 128
ROPE_THETA = 500000.0
ROPE_DIM = HEAD_DIM // 4
D_FF = -(-(8 * D_MODEL) // (3 * 256)) * 256
EPS = 1e-6
NEG_INF = -1e30

DIL_WIDTH = N_DIL_HEADS * HEAD_DIM
DIL_OUT_WIDTH = DIL_HEADS_PER_GROUP * HEAD_DIM
FOX_WIDTH = N_FOX_HEADS * HEAD_DIM
IN_SPLITS = (DIL_WIDTH, DIL_WIDTH, DIL_WIDTH, FOX_WIDTH, FOX_WIDTH, FOX_WIDTH,
             N_FOX_HEADS, D_MODEL, D_MODEL)
IN_COLS = sum(IN_SPLITS)

kernel_name = "hybrid_dilated_fox_gated_block"


def rmsnorm(x, gain):
    xf = x.astype(jnp.float32)
    y = xf * lax.rsqrt(jnp.mean(xf * xf, axis=-1, keepdims=True) + EPS)
    return (y * gain.astype(jnp.float32)).astype(x.dtype)


def partial_rope(x, positions):
    half = ROPE_DIM // 2
    inv_freq = jnp.power(ROPE_THETA, -jnp.arange(0, ROPE_DIM, 2, dtype=jnp.float32) / ROPE_DIM)
    ang = positions[:, None] * inv_freq[None, :]
    cos = jnp.cos(ang)[None, :, None, :].astype(x.dtype)
    sin = jnp.sin(ang)[None, :, None, :].astype(x.dtype)
    x1, x2, rest = x[..., :half], x[..., half:ROPE_DIM], x[..., ROPE_DIM:]
    return jnp.concatenate([x1 * cos - x2 * sin, x2 * cos + x1 * sin, rest], axis=-1)


def dilated_window_attention(q, k, v, window, dilation):
    B, S, H, Dh = q.shape
    steps = window // dilation
    span = dilation * BLOCK
    s_pad = -(-S // span) * span
    L = s_pad // dilation
    nb = L // BLOCK

    def to_blocks(t):
        t = jnp.pad(t, ((0, 0), (0, s_pad - S), (0, 0), (0, 0)))
        t = t.reshape(B, L, dilation, H, Dh).transpose(0, 2, 1, 3, 4)
        return t.reshape(B * dilation, nb, BLOCK, H, Dh)

    qb, kb, vb = to_blocks(q), to_blocks(k), to_blocks(v)

    def with_prev(t):
        prev = jnp.pad(t[:, :-1], ((0, 0), (1, 0), (0, 0), (0, 0), (0, 0)))
        return jnp.concatenate([prev, t], axis=2)

    kc, vc = with_prev(kb), with_prev(vb)
    scale = 1.0 / math.sqrt(Dh)
    scores = jnp.einsum('znqhd,znkhd->znhqk', qb, kc).astype(jnp.float32) * scale
    qi = jnp.arange(BLOCK)[:, None]
    kj = jnp.arange(2 * BLOCK)[None, :]
    rel = qi + BLOCK - kj
    blk = jnp.arange(nb)[:, None, None]
    valid = (rel >= 0) & (rel <= steps) & (blk * BLOCK + kj - BLOCK >= 0)
    scores = jnp.where(valid[None, :, None, :, :], scores, NEG_INF)
    m = jnp.max(scores, axis=-1, keepdims=True)
    p = jnp.exp(scores - m)
    den = jnp.sum(p, axis=-1, keepdims=True)
    out = jnp.einsum('znhqk,znkhd->znqhd', (p / den).astype(v.dtype), vc)
    lse = (m + jnp.log(den))[..., 0].transpose(0, 1, 3, 2)

    def from_blocks(t):
        t = t.reshape((B, dilation, L) + t.shape[3:])
        t = jnp.moveaxis(t, 1, 2).reshape((B, s_pad) + t.shape[3:])
        return t[:, :S]

    return from_blocks(out), from_blocks(lse)


def forgetting_attention(q, k, v, log_f):
    B, S, H, Dh = q.shape
    F = jnp.cumsum(log_f.astype(jnp.float32), axis=1).transpose(0, 2, 1)
    scale = 1.0 / math.sqrt(Dh)
    outs = []
    for n in range(S // BLOCK):
        lo, hi = n * BLOCK, (n + 1) * BLOCK
        s = jnp.einsum('bqhd,bkhd->bhqk', q[:, lo:hi], k[:, :hi]).astype(jnp.float32) * scale
        s = s + (F[:, :, lo:hi, None] - F[:, :, None, :hi])
        causal = jnp.arange(lo, hi)[:, None] >= jnp.arange(hi)[None, :]
        s = jnp.where(causal[None, None], s, NEG_INF)
        p = jax.nn.softmax(s, axis=-1)
        outs.append(jnp.einsum('bhqk,bkhd->bqhd', p.astype(v.dtype), v[:, :hi]))
    return jnp.concatenate(outs, axis=1)


def token_mixer(h, w_in, w_proj_a, w_proj_b, w_out, b_forget):
    B, S, _ = h.shape
    proj = h @ w_in
    idx = list(np.cumsum(IN_SPLITS)[:-1])
    qa, ka, va, qb, kb, vb, f_logit, g_a, g_b = jnp.split(proj, idx, axis=-1)
    positions = jnp.arange(S, dtype=jnp.float32)

    qa = partial_rope(qa.reshape(B, S, N_DIL_HEADS, HEAD_DIM), positions)
    ka = partial_rope(ka.reshape(B, S, N_DIL_HEADS, HEAD_DIM), positions)
    va = va.reshape(B, S, N_DIL_HEADS, HEAD_DIM)
    group_out, group_lse = [], []
    for g, (window, dilation) in enumerate(DIL_CONFIGS):
        sl = slice(g * DIL_HEADS_PER_GROUP, (g + 1) * DIL_HEADS_PER_GROUP)
        o, lse = dilated_window_attention(qa[:, :, sl], ka[:, :, sl], va[:, :, sl], window, dilation)
        group_out.append(o)
        group_lse.append(lse)
    w_groups = jax.nn.softmax(jnp.stack(group_lse, axis=0), axis=0)
    out_a = jnp.sum(w_groups[..., None].astype(h.dtype) * jnp.stack(group_out, axis=0), axis=0)
    out_a = out_a.reshape(B, S, DIL_OUT_WIDTH)

    log_f = jax.nn.log_sigmoid((f_logit + b_forget).astype(jnp.float32))
    out_b = forgetting_attention(qb.reshape(B, S, N_FOX_HEADS, HEAD_DIM),
                                 kb.reshape(B, S, N_FOX_HEADS, HEAD_DIM),
                                 vb.reshape(B, S, N_FOX_HEADS, HEAD_DIM), log_f)
    out_b = out_b.reshape(B, S, FOX_WIDTH)

    merged = jax.nn.sigmoid(g_a) * (out_a @ w_proj_a) + jax.nn.sigmoid(g_b) * (out_b @ w_proj_b)
    return merged @ w_out


def swiglu(h, w_gate, w_up, w_down):
    return (jax.nn.silu(h @ w_gate) * (h @ w_up)) @ w_down


def setup_inputs(seed: int = 0) -> dict:
    key = jax.random.key(seed)
    ks = jax.random.split(key, 14)
    f32 = jnp.float32

    def dense(k, fan_in, fan_out):
        return jax.random.normal(k, (DEPTH, fan_in, fan_out), f32) * fan_in ** -0.5

    def gain(k):
        return 1.0 + 0.05 * jax.random.normal(k, (DEPTH, D_MODEL), f32)

    return {
        "x": jax.random.normal(ks[0], (BATCH, SEQ, D_MODEL), f32),
        "w_in": dense(ks[1], D_MODEL, IN_COLS),
        "w_proj_a": dense(ks[2], DIL_OUT_WIDTH, D_MODEL),
        "w_proj_b": dense(ks[3], FOX_WIDTH, D_MODEL),
        "w_out": dense(ks[4], D_MODEL, D_MODEL),
        "b_forget": jax.random.uniform(ks[5], (DEPTH, N_FOX_HEADS), f32, minval=1.0, maxval=5.0),
        "w_ffn_gate": dense(ks[6], D_MODEL, D_FF),
        "w_ffn_up": dense(ks[7], D_MODEL, D_FF),
        "w_ffn_down": dense(ks[8], D_FF, D_MODEL),
        "norm_mix_pre": gain(ks[9]),
        "norm_mix_post": gain(ks[10]),
        "norm_ffn_pre": gain(ks[11]),
        "norm_ffn_post": gain(ks[12]),
    }


def reference(x, w_in, w_proj_a, w_proj_b, w_out, b_forget, w_ffn_gate, w_ffn_up, w_ffn_down,
              norm_mix_pre, norm_mix_post, norm_ffn_pre, norm_ffn_post):
    for layer in range(DEPTH):
        h = rmsnorm(x, norm_mix_pre[layer])
        mix = token_mixer(h, w_in[layer], w_proj_a[layer], w_proj_b[layer], w_out[layer], b_forget[layer])
        x = x + rmsnorm(mix, norm_mix_post[layer])
        h = rmsnorm(x, norm_ffn_pre[layer])
        ff = swiglu(h, w_ffn_gate[layer], w_ffn_up[layer], w_ffn_down[layer])
        x = x + rmsnorm(ff, norm_ffn_post[layer])
    return x
```

```python
import functools

import numpy as np
import jax
import jax.numpy as jnp
from jax import lax
from jax.experimental import pallas as pl
from jax.experimental.pallas import tpu as pltpu

D_MODEL = 1024
HEAD_DIM = 64
DIL_CONFIGS = ((128, 1), (512, 4), (2048, 16))
N_DIL_GROUPS = 3
GROUP_WIDTH = 256
N_FOX_HEADS = 8
FOX_WIDTH = 512
BLOCK = 128
ROPE_THETA = 500000.0
ROPE_DIM = 16
D_FF = 2816
EPS = 1e-6
NEG_INF = -1e30

DIL_WIDTH = N_DIL_GROUPS * GROUP_WIDTH
MAIN_COLS = 3 * DIL_WIDTH + 3 * FOX_WIDTH + 2 * D_MODEL
F_ROWS = 16
LANES = 128
MXU_COLS = 256
VMEM_LIMIT = 56 * 1024 * 1024

F32 = jnp.float32
BF16 = jnp.bfloat16
_NT = (((1,), (1,)), ((), ()))


def _rmsnorm(x, gain):
    return x * lax.rsqrt(jnp.mean(x * x, axis=-1, keepdims=True) + EPS) * gain


def _log_sigmoid(z):
    return jnp.minimum(z, 0.0) - jnp.log1p(jnp.exp(-jnp.abs(z)))


def _in_proj_kernel(x_ref, gain_ref, w_ref, wft_ref, bf_ref, cos_ref, sa_ref, sb_ref,
                    qa0, qa1, qa2, ka0, ka1, ka2, va0, va1, va2,
                    qb_ref, kb_ref, vb_ref, sga_ref, sgb_ref, ft_ref, carry_ref, *, tm):
    i = pl.program_id(1)
    h = _rmsnorm(x_ref[0], gain_ref[...]).astype(BF16)

    def proj(c0, width=MXU_COLS):
        return jnp.dot(h, w_ref[:, c0:c0 + width], preferred_element_type=F32)

    cos, sa, sb = cos_ref[...], sa_ref[...], sb_ref[...]

    def rope(y):
        halves = []
        for c in range(MXU_COLS // LANES):
            yc = y[:, c * LANES:(c + 1) * LANES]
            halves.append(yc * cos + pltpu.roll(yc, LANES - ROPE_DIM // 2, 1) * sa
                          + pltpu.roll(yc, ROPE_DIM // 2, 1) * sb)
        return jnp.concatenate(halves, axis=1)

    for g, (q_o, k_o, v_o) in enumerate(((qa0, ka0, va0), (qa1, ka1, va1), (qa2, ka2, va2))):
        q_o[0] = rope(proj(g * GROUP_WIDTH)).astype(BF16)
        k_o[0] = rope(proj(DIL_WIDTH + g * GROUP_WIDTH)).astype(BF16)
        v_o[0] = proj(2 * DIL_WIDTH + g * GROUP_WIDTH).astype(BF16)
    base = 3 * DIL_WIDTH
    for s, o_ref in enumerate((qb_ref, kb_ref, vb_ref)):
        for c in range(FOX_WIDTH // MXU_COLS):
            o_ref[0, :, c * MXU_COLS:(c + 1) * MXU_COLS] = proj(
                base + s * FOX_WIDTH + c * MXU_COLS).astype(BF16)
    base += 3 * FOX_WIDTH
    for s, o_ref in enumerate((sga_ref, sgb_ref)):
        for c in range(D_MODEL // MXU_COLS):
            o_ref[0, :, c * MXU_COLS:(c + 1) * MXU_COLS] = jax.nn.sigmoid(
                proj(base + s * D_MODEL + c * MXU_COLS)).astype(BF16)

    z = lax.dot_general(wft_ref[...], h, _NT, preferred_element_type=F32) + bf_ref[...]
    logf = _log_sigmoid(z)
    lane = lax.broadcasted_iota(jnp.int32, (F_ROWS, LANES), 1)
    carry = jnp.where(i == 0, 0.0, carry_ref[...])
    for j in range(tm // LANES):
        c = logf[:, j * LANES:(j + 1) * LANES]
        k = 1
        while k < LANES:
            c = c + jnp.where(lane >= k, pltpu.roll(c, k, 1), 0.0)
            k *= 2
        c = c + carry
        ft_ref[0, :, j * LANES:(j + 1) * LANES] = c
        carry = jnp.broadcast_to(c[:, LANES - 1:LANES], (F_ROWS, LANES))
    carry_ref[...] = carry


def _rope_tables(seq):
    half = ROPE_DIM // 2
    inv_freq = np.power(ROPE_THETA, -np.arange(0, ROPE_DIM, 2, dtype=np.float64) / ROPE_DIM)
    ang = np.arange(seq, dtype=np.float64)[:, None] * inv_freq[None, :]
    cos = np.ones((seq, HEAD_DIM)); sa = np.zeros((seq, HEAD_DIM)); sb = np.zeros((seq, HEAD_DIM))
    cos[:, :half] = np.cos(ang); cos[:, half:ROPE_DIM] = np.cos(ang)
    sa[:, :half] = -np.sin(ang)
    sb[:, half:ROPE_DIM] = np.sin(ang)
    rep = LANES // HEAD_DIM
    return tuple(jnp.asarray(np.tile(t, (1, rep)), dtype=F32) for t in (cos, sa, sb))


def _in_proj(x, gain, w_main, wft, bf, tm=512):
    B, S, D = x.shape
    cos, sa, sb = _rope_tables(S)
    const = lambda shape: pl.BlockSpec(shape, lambda b, i: (0,) * len(shape))
    row = lambda w: pl.BlockSpec((1, tm, w), lambda b, i: (b, i, 0))
    tab = pl.BlockSpec((tm, LANES), lambda b, i: (i, 0))
    out_shape = ([jax.ShapeDtypeStruct((B, S, GROUP_WIDTH), BF16)] * 9
                 + [jax.ShapeDtypeStruct((B, S, FOX_WIDTH), BF16)] * 3
                 + [jax.ShapeDtypeStruct((B, S, D), BF16)] * 2
                 + [jax.ShapeDtypeStruct((B, F_ROWS, S), F32)])
    out_specs = ([row(GROUP_WIDTH)] * 9 + [row(FOX_WIDTH)] * 3 + [row(D)] * 2
                 + [pl.BlockSpec((1, F_ROWS, tm), lambda b, i: (b, 0, i))])
    return pl.pallas_call(
        functools.partial(_in_proj_kernel, tm=tm),
        grid=(B, S // tm),
        in_specs=[row(D), const((1, D)), const((D, MAIN_COLS)), const((F_ROWS, D)),
                  const((F_ROWS, 1)), tab, tab, tab],
        out_specs=out_specs,
        out_shape=out_shape,
        scratch_shapes=[pltpu.VMEM((F_ROWS, LANES), F32)],
        compiler_params=pltpu.CompilerParams(
            dimension_semantics=("arbitrary", "arbitrary"), vmem_limit_bytes=VMEM_LIMIT),
        name="in_proj",
    )(x, gain, w_main, wft, bf, cos, sa, sb)


def _dilated_kernel(q_ref, k_ref, v_ref, o_ref, lse_ref, *, d, nb):
    lane = lax.broadcasted_iota(jnp.int32, (BLOCK, LANES), 1)
    lo_half = lane < HEAD_DIM
    for r in range(d):
        for hp in range(GROUP_WIDTH // LANES):
            c0 = r * GROUP_WIDTH + hp * LANES
            for n in range(nb):
                rows = slice(n * BLOCK, (n + 1) * BLOCK)
                k0, kw = (0, BLOCK) if n == 0 else ((n - 1) * BLOCK, 2 * BLOCK)
                q = q_ref[0, rows, c0:c0 + LANES]
                kc = k_ref[0, k0:k0 + kw, c0:c0 + LANES]
                vc = v_ref[0, k0:k0 + kw, c0:c0 + LANES]
                qi = lax.broadcasted_iota(jnp.int32, (BLOCK, kw), 0)
                kj = lax.broadcasted_iota(jnp.int32, (BLOCK, kw), 1)
                valid = (kj <= qi) if n == 0 else ((kj >= qi) & (kj <= qi + BLOCK))
                outs, lses = [], []
                for hh in range(2):
                    qm = jnp.where(lo_half if hh == 0 else ~lo_half, q, jnp.zeros_like(q))
                    s = lax.dot_general(qm, kc, _NT, preferred_element_type=F32)
                    s = jnp.where(valid, s, NEG_INF)
                    m = jnp.max(s, axis=1, keepdims=True)
                    p = jnp.exp(s - m)
                    den = jnp.sum(p, axis=1, keepdims=True)
                    o = jnp.dot(p.astype(BF16), vc, preferred_element_type=F32)
                    outs.append(o * (1.0 / den))
                    lses.append(m + jnp.log(den))
                o_ref[0, rows, c0:c0 + LANES] = jnp.where(lo_half, outs[0], outs[1]).astype(BF16)
                lse_ref[0, rows, c0:c0 + LANES] = jnp.where(lo_half, lses[0], lses[1])


def _dilated_group(q, k, v, d):
    B, S, W = q.shape
    L = S // d
    view = lambda t: t.reshape(B, L, d * W)
    spec = pl.BlockSpec((1, L, d * W), lambda b: (b, 0, 0))
    out, lse = pl.pallas_call(
        functools.partial(_dilated_kernel, d=d, nb=L // BLOCK),
        grid=(B,),
        in_specs=[spec, spec, spec],
        out_specs=[spec, spec],
        out_shape=[jax.ShapeDtypeStruct((B, L, d * W), BF16),
                   jax.ShapeDtypeStruct((B, L, d * W), F32)],
        compiler_params=pltpu.CompilerParams(
            dimension_semantics=("arbitrary",), vmem_limit_bytes=VMEM_LIMIT),
        name=f"dilated_d{d}",
    )(view(q), view(k), view(v))
    return out.reshape(B * S, W), lse.reshape(B * S, W)


def _fox_kernel(q_ref, k_ref, v_ref, ft_ref, o_ref, *, tq, nblk):
    p = pl.program_id(1)
    i = pl.program_id(2)
    q = q_ref[0]
    lane = lax.broadcasted_iota(jnp.int32, (tq, LANES), 1)
    lo_half = lane < HEAD_DIM
    qi = lax.broadcasted_iota(jnp.int32, (tq, tq), 0)
    kj = lax.broadcasted_iota(jnp.int32, (tq, tq), 1)
    causal = kj <= qi
    outs = []
    for hh in range(2):
        frow0 = (2 * p + hh) * nblk
        qm = jnp.where(lo_half if hh == 0 else ~lo_half, q, jnp.zeros_like(q))
        c = ft_ref[0, pl.ds(frow0 + i, 1), :][:, 0:1]

        def tile(j, carry, masked):
            m, l, acc = carry
            kc = k_ref[0, pl.ds(pl.multiple_of(j * tq, tq), tq), :]
            vc = v_ref[0, pl.ds(pl.multiple_of(j * tq, tq), tq), :]
            bias = c - ft_ref[0, pl.ds(frow0 + j, 1), :]
            s = lax.dot_general(qm, kc, _NT, preferred_element_type=F32) + bias
            if masked:
                s = jnp.where(causal, s, NEG_INF)
            m_new = jnp.maximum(m, jnp.max(s, axis=1, keepdims=True))
            alpha = jnp.exp(m - m_new)
            pe = jnp.exp(s - m_new)
            l = alpha * l + jnp.sum(pe, axis=1, keepdims=True)
            acc = alpha * acc + jnp.dot(pe.astype(BF16), vc, preferred_element_type=F32)
            return m_new, l, acc

        init = (jnp.full((tq, 1), NEG_INF, F32), jnp.zeros((tq, 1), F32),
                jnp.zeros((tq, LANES), F32))
        carry = lax.fori_loop(0, i, lambda j, cr: tile(j, cr, False), init)
        m, l, acc = tile(i, carry, True)
        outs.append(acc * (1.0 / l))
    o_ref[0] = jnp.where(lo_half, outs[0], outs[1]).astype(BF16)


def _fox(q, k, v, ft, tq=256):
    B, S, W = q.shape
    nblk = S // tq
    ftv = ft.reshape(B, F_ROWS * nblk, tq)
    kv_spec = pl.BlockSpec((1, S, LANES), lambda b, p, i: (b, 0, p))
    q_spec = pl.BlockSpec((1, tq, LANES), lambda b, p, i: (b, i, p))
    out = pl.pallas_call(
        functools.partial(_fox_kernel, tq=tq, nblk=nblk),
        grid=(B, W // LANES, nblk),
        in_specs=[q_spec, kv_spec, kv_spec,
                  pl.BlockSpec((1, F_ROWS * nblk, tq), lambda b, p, i: (b, 0, 0))],
        out_specs=q_spec,
        out_shape=jax.ShapeDtypeStruct((B, S, W), BF16),
        compiler_params=pltpu.CompilerParams(
            dimension_semantics=("arbitrary", "arbitrary", "arbitrary"),
            vmem_limit_bytes=VMEM_LIMIT),
        name="fox",
    )(q, k, v, ftv)
    return out.reshape(B * S, W)


def _mix_kernel(x_ref, o0, o1, o2, l0, l1, l2, ob_ref, sga_ref, sgb_ref,
                wa_ref, wb_ref, wo_ref, gain_ref, y_ref):
    la, lb, lc = l0[...], l1[...], l2[...]
    mx = jnp.maximum(jnp.maximum(la, lb), lc)
    e0, e1, e2 = jnp.exp(la - mx), jnp.exp(lb - mx), jnp.exp(lc - mx)
    inv = 1.0 / (e0 + e1 + e2)
    oa = ((e0 * o0[...].astype(F32) + e1 * o1[...].astype(F32) + e2 * o2[...].astype(F32))
          * inv).astype(BF16)
    a = jnp.dot(oa, wa_ref[...], preferred_element_type=F32)
    b = jnp.dot(ob_ref[...], wb_ref[...], preferred_element_type=F32)
    merged = (sga_ref[...].astype(F32) * a + sgb_ref[...].astype(F32) * b).astype(BF16)
    mix = jnp.dot(merged, wo_ref[...], preferred_element_type=F32)
    y_ref[...] = x_ref[...] + _rmsnorm(mix, gain_ref[...])


def _mix(x2, outs, lses, ob, sga, sgb, wa, wb, wo, gain, tm=512):
    T, D = x2.shape
    row = lambda w: pl.BlockSpec((tm, w), lambda i: (i, 0))
    const = lambda shape: pl.BlockSpec(shape, lambda i: (0, 0))
    return pl.pallas_call(
        _mix_kernel,
        grid=(T // tm,),
        in_specs=[row(D)] + [row(GROUP_WIDTH)] * 6 + [row(FOX_WIDTH), row(D), row(D),
                  const((GROUP_WIDTH, D)), const((FOX_WIDTH, D)), const((D, D)), const((1, D))],
        out_specs=row(D),
        out_shape=jax.ShapeDtypeStruct((T, D), F32),
        compiler_params=pltpu.CompilerParams(
            dimension_semantics=("arbitrary",), vmem_limit_bytes=VMEM_LIMIT),
        name="mix",
    )(x2, *outs, *lses, ob, sga, sgb, wa, wb, wo, gain)


def _ffn_kernel(x_ref, g1_ref, g2_ref, wg_ref, wu_ref, wd_ref, o_ref):
    x = x_ref[...]
    h = _rmsnorm(x, g1_ref[...]).astype(BF16)
    acc = jnp.zeros(x.shape, F32)
    for c in range(D_FF // MXU_COLS):
        cols = slice(c * MXU_COLS, (c + 1) * MXU_COLS)
        g = jnp.dot(h, wg_ref[:, cols], preferred_element_type=F32)
        u = jnp.dot(h, wu_ref[:, cols], preferred_element_type=F32)
        a = (g * jax.nn.sigmoid(g) * u).astype(BF16)
        acc = acc + jnp.dot(a, wd_ref[cols, :], preferred_element_type=F32)
    o_ref[...] = x + _rmsnorm(acc, g2_ref[...])


def _ffn(x2, g1, g2, wg, wu, wd, tm=512):
    T, D = x2.shape
    row = pl.BlockSpec((tm, D), lambda i: (i, 0))
    const = lambda shape: pl.BlockSpec(shape, lambda i: (0, 0))
    return pl.pallas_call(
        _ffn_kernel,
        grid=(T // tm,),
        in_specs=[row, const((1, D)), const((1, D)), const((D, D_FF)), const((D, D_FF)),
                  const((D_FF, D))],
        out_specs=row,
        out_shape=jax.ShapeDtypeStruct((T, D), F32),
        compiler_params=pltpu.CompilerParams(
            dimension_semantics=("arbitrary",), vmem_limit_bytes=VMEM_LIMIT),
        name="ffn",
    )(x2, g1, g2, wg, wu, wd)


def kernel(x, w_in, w_proj_a, w_proj_b, w_out, b_forget, w_ffn_gate, w_ffn_up, w_ffn_down,
           norm_mix_pre, norm_mix_post, norm_ffn_pre, norm_ffn_post):
    B, S, D = x.shape
    scale = 1.0 / np.sqrt(HEAD_DIM)
    for layer in range(w_in.shape[0]):
        w = w_in[layer]
        o = 0
        secs = []
        for width in (DIL_WIDTH, DIL_WIDTH, DIL_WIDTH, FOX_WIDTH, FOX_WIDTH, FOX_WIDTH,
                      N_FOX_HEADS, D, D):
            secs.append(w[:, o:o + width])
            o += width
        wqa, wka, wva, wqb, wkb, wvb, wf, wga, wgb = secs
        w_main = jnp.concatenate([wqa * scale, wka, wva, wqb * scale, wkb, wvb, wga, wgb],
                                 axis=1).astype(BF16)
        wft = jnp.zeros((F_ROWS, D), F32).at[:N_FOX_HEADS].set(wf.T).astype(BF16)
        bf = jnp.zeros((F_ROWS, 1), F32).at[:N_FOX_HEADS, 0].set(b_forget[layer])

        (qa0, qa1, qa2, ka0, ka1, ka2, va0, va1, va2, qb, kb, vb, sga, sgb, ft) = _in_proj(
            x, norm_mix_pre[layer][None, :], w_main, wft, bf)

        outs, lses = [], []
        for (q, k, v), (_, d) in zip(((qa0, ka0, va0), (qa1, ka1, va1), (qa2, ka2, va2)),
                                     DIL_CONFIGS):
            o_g, l_g = _dilated_group(q, k, v, d)
            outs.append(o_g)
            lses.append(l_g)
        ob = _fox(qb, kb, vb, ft)

        x2 = _mix(x.reshape(B * S, D), outs, lses, ob,
                  sga.reshape(B * S, D), sgb.reshape(B * S, D),
                  w_proj_a[layer].astype(BF16), w_proj_b[layer].astype(BF16),
                  w_out[layer].astype(BF16), norm_mix_post[layer][None, :])
        x2 = _ffn(x2, norm_ffn_pre[layer][None, :], norm_ffn_post[layer][None, :],
                  w_ffn_gate[layer].astype(BF16), w_ffn_up[layer].astype(BF16),
                  w_ffn_down[layer].astype(BF16))
        x = x2.reshape(B, S, D)
    return x
```

```python
import functools

import numpy as np
import jax
import jax.numpy as jnp
from jax import lax
from jax.experimental import pallas as pl
from jax.experimental.pallas import tpu as pltpu

D_MODEL = 1024
HEAD_DIM = 64
DIL_CONFIGS = ((128, 1), (512, 4), (2048, 16))
N_DIL_GROUPS = 3
GROUP_WIDTH = 256
N_FOX_HEADS = 8
FOX_WIDTH = 512
BLOCK = 128
ROPE_THETA = 500000.0
ROPE_DIM = 16
D_FF = 2816
EPS = 1e-6
NEG_INF = -1e30

DIL_WIDTH = N_DIL_GROUPS * GROUP_WIDTH
MAIN_COLS = 3 * DIL_WIDTH + 3 * FOX_WIDTH + 2 * D_MODEL
F_ROWS = 16
LANES = 128
MXU_COLS = 256
VMEM_LIMIT = 56 * 1024 * 1024

F32 = jnp.float32
BF16 = jnp.bfloat16
_NT = (((1,), (1,)), ((), ()))


def _rmsnorm(x, gain):
    return x * lax.rsqrt(jnp.mean(x * x, axis=-1, keepdims=True) + EPS) * gain


def _log_sigmoid(z):
    return jnp.minimum(z, 0.0) - jnp.log1p(jnp.exp(-jnp.abs(z)))


def _in_proj_kernel(x_ref, gain_ref, w_ref, wft_ref, bf_ref, cos_ref, sa_ref, sb_ref,
                    qa0, qa1, qa2, ka0, ka1, ka2, va0, va1, va2,
                    qb_ref, kb_ref, vb_ref, sga_ref, sgb_ref, ft_ref, carry_ref, *stage_refs, tm):
    i = pl.program_id(1)
    h = _rmsnorm(x_ref[0], gain_ref[...]).astype(BF16)

    def proj(c0, width=MXU_COLS):
        return jnp.dot(h, w_ref[:, c0:c0 + width], preferred_element_type=F32)

    cos, sa, sb = cos_ref[...], sa_ref[...], sb_ref[...]

    def rope(y):
        halves = []
        for c in range(MXU_COLS // LANES):
            yc = y[:, c * LANES:(c + 1) * LANES]
            halves.append(yc * cos + pltpu.roll(yc, LANES - ROPE_DIM // 2, 1) * sa
                          + pltpu.roll(yc, ROPE_DIM // 2, 1) * sb)
        return jnp.concatenate(halves, axis=1)

    stage = iter(stage_refs)

    def put(o_ref, y, d):
        if d == 1:
            o_ref[0, 0] = y.astype(BF16)
            return
        st = next(stage)
        for c in range(GROUP_WIDTH // LANES):
            st[c] = y[:, c * LANES:(c + 1) * LANES]
        for r in range(d):
            for c in range(GROUP_WIDTH // LANES):
                o_ref[0, r, :, c * LANES:(c + 1) * LANES] = st[
                    c, pl.ds(r, tm // d, stride=d), :].astype(BF16)

    for g, (q_o, k_o, v_o) in enumerate(((qa0, ka0, va0), (qa1, ka1, va1), (qa2, ka2, va2))):
        d = DIL_CONFIGS[g][1]
        put(q_o, rope(proj(g * GROUP_WIDTH)), d)
        put(k_o, rope(proj(DIL_WIDTH + g * GROUP_WIDTH)), d)
        put(v_o, proj(2 * DIL_WIDTH + g * GROUP_WIDTH), d)
    base = 3 * DIL_WIDTH
    for s, o_ref in enumerate((qb_ref, kb_ref, vb_ref)):
        for c in range(FOX_WIDTH // MXU_COLS):
            o_ref[0, :, c * MXU_COLS:(c + 1) * MXU_COLS] = proj(
                base + s * FOX_WIDTH + c * MXU_COLS).astype(BF16)
    base += 3 * FOX_WIDTH
    for s, o_ref in enumerate((sga_ref, sgb_ref)):
        for c in range(D_MODEL // MXU_COLS):
            o_ref[0, :, c * MXU_COLS:(c + 1) * MXU_COLS] = jax.nn.sigmoid(
                proj(base + s * D_MODEL + c * MXU_COLS)).astype(BF16)

    z = lax.dot_general(wft_ref[...], h, _NT, preferred_element_type=F32) + bf_ref[...]
    logf = _log_sigmoid(z)
    lane = lax.broadcasted_iota(jnp.int32, (F_ROWS, LANES), 1)
    carry = jnp.where(i == 0, 0.0, carry_ref[...])
    for j in range(tm // LANES):
        c = logf[:, j * LANES:(j + 1) * LANES]
        k = 1
        while k < LANES:
            c = c + jnp.where(lane >= k, pltpu.roll(c, k, 1), 0.0)
            k *= 2
        c = c + carry
        ft_ref[0, :, j * LANES:(j + 1) * LANES] = c
        carry = jnp.broadcast_to(c[:, LANES - 1:LANES], (F_ROWS, LANES))
    carry_ref[...] = carry


def _rope_tables(seq):
    half = ROPE_DIM // 2
    inv_freq = np.power(ROPE_THETA, -np.arange(0, ROPE_DIM, 2, dtype=np.float64) / ROPE_DIM)
    ang = np.arange(seq, dtype=np.float64)[:, None] * inv_freq[None, :]
    cos = np.ones((seq, HEAD_DIM)); sa = np.zeros((seq, HEAD_DIM)); sb = np.zeros((seq, HEAD_DIM))
    cos[:, :half] = np.cos(ang); cos[:, half:ROPE_DIM] = np.cos(ang)
    sa[:, :half] = -np.sin(ang)
    sb[:, half:ROPE_DIM] = np.sin(ang)
    rep = LANES // HEAD_DIM
    return tuple(jnp.asarray(np.tile(t, (1, rep)), dtype=F32) for t in (cos, sa, sb))


def _in_proj(x, gain, w_main, wft, bf, tm=512):
    B, S, D = x.shape
    cos, sa, sb = _rope_tables(S)
    const = lambda shape: pl.BlockSpec(shape, lambda b, i: (0,) * len(shape))
    row = lambda w: pl.BlockSpec((1, tm, w), lambda b, i: (b, i, 0))
    tab = pl.BlockSpec((tm, LANES), lambda b, i: (i, 0))
    dils = [d for _, d in DIL_CONFIGS]
    res_shape = [jax.ShapeDtypeStruct((B, d, S // d, GROUP_WIDTH), BF16) for d in dils] * 3
    res_spec = [pl.BlockSpec((1, d, tm // d, GROUP_WIDTH), lambda b, i: (b, 0, i, 0))
                for d in dils] * 3
    out_shape = (res_shape
                 + [jax.ShapeDtypeStruct((B, S, FOX_WIDTH), BF16)] * 3
                 + [jax.ShapeDtypeStruct((B, S, D), BF16)] * 2
                 + [jax.ShapeDtypeStruct((B, F_ROWS, S), F32)])
    out_specs = (res_spec + [row(FOX_WIDTH)] * 3 + [row(D)] * 2
                 + [pl.BlockSpec((1, F_ROWS, tm), lambda b, i: (b, 0, i))])
    n_stage = 3 * sum(1 for d in dils if d > 1)
    return pl.pallas_call(
        functools.partial(_in_proj_kernel, tm=tm),
        grid=(B, S // tm),
        in_specs=[row(D), const((1, D)), const((D, MAIN_COLS)), const((F_ROWS, D)),
                  const((F_ROWS, 1)), tab, tab, tab],
        out_specs=out_specs,
        out_shape=out_shape,
        scratch_shapes=([pltpu.VMEM((F_ROWS, LANES), F32)]
                        + [pltpu.VMEM((GROUP_WIDTH // LANES, tm, LANES), F32)] * n_stage),
        compiler_params=pltpu.CompilerParams(
            dimension_semantics=("arbitrary", "arbitrary"), vmem_limit_bytes=VMEM_LIMIT),
        name="in_proj",
    )(x, gain, w_main, wft, bf, cos, sa, sb)


def _dilated_kernel(q_ref, k_ref, v_ref, o_ref, lse_ref, *, d, nb):
    lane = lax.broadcasted_iota(jnp.int32, (BLOCK, LANES), 1)
    lo_half = lane < HEAD_DIM

    def tiles(r, row0, k0, kw, first):
        qi = lax.broadcasted_iota(jnp.int32, (BLOCK, kw), 0)
        kj = lax.broadcasted_iota(jnp.int32, (BLOCK, kw), 1)
        valid = (kj <= qi) if first else ((kj >= qi) & (kj <= qi + BLOCK))
        for hp in range(GROUP_WIDTH // LANES):
            cols = slice(hp * LANES, (hp + 1) * LANES)
            q = q_ref[0, r, pl.ds(row0, BLOCK), cols]
            kc = k_ref[0, r, pl.ds(k0, kw), cols]
            vc = v_ref[0, r, pl.ds(k0, kw), cols]
            outs, lses = [], []
            for hh in range(2):
                qm = jnp.where(lo_half if hh == 0 else ~lo_half, q, jnp.zeros_like(q))
                s = lax.dot_general(qm, kc, _NT, preferred_element_type=F32)
                s = jnp.where(valid, s, NEG_INF)
                m = jnp.max(s, axis=1, keepdims=True)
                p = jnp.exp(s - m)
                den = jnp.sum(p, axis=1, keepdims=True)
                o = jnp.dot(p.astype(BF16), vc, preferred_element_type=F32)
                outs.append(o * (1.0 / den))
                lses.append(m + jnp.log(den))
            o_ref[0, r, pl.ds(row0, BLOCK), cols] = jnp.where(
                lo_half, outs[0], outs[1]).astype(BF16)
            lse_ref[0, r, pl.ds(row0, BLOCK), cols] = jnp.where(lo_half, lses[0], lses[1])

    def residue(r, _):
        tiles(r, 0, 0, BLOCK, True)
        if nb > 1:
            def body(n, _):
                row0 = pl.multiple_of(n * BLOCK, BLOCK)
                tiles(r, row0, row0 - BLOCK, 2 * BLOCK, False)
                return 0
            lax.fori_loop(1, nb, body, 0)
        return 0

    lax.fori_loop(0, d, residue, 0)


def _dilated_group(q, k, v):
    B, d, L, W = q.shape
    spec = pl.BlockSpec((1, d, L, W), lambda b: (b, 0, 0, 0))
    return pl.pallas_call(
        functools.partial(_dilated_kernel, d=d, nb=L // BLOCK),
        grid=(B,),
        in_specs=[spec, spec, spec],
        out_specs=[spec, spec],
        out_shape=[jax.ShapeDtypeStruct(q.shape, BF16), jax.ShapeDtypeStruct(q.shape, F32)],
        compiler_params=pltpu.CompilerParams(
            dimension_semantics=("arbitrary",), vmem_limit_bytes=VMEM_LIMIT),
        name=f"dilated_d{d}",
    )(q, k, v)


def _fox_kernel(q_ref, k_ref, v_ref, ft_ref, o_ref, *, tq, nblk):
    p = pl.program_id(1)
    lane = lax.broadcasted_iota(jnp.int32, (tq, LANES), 1)
    lo_half = lane < HEAD_DIM
    qi = lax.broadcasted_iota(jnp.int32, (tq, tq), 0)
    kj = lax.broadcasted_iota(jnp.int32, (tq, tq), 1)
    causal = kj <= qi
    state = {}
    for t in range(nblk):
        for i in range(t, nblk):
            j = i - t
            q = q_ref[0, i * tq:(i + 1) * tq, :]
            kc = k_ref[0, j * tq:(j + 1) * tq, :]
            vc = v_ref[0, j * tq:(j + 1) * tq, :]
            for hh in range(2):
                frow0 = (2 * p + hh) * nblk
                qm = jnp.where(lo_half if hh == 0 else ~lo_half, q, jnp.zeros_like(q))
                c = ft_ref[0, pl.ds(frow0 + i, 1), :][:, 0:1]
                bias = c - ft_ref[0, pl.ds(frow0 + j, 1), :]
                s = lax.dot_general(qm, kc, _NT, preferred_element_type=F32) + bias
                if t == 0:
                    s = jnp.where(causal, s, NEG_INF)
                    m = jnp.max(s, axis=1, keepdims=True)
                    pe = jnp.exp(s - m)
                    l = jnp.sum(pe, axis=1, keepdims=True)
                    acc = jnp.dot(pe.astype(BF16), vc, preferred_element_type=F32)
                else:
                    m_old, l_old, acc_old = state[i, hh]
                    m = jnp.maximum(m_old, jnp.max(s, axis=1, keepdims=True))
                    alpha = jnp.exp(m_old - m)
                    pe = jnp.exp(s - m)
                    l = alpha * l_old + jnp.sum(pe, axis=1, keepdims=True)
                    acc = alpha * acc_old + jnp.dot(pe.astype(BF16), vc,
                                                    preferred_element_type=F32)
                state[i, hh] = (m, l, acc)
            if j == 0:
                outs = [state[i, hh][2] * (1.0 / state[i, hh][1]) for hh in range(2)]
                o_ref[0, i * tq:(i + 1) * tq, :] = jnp.where(
                    lo_half, outs[0], outs[1]).astype(BF16)


def _fox(q, k, v, ft, tq=256):
    B, S, W = q.shape
    nblk = S // tq
    ftv = ft.reshape(B, F_ROWS * nblk, tq)
    spec = pl.BlockSpec((1, S, LANES), lambda b, p: (b, 0, p))
    return pl.pallas_call(
        functools.partial(_fox_kernel, tq=tq, nblk=nblk),
        grid=(B, W // LANES),
        in_specs=[spec, spec, spec,
                  pl.BlockSpec((1, F_ROWS * nblk, tq), lambda b, p: (b, 0, 0))],
        out_specs=spec,
        out_shape=jax.ShapeDtypeStruct((B, S, W), BF16),
        compiler_params=pltpu.CompilerParams(
            dimension_semantics=("arbitrary", "arbitrary"), vmem_limit_bytes=VMEM_LIMIT),
        name="fox",
    )(q, k, v, ftv)


def _mix_kernel(x_ref, o0, o1, o2, l0, l1, l2, ob_ref, sga_ref, sgb_ref,
                wa_ref, wb_ref, wo_ref, gain_ref, y_ref, *stage_refs, tm):
    stage = iter(stage_refs)

    def tokens(ref):
        d = ref.shape[1]
        if d == 1:
            return ref[0, 0].astype(F32)
        st = next(stage)
        for r in range(d):
            for c in range(GROUP_WIDTH // LANES):
                st[c, pl.ds(r, tm // d, stride=d), :] = ref[
                    0, r, :, c * LANES:(c + 1) * LANES].astype(F32)
        return jnp.concatenate([st[c] for c in range(GROUP_WIDTH // LANES)], axis=1)

    la, lb, lc = tokens(l0), tokens(l1), tokens(l2)
    mx = jnp.maximum(jnp.maximum(la, lb), lc)
    e0, e1, e2 = jnp.exp(la - mx), jnp.exp(lb - mx), jnp.exp(lc - mx)
    inv = 1.0 / (e0 + e1 + e2)
    oa = ((e0 * tokens(o0) + e1 * tokens(o1) + e2 * tokens(o2)) * inv).astype(BF16)
    a = jnp.dot(oa, wa_ref[...], preferred_element_type=F32)
    b = jnp.dot(ob_ref[0], wb_ref[...], preferred_element_type=F32)
    merged = (sga_ref[0].astype(F32) * a + sgb_ref[0].astype(F32) * b).astype(BF16)
    mix = jnp.dot(merged, wo_ref[...], preferred_element_type=F32)
    y_ref[0] = x_ref[0] + _rmsnorm(mix, gain_ref[...])


def _mix(x, outs, lses, ob, sga, sgb, wa, wb, wo, gain, tm=512):
    B, S, D = x.shape
    row = lambda w: pl.BlockSpec((1, tm, w), lambda b, i: (b, i, 0))
    const = lambda shape: pl.BlockSpec(shape, lambda b, i: (0, 0))
    res = lambda t: pl.BlockSpec((1, t.shape[1], tm // t.shape[1], GROUP_WIDTH),
                                 lambda b, i: (b, 0, i, 0))
    n_stage = 2 * sum(1 for t in outs if t.shape[1] > 1)
    return pl.pallas_call(
        functools.partial(_mix_kernel, tm=tm),
        grid=(B, S // tm),
        in_specs=([row(D)] + [res(t) for t in outs] + [res(t) for t in lses]
                  + [row(FOX_WIDTH), row(D), row(D),
                     const((GROUP_WIDTH, D)), const((FOX_WIDTH, D)), const((D, D)),
                     const((1, D))]),
        out_specs=row(D),
        out_shape=jax.ShapeDtypeStruct((B, S, D), F32),
        scratch_shapes=[pltpu.VMEM((GROUP_WIDTH // LANES, tm, LANES), F32)] * n_stage,
        compiler_params=pltpu.CompilerParams(
            dimension_semantics=("arbitrary", "arbitrary"), vmem_limit_bytes=VMEM_LIMIT),
        name="mix",
    )(x, *outs, *lses, ob, sga, sgb, wa, wb, wo, gain)


def _ffn_kernel(x_ref, g1_ref, g2_ref, wg_ref, wu_ref, wd_ref, o_ref):
    x = x_ref[...]
    h = _rmsnorm(x, g1_ref[...]).astype(BF16)
    acc = jnp.zeros(x.shape, F32)
    for c in range(D_FF // MXU_COLS):
        cols = slice(c * MXU_COLS, (c + 1) * MXU_COLS)
        g = jnp.dot(h, wg_ref[:, cols], preferred_element_type=F32)
        u = jnp.dot(h, wu_ref[:, cols], preferred_element_type=F32)
        a = (g * jax.nn.sigmoid(g) * u).astype(BF16)
        acc = acc + jnp.dot(a, wd_ref[cols, :], preferred_element_type=F32)
    o_ref[...] = x + _rmsnorm(acc, g2_ref[...])


def _ffn(x2, g1, g2, wg, wu, wd, tm=512):
    T, D = x2.shape
    row = pl.BlockSpec((tm, D), lambda i: (i, 0))
    const = lambda shape: pl.BlockSpec(shape, lambda i: (0, 0))
    return pl.pallas_call(
        _ffn_kernel,
        grid=(T // tm,),
        in_specs=[row, const((1, D)), const((1, D)), const((D, D_FF)), const((D, D_FF)),
                  const((D_FF, D))],
        out_specs=row,
        out_shape=jax.ShapeDtypeStruct((T, D), F32),
        compiler_params=pltpu.CompilerParams(
            dimension_semantics=("arbitrary",), vmem_limit_bytes=VMEM_LIMIT),
        name="ffn",
    )(x2, g1, g2, wg, wu, wd)


def kernel(x, w_in, w_proj_a, w_proj_b, w_out, b_forget, w_ffn_gate, w_ffn_up, w_ffn_down,
           norm_mix_pre, norm_mix_post, norm_ffn_pre, norm_ffn_post):
    B, S, D = x.shape
    scale = 1.0 / np.sqrt(HEAD_DIM)
    for layer in range(w_in.shape[0]):
        w = w_in[layer]
        o = 0
        secs = []
        for width in (DIL_WIDTH, DIL_WIDTH, DIL_WIDTH, FOX_WIDTH, FOX_WIDTH, FOX_WIDTH,
                      N_FOX_HEADS, D, D):
            secs.append(w[:, o:o + width])
            o += width
        wqa, wka, wva, wqb, wkb, wvb, wf, wga, wgb = secs
        w_main = jnp.concatenate([wqa * scale, wka, wva, wqb * scale, wkb, wvb, wga, wgb],
                                 axis=1).astype(BF16)
        wft = jnp.zeros((F_ROWS, D), F32).at[:N_FOX_HEADS].set(wf.T).astype(BF16)
        bf = jnp.zeros((F_ROWS, 1), F32).at[:N_FOX_HEADS, 0].set(b_forget[layer])

        (qa0, qa1, qa2, ka0, ka1, ka2, va0, va1, va2, qb, kb, vb, sga, sgb, ft) = _in_proj(
            x, norm_mix_pre[layer][None, :], w_main, wft, bf)

        outs, lses = [], []
        for q, k, v in ((qa0, ka0, va0), (qa1, ka1, va1), (qa2, ka2, va2)):
            o_g, l_g = _dilated_group(q, k, v)
            outs.append(o_g)
            lses.append(l_g)
        ob = _fox(qb, kb, vb, ft)

        x = _mix(x, outs, lses, ob, sga, sgb,
                 w_proj_a[layer].astype(BF16), w_proj_b[layer].astype(BF16),
                 w_out[layer].astype(BF16), norm_mix_post[layer][None, :])
        x = _ffn(x.reshape(B * S, D), norm_ffn_pre[layer][None, :], norm_ffn_post[layer][None, :],
                 w_ffn_gate[layer].astype(BF16), w_ffn_up[layer].astype(BF16),
                 w_ffn_down[layer].astype(BF16)).reshape(B, S, D)
    return x
```

```python
import functools

import numpy as np
import jax
import jax.numpy as jnp
from jax import lax
from jax.experimental import pallas as pl
from jax.experimental.pallas import tpu as pltpu

D_MODEL = 1024
HEAD_DIM = 64
DIL_CONFIGS = ((128, 1), (512, 4), (2048, 16))
N_DIL_GROUPS = 3
GROUP_WIDTH = 256
N_FOX_HEADS = 8
FOX_WIDTH = 512
BLOCK = 128
ROPE_THETA = 500000.0
ROPE_DIM = 16
D_FF = 2816
EPS = 1e-6
NEG_INF = -1e30

DIL_WIDTH = N_DIL_GROUPS * GROUP_WIDTH
MAIN_COLS = 3 * DIL_WIDTH + 3 * FOX_WIDTH + 2 * D_MODEL
F_ROWS = 16
LANES = 128
MXU_COLS = 256
VMEM_LIMIT = 56 * 1024 * 1024

F32 = jnp.float32
BF16 = jnp.bfloat16
_NT = (((1,), (1,)), ((), ()))


def _rmsnorm(x, gain):
    return x * lax.rsqrt(jnp.mean(x * x, axis=-1, keepdims=True) + EPS) * gain


def _log_sigmoid(z):
    return jnp.minimum(z, 0.0) - jnp.log1p(jnp.exp(-jnp.abs(z)))


def _in_proj_kernel(x_ref, gain_ref, w_ref, wft_ref, bf_ref, cos_ref, sa_ref, sb_ref,
                    qa0, qa1, qa2, ka0, ka1, ka2, va0, va1, va2,
                    qb_ref, kb_ref, vb_ref, sga_ref, sgb_ref, ft_ref, carry_ref, *stage_refs, tm):
    i = pl.program_id(1)
    h = _rmsnorm(x_ref[0], gain_ref[...]).astype(BF16)

    def proj(c0, width=MXU_COLS):
        return jnp.dot(h, w_ref[:, c0:c0 + width], preferred_element_type=F32)

    cos, sa, sb = cos_ref[...], sa_ref[...], sb_ref[...]

    def rope(y):
        halves = []
        for c in range(MXU_COLS // LANES):
            yc = y[:, c * LANES:(c + 1) * LANES]
            halves.append(yc * cos + pltpu.roll(yc, LANES - ROPE_DIM // 2, 1) * sa
                          + pltpu.roll(yc, ROPE_DIM // 2, 1) * sb)
        return jnp.concatenate(halves, axis=1)

    stage = iter(stage_refs)

    def put(o_ref, y, d):
        if d == 1:
            o_ref[0, 0] = y.astype(BF16)
            return
        st = next(stage)
        for c in range(GROUP_WIDTH // LANES):
            st[c] = y[:, c * LANES:(c + 1) * LANES]
        for r in range(d):
            for c in range(GROUP_WIDTH // LANES):
                o_ref[0, r, :, c * LANES:(c + 1) * LANES] = st[
                    c, pl.ds(r, tm // d, stride=d), :].astype(BF16)

    for g, (q_o, k_o, v_o) in enumerate(((qa0, ka0, va0), (qa1, ka1, va1), (qa2, ka2, va2))):
        d = DIL_CONFIGS[g][1]
        put(q_o, rope(proj(g * GROUP_WIDTH)), d)
        put(k_o, rope(proj(DIL_WIDTH + g * GROUP_WIDTH)), d)
        put(v_o, proj(2 * DIL_WIDTH + g * GROUP_WIDTH), d)
    base = 3 * DIL_WIDTH
    for s, o_ref in enumerate((qb_ref, kb_ref, vb_ref)):
        for c in range(FOX_WIDTH // MXU_COLS):
            o_ref[0, :, c * MXU_COLS:(c + 1) * MXU_COLS] = proj(
                base + s * FOX_WIDTH + c * MXU_COLS).astype(BF16)
    base += 3 * FOX_WIDTH
    for s, o_ref in enumerate((sga_ref, sgb_ref)):
        for c in range(D_MODEL // MXU_COLS):
            o_ref[0, :, c * MXU_COLS:(c + 1) * MXU_COLS] = jax.nn.sigmoid(
                proj(base + s * D_MODEL + c * MXU_COLS)).astype(BF16)

    z = lax.dot_general(wft_ref[...], h, _NT, preferred_element_type=F32) + bf_ref[...]
    logf = _log_sigmoid(z)
    lane = lax.broadcasted_iota(jnp.int32, (F_ROWS, LANES), 1)
    carry = jnp.where(i == 0, 0.0, carry_ref[...])
    for j in range(tm // LANES):
        c = logf[:, j * LANES:(j + 1) * LANES]
        k = 1
        while k < LANES:
            c = c + jnp.where(lane >= k, pltpu.roll(c, k, 1), 0.0)
            k *= 2
        c = c + carry
        ft_ref[0, :, j * LANES:(j + 1) * LANES] = c
        carry = jnp.broadcast_to(c[:, LANES - 1:LANES], (F_ROWS, LANES))
    carry_ref[...] = carry


def _rope_tables(seq):
    half = ROPE_DIM // 2
    inv_freq = np.power(ROPE_THETA, -np.arange(0, ROPE_DIM, 2, dtype=np.float64) / ROPE_DIM)
    ang = np.arange(seq, dtype=np.float64)[:, None] * inv_freq[None, :]
    cos = np.ones((seq, HEAD_DIM)); sa = np.zeros((seq, HEAD_DIM)); sb = np.zeros((seq, HEAD_DIM))
    cos[:, :half] = np.cos(ang); cos[:, half:ROPE_DIM] = np.cos(ang)
    sa[:, :half] = -np.sin(ang)
    sb[:, half:ROPE_DIM] = np.sin(ang)
    rep = LANES // HEAD_DIM
    return tuple(jnp.asarray(np.tile(t, (1, rep)), dtype=F32) for t in (cos, sa, sb))


def _in_proj(x, gain, w_main, wft, bf, tm=512):
    B, S, D = x.shape
    cos, sa, sb = _rope_tables(S)
    const = lambda shape: pl.BlockSpec(shape, lambda b, i: (0,) * len(shape))
    row = lambda w: pl.BlockSpec((1, tm, w), lambda b, i: (b, i, 0))
    tab = pl.BlockSpec((tm, LANES), lambda b, i: (i, 0))
    dils = [d for _, d in DIL_CONFIGS]
    res_shape = [jax.ShapeDtypeStruct((B, d, S // d, GROUP_WIDTH), BF16) for d in dils] * 3
    res_spec = [pl.BlockSpec((1, d, tm // d, GROUP_WIDTH), lambda b, i: (b, 0, i, 0))
                for d in dils] * 3
    out_shape = (res_shape
                 + [jax.ShapeDtypeStruct((B, S, FOX_WIDTH), BF16)] * 3
                 + [jax.ShapeDtypeStruct((B, S, D), BF16)] * 2
                 + [jax.ShapeDtypeStruct((B, F_ROWS, S), F32)])
    out_specs = (res_spec + [row(FOX_WIDTH)] * 3 + [row(D)] * 2
                 + [pl.BlockSpec((1, F_ROWS, tm), lambda b, i: (b, 0, i))])
    n_stage = 3 * sum(1 for d in dils if d > 1)
    return pl.pallas_call(
        functools.partial(_in_proj_kernel, tm=tm),
        grid=(B, S // tm),
        in_specs=[row(D), const((1, D)), const((D, MAIN_COLS)), const((F_ROWS, D)),
                  const((F_ROWS, 1)), tab, tab, tab],
        out_specs=out_specs,
        out_shape=out_shape,
        scratch_shapes=([pltpu.VMEM((F_ROWS, LANES), F32)]
                        + [pltpu.VMEM((GROUP_WIDTH // LANES, tm, LANES), F32)] * n_stage),
        compiler_params=pltpu.CompilerParams(
            dimension_semantics=("arbitrary", "arbitrary"), vmem_limit_bytes=VMEM_LIMIT),
        name="in_proj",
    )(x, gain, w_main, wft, bf, cos, sa, sb)


def _dilated_kernel(q_ref, k_ref, v_ref, o_ref, lse_ref, *, d, nb):
    lane = lax.broadcasted_iota(jnp.int32, (BLOCK, LANES), 1)
    lo_half = lane < HEAD_DIM

    kw = 2 * BLOCK
    qi = lax.broadcasted_iota(jnp.int32, (BLOCK, kw), 0)
    kj = lax.broadcasted_iota(jnp.int32, (BLOCK, kw), 1)
    mask_lead = kj <= qi
    mask_band = (kj >= qi) & (kj <= qi + BLOCK)
    mask_tail = (kj >= BLOCK) & (kj - BLOCK <= qi)

    def tiles(r, n):
        rows = slice(n * BLOCK, (n + 1) * BLOCK)
        for hp in range(GROUP_WIDTH // LANES):
            cols = slice(hp * LANES, (hp + 1) * LANES)
            q = q_ref[0, r, rows, cols]
            if nb == 1:
                r0 = r - r % 2
                kc = k_ref[0, r0:r0 + 2, :, cols].reshape(kw, LANES)
                vc = v_ref[0, r0:r0 + 2, :, cols].reshape(kw, LANES)
                valid = mask_lead if r % 2 == 0 else mask_tail
            else:
                k0 = max(n - 1, 0) * BLOCK
                kc = k_ref[0, r, k0:k0 + kw, cols]
                vc = v_ref[0, r, k0:k0 + kw, cols]
                valid = mask_lead if n == 0 else mask_band
            outs, lses = [], []
            for hh in range(2):
                qm = jnp.where(lo_half if hh == 0 else ~lo_half, q, jnp.zeros_like(q))
                s = lax.dot_general(qm, kc, _NT, preferred_element_type=F32)
                s = jnp.where(valid, s, NEG_INF)
                m = jnp.max(s, axis=1, keepdims=True)
                p = jnp.exp(s - m)
                den = jnp.sum(p, axis=1, keepdims=True)
                o = jnp.dot(p.astype(BF16), vc, preferred_element_type=F32)
                outs.append(o * (1.0 / den))
                lses.append(m + jnp.log(den))
            o_ref[0, r, rows, cols] = jnp.where(lo_half, outs[0], outs[1]).astype(BF16)
            lse_ref[0, r, rows, cols] = jnp.where(lo_half, lses[0], lses[1])

    for r in range(d):
        for n in range(nb):
            tiles(r, n)


def _dilated_group(q, k, v):
    B, d, L, W = q.shape
    spec = pl.BlockSpec((1, d, L, W), lambda b: (b, 0, 0, 0))
    return pl.pallas_call(
        functools.partial(_dilated_kernel, d=d, nb=L // BLOCK),
        grid=(B,),
        in_specs=[spec, spec, spec],
        out_specs=[spec, spec],
        out_shape=[jax.ShapeDtypeStruct(q.shape, BF16), jax.ShapeDtypeStruct(q.shape, F32)],
        compiler_params=pltpu.CompilerParams(
            dimension_semantics=("arbitrary",), vmem_limit_bytes=VMEM_LIMIT),
        name=f"dilated_d{d}",
    )(q, k, v)


def _fox_kernel(q_ref, k_ref, v_ref, ft_ref, o_ref, *, tq, nblk):
    p = pl.program_id(1)
    lane = lax.broadcasted_iota(jnp.int32, (tq, LANES), 1)
    lo_half = lane < HEAD_DIM
    qi = lax.broadcasted_iota(jnp.int32, (tq, tq), 0)
    kj = lax.broadcasted_iota(jnp.int32, (tq, tq), 1)
    causal = kj <= qi
    state = {}
    for t in range(nblk):
        for i in range(t, nblk):
            j = i - t
            q = q_ref[0, i * tq:(i + 1) * tq, :]
            kc = k_ref[0, j * tq:(j + 1) * tq, :]
            vc = v_ref[0, j * tq:(j + 1) * tq, :]
            for hh in range(2):
                frow0 = (2 * p + hh) * nblk
                qm = jnp.where(lo_half if hh == 0 else ~lo_half, q, jnp.zeros_like(q))
                c = ft_ref[0, pl.ds(frow0 + i, 1), :][:, 0:1]
                bias = c - ft_ref[0, pl.ds(frow0 + j, 1), :]
                s = lax.dot_general(qm, kc, _NT, preferred_element_type=F32) + bias
                if t == 0:
                    s = jnp.where(causal, s, NEG_INF)
                    m = jnp.max(s, axis=1, keepdims=True)
                    pe = jnp.exp(s - m)
                    l = jnp.sum(pe, axis=1, keepdims=True)
                    acc = jnp.dot(pe.astype(BF16), vc, preferred_element_type=F32)
                else:
                    m_old, l_old, acc_old = state[i, hh]
                    m = jnp.maximum(m_old, jnp.max(s, axis=1, keepdims=True))
                    alpha = jnp.exp(m_old - m)
                    pe = jnp.exp(s - m)
                    l = alpha * l_old + jnp.sum(pe, axis=1, keepdims=True)
                    acc = alpha * acc_old + jnp.dot(pe.astype(BF16), vc,
                                                    preferred_element_type=F32)
                state[i, hh] = (m, l, acc)
            if j == 0:
                outs = [state[i, hh][2] * (1.0 / state[i, hh][1]) for hh in range(2)]
                o_ref[0, i * tq:(i + 1) * tq, :] = jnp.where(
                    lo_half, outs[0], outs[1]).astype(BF16)


def _fox(q, k, v, ft, tq=256):
    B, S, W = q.shape
    nblk = S // tq
    ftv = ft.reshape(B, F_ROWS * nblk, tq)
    spec = pl.BlockSpec((1, S, LANES), lambda b, p: (b, 0, p))
    return pl.pallas_call(
        functools.partial(_fox_kernel, tq=tq, nblk=nblk),
        grid=(B, W // LANES),
        in_specs=[spec, spec, spec,
                  pl.BlockSpec((1, F_ROWS * nblk, tq), lambda b, p: (b, 0, 0))],
        out_specs=spec,
        out_shape=jax.ShapeDtypeStruct((B, S, W), BF16),
        compiler_params=pltpu.CompilerParams(
            dimension_semantics=("arbitrary", "arbitrary"), vmem_limit_bytes=VMEM_LIMIT),
        name="fox",
    )(q, k, v, ftv)


def _mix_kernel(x_ref, o0, o1, o2, l0, l1, l2, ob_ref, sga_ref, sgb_ref,
                wa_ref, wb_ref, wo_ref, gain_ref, y_ref, *stage_refs, tm):
    stage = iter(stage_refs)

    def tokens(ref):
        d = ref.shape[1]
        if d == 1:
            return ref[0, 0].astype(F32)
        st = next(stage)
        for r in range(d):
            for c in range(GROUP_WIDTH // LANES):
                st[c, pl.ds(r, tm // d, stride=d), :] = ref[
                    0, r, :, c * LANES:(c + 1) * LANES].astype(F32)
        return jnp.concatenate([st[c] for c in range(GROUP_WIDTH // LANES)], axis=1)

    la, lb, lc = tokens(l0), tokens(l1), tokens(l2)
    mx = jnp.maximum(jnp.maximum(la, lb), lc)
    e0, e1, e2 = jnp.exp(la - mx), jnp.exp(lb - mx), jnp.exp(lc - mx)
    inv = 1.0 / (e0 + e1 + e2)
    oa = ((e0 * tokens(o0) + e1 * tokens(o1) + e2 * tokens(o2)) * inv).astype(BF16)
    a = jnp.dot(oa, wa_ref[...], preferred_element_type=F32)
    b = jnp.dot(ob_ref[0], wb_ref[...], preferred_element_type=F32)
    merged = (sga_ref[0].astype(F32) * a + sgb_ref[0].astype(F32) * b).astype(BF16)
    mix = jnp.dot(merged, wo_ref[...], preferred_element_type=F32)
    y_ref[0] = x_ref[0] + _rmsnorm(mix, gain_ref[...])


def _mix(x, outs, lses, ob, sga, sgb, wa, wb, wo, gain, tm=512):
    B, S, D = x.shape
    row = lambda w: pl.BlockSpec((1, tm, w), lambda b, i: (b, i, 0))
    const = lambda shape: pl.BlockSpec(shape, lambda b, i: (0, 0))
    res = lambda t: pl.BlockSpec((1, t.shape[1], tm // t.shape[1], GROUP_WIDTH),
                                 lambda b, i: (b, 0, i, 0))
    n_stage = 2 * sum(1 for t in outs if t.shape[1] > 1)
    return pl.pallas_call(
        functools.partial(_mix_kernel, tm=tm),
        grid=(B, S // tm),
        in_specs=([row(D)] + [res(t) for t in outs] + [res(t) for t in lses]
                  + [row(FOX_WIDTH), row(D), row(D),
                     const((GROUP_WIDTH, D)), const((FOX_WIDTH, D)), const((D, D)),
                     const((1, D))]),
        out_specs=row(D),
        out_shape=jax.ShapeDtypeStruct((B, S, D), F32),
        scratch_shapes=[pltpu.VMEM((GROUP_WIDTH // LANES, tm, LANES), F32)] * n_stage,
        compiler_params=pltpu.CompilerParams(
            dimension_semantics=("arbitrary", "arbitrary"), vmem_limit_bytes=VMEM_LIMIT),
        name="mix",
    )(x, *outs, *lses, ob, sga, sgb, wa, wb, wo, gain)


def _ffn_kernel(x_ref, g1_ref, g2_ref, wg_ref, wu_ref, wd_ref, o_ref):
    x = x_ref[...]
    h = _rmsnorm(x, g1_ref[...]).astype(BF16)
    acc = jnp.zeros(x.shape, F32)
    for c in range(D_FF // MXU_COLS):
        cols = slice(c * MXU_COLS, (c + 1) * MXU_COLS)
        g = jnp.dot(h, wg_ref[:, cols], preferred_element_type=F32)
        u = jnp.dot(h, wu_ref[:, cols], preferred_element_type=F32)
        a = (g * jax.nn.sigmoid(g) * u).astype(BF16)
        acc = acc + jnp.dot(a, wd_ref[cols, :], preferred_element_type=F32)
    o_ref[...] = x + _rmsnorm(acc, g2_ref[...])


def _ffn(x2, g1, g2, wg, wu, wd, tm=512):
    T, D = x2.shape
    row = pl.BlockSpec((tm, D), lambda i: (i, 0))
    const = lambda shape: pl.BlockSpec(shape, lambda i: (0, 0))
    return pl.pallas_call(
        _ffn_kernel,
        grid=(T // tm,),
        in_specs=[row, const((1, D)), const((1, D)), const((D, D_FF)), const((D, D_FF)),
                  const((D_FF, D))],
        out_specs=row,
        out_shape=jax.ShapeDtypeStruct((T, D), F32),
        compiler_params=pltpu.CompilerParams(
            dimension_semantics=("arbitrary",), vmem_limit_bytes=VMEM_LIMIT),
        name="ffn",
    )(x2, g1, g2, wg, wu, wd)


def kernel(x, w_in, w_proj_a, w_proj_b, w_out, b_forget, w_ffn_gate, w_ffn_up, w_ffn_down,
           norm_mix_pre, norm_mix_post, norm_ffn_pre, norm_ffn_post):
    B, S, D = x.shape
    scale = 1.0 / np.sqrt(HEAD_DIM)
    for layer in range(w_in.shape[0]):
        w = w_in[layer]
        o = 0
        secs = []
        for width in (DIL_WIDTH, DIL_WIDTH, DIL_WIDTH, FOX_WIDTH, FOX_WIDTH, FOX_WIDTH,
                      N_FOX_HEADS, D, D):
            secs.append(w[:, o:o + width])
            o += width
        wqa, wka, wva, wqb, wkb, wvb, wf, wga, wgb = secs
        w_main = jnp.concatenate([wqa * scale, wka, wva, wqb * scale, wkb, wvb, wga, wgb],
                                 axis=1).astype(BF16)
        wft = jnp.zeros((F_ROWS, D), F32).at[:N_FOX_HEADS].set(wf.T).astype(BF16)
        bf = jnp.zeros((F_ROWS, 1), F32).at[:N_FOX_HEADS, 0].set(b_forget[layer])

        (qa0, qa1, qa2, ka0, ka1, ka2, va0, va1, va2, qb, kb, vb, sga, sgb, ft) = _in_proj(
            x, norm_mix_pre[layer][None, :], w_main, wft, bf)

        outs, lses = [], []
        for q, k, v in ((qa0, ka0, va0), (qa1, ka1, va1), (qa2, ka2, va2)):
            o_g, l_g = _dilated_group(q, k, v)
            outs.append(o_g)
            lses.append(l_g)
        ob = _fox(qb, kb, vb, ft)

        x = _mix(x, outs, lses, ob, sga, sgb,
                 w_proj_a[layer].astype(BF16), w_proj_b[layer].astype(BF16),
                 w_out[layer].astype(BF16), norm_mix_post[layer][None, :])
        x = _ffn(x.reshape(B * S, D), norm_ffn_pre[layer][None, :], norm_ffn_post[layer][None, :],
                 w_ffn_gate[layer].astype(BF16), w_ffn_up[layer].astype(BF16),
                 w_ffn_down[layer].astype(BF16)).reshape(B, S, D)
    return x
```

```python
import functools

import numpy as np
import jax
import jax.numpy as jnp
from jax import lax
from jax.experimental import pallas as pl
from jax.experimental.pallas import tpu as pltpu

D_MODEL = 1024
HEAD_DIM = 64
DIL_CONFIGS = ((128, 1), (512, 4), (2048, 16))
N_DIL_GROUPS = 3
GROUP_WIDTH = 256
N_FOX_HEADS = 8
FOX_WIDTH = 512
BLOCK = 128
ROPE_THETA = 500000.0
ROPE_DIM = 16
D_FF = 2816
EPS = 1e-6
NEG_INF = -1e30

DIL_WIDTH = N_DIL_GROUPS * GROUP_WIDTH
MAIN_COLS = 3 * DIL_WIDTH + 3 * FOX_WIDTH + 2 * D_MODEL
F_ROWS = 16
LANES = 128
MXU_COLS = 256
VMEM_LIMIT = 56 * 1024 * 1024
VT_ROWS = HEAD_DIM + 16
LOG2E = 1.4426950408889634

F32 = jnp.float32
BF16 = jnp.bfloat16
_NT = (((1,), (1,)), ((), ()))


def _rmsnorm(x, gain):
    return x * lax.rsqrt(jnp.mean(x * x, axis=-1, keepdims=True) + EPS) * gain


def _log_sigmoid(z):
    return jnp.minimum(z, 0.0) - jnp.log1p(jnp.exp(-jnp.abs(z)))


def _in_proj_kernel(x_ref, gain_ref, w_ref, wft_ref, bf_ref, cos_ref, sa_ref, sb_ref,
                    qa0, qa1, qa2, ka0, ka1, ka2, va0, va1, va2,
                    qb_ref, kb_ref, vb_ref, sga_ref, sgb_ref, ft_ref, carry_ref, *stage_refs, tm):
    i = pl.program_id(1)
    h = _rmsnorm(x_ref[0], gain_ref[...]).astype(BF16)

    def proj(c0, width=MXU_COLS):
        return jnp.dot(h, w_ref[:, c0:c0 + width], preferred_element_type=F32)

    cos, sa, sb = cos_ref[...], sa_ref[...], sb_ref[...]

    def rope(y):
        halves = []
        for c in range(MXU_COLS // LANES):
            yc = y[:, c * LANES:(c + 1) * LANES]
            halves.append(yc * cos + pltpu.roll(yc, LANES - ROPE_DIM // 2, 1) * sa
                          + pltpu.roll(yc, ROPE_DIM // 2, 1) * sb)
        return jnp.concatenate(halves, axis=1)

    stage = iter(stage_refs)

    def put(o_ref, y, d):
        if d == 1:
            o_ref[0, 0] = y.astype(BF16)
            return
        st = next(stage)
        for c in range(GROUP_WIDTH // LANES):
            st[c] = y[:, c * LANES:(c + 1) * LANES]
        for r in range(d):
            for c in range(GROUP_WIDTH // LANES):
                o_ref[0, r, :, c * LANES:(c + 1) * LANES] = st[
                    c, pl.ds(r, tm // d, stride=d), :].astype(BF16)

    for g, (q_o, k_o, v_o) in enumerate(((qa0, ka0, va0), (qa1, ka1, va1), (qa2, ka2, va2))):
        d = DIL_CONFIGS[g][1]
        put(q_o, rope(proj(g * GROUP_WIDTH)), d)
        put(k_o, rope(proj(DIL_WIDTH + g * GROUP_WIDTH)), d)
        put(v_o, proj(2 * DIL_WIDTH + g * GROUP_WIDTH), d)
    base = 3 * DIL_WIDTH
    for s, o_ref in enumerate((qb_ref, kb_ref, vb_ref)):
        for c in range(FOX_WIDTH // MXU_COLS):
            o_ref[0, :, c * MXU_COLS:(c + 1) * MXU_COLS] = proj(
                base + s * FOX_WIDTH + c * MXU_COLS).astype(BF16)
    base += 3 * FOX_WIDTH
    for s, o_ref in enumerate((sga_ref, sgb_ref)):
        for c in range(D_MODEL // MXU_COLS):
            o_ref[0, :, c * MXU_COLS:(c + 1) * MXU_COLS] = jax.nn.sigmoid(
                proj(base + s * D_MODEL + c * MXU_COLS)).astype(BF16)

    z = lax.dot_general(wft_ref[...], h, _NT, preferred_element_type=F32) + bf_ref[...]
    logf = _log_sigmoid(z)
    lane = lax.broadcasted_iota(jnp.int32, (F_ROWS, LANES), 1)
    carry = jnp.where(i == 0, 0.0, carry_ref[...])
    for j in range(tm // LANES):
        c = logf[:, j * LANES:(j + 1) * LANES]
        k = 1
        while k < LANES:
            c = c + jnp.where(lane >= k, pltpu.roll(c, k, 1), 0.0)
            k *= 2
        c = c + carry
        ft_ref[0, :, j * LANES:(j + 1) * LANES] = c
        carry = jnp.broadcast_to(c[:, LANES - 1:LANES], (F_ROWS, LANES))
    carry_ref[...] = carry


def _rope_tables(seq):
    half = ROPE_DIM // 2
    inv_freq = np.power(ROPE_THETA, -np.arange(0, ROPE_DIM, 2, dtype=np.float64) / ROPE_DIM)
    ang = np.arange(seq, dtype=np.float64)[:, None] * inv_freq[None, :]
    cos = np.ones((seq, HEAD_DIM)); sa = np.zeros((seq, HEAD_DIM)); sb = np.zeros((seq, HEAD_DIM))
    cos[:, :half] = np.cos(ang); cos[:, half:ROPE_DIM] = np.cos(ang)
    sa[:, :half] = -np.sin(ang)
    sb[:, half:ROPE_DIM] = np.sin(ang)
    rep = LANES // HEAD_DIM
    return tuple(jnp.asarray(np.tile(t, (1, rep)), dtype=F32) for t in (cos, sa, sb))


def _in_proj(x, gain, w_main, wft, bf, tm=512):
    B, S, D = x.shape
    cos, sa, sb = _rope_tables(S)
    const = lambda shape: pl.BlockSpec(shape, lambda b, i: (0,) * len(shape))
    row = lambda w: pl.BlockSpec((1, tm, w), lambda b, i: (b, i, 0))
    tab = pl.BlockSpec((tm, LANES), lambda b, i: (i, 0))
    dils = [d for _, d in DIL_CONFIGS]
    res_shape = [jax.ShapeDtypeStruct((B, d, S // d, GROUP_WIDTH), BF16) for d in dils] * 3
    res_spec = [pl.BlockSpec((1, d, tm // d, GROUP_WIDTH), lambda b, i: (b, 0, i, 0))
                for d in dils] * 3
    out_shape = (res_shape
                 + [jax.ShapeDtypeStruct((B, S, FOX_WIDTH), BF16)] * 3
                 + [jax.ShapeDtypeStruct((B, S, D), BF16)] * 2
                 + [jax.ShapeDtypeStruct((B, F_ROWS, S), F32)])
    out_specs = (res_spec + [row(FOX_WIDTH)] * 3 + [row(D)] * 2
                 + [pl.BlockSpec((1, F_ROWS, tm), lambda b, i: (b, 0, i))])
    n_stage = 3 * sum(1 for d in dils if d > 1)
    return pl.pallas_call(
        functools.partial(_in_proj_kernel, tm=tm),
        grid=(B, S // tm),
        in_specs=[row(D), const((1, D)), const((D, MAIN_COLS)), const((F_ROWS, D)),
                  const((F_ROWS, 1)), tab, tab, tab],
        out_specs=out_specs,
        out_shape=out_shape,
        scratch_shapes=([pltpu.VMEM((F_ROWS, LANES), F32)]
                        + [pltpu.VMEM((GROUP_WIDTH // LANES, tm, LANES), F32)] * n_stage),
        compiler_params=pltpu.CompilerParams(
            dimension_semantics=("arbitrary", "arbitrary"), vmem_limit_bytes=VMEM_LIMIT),
        name="in_proj",
    )(x, gain, w_main, wft, bf, cos, sa, sb)


def _dilated_kernel(q_ref, k_ref, v_ref, o_ref, lse_ref, *, d, nb):
    lane = lax.broadcasted_iota(jnp.int32, (BLOCK, LANES), 1)
    lo_half = lane < HEAD_DIM

    kw = 2 * BLOCK
    qi = lax.broadcasted_iota(jnp.int32, (BLOCK, kw), 0)
    kj = lax.broadcasted_iota(jnp.int32, (BLOCK, kw), 1)
    mask_lead = kj <= qi
    mask_band = (kj >= qi) & (kj <= qi + BLOCK)
    mask_tail = (kj >= BLOCK) & (kj - BLOCK <= qi)

    def tiles(r, n):
        rows = slice(n * BLOCK, (n + 1) * BLOCK)
        for hp in range(GROUP_WIDTH // LANES):
            cols = slice(hp * LANES, (hp + 1) * LANES)
            q = q_ref[0, r, rows, cols]
            if nb == 1:
                r0 = r - r % 2
                kc = k_ref[0, r0:r0 + 2, :, cols].reshape(kw, LANES)
                vc = v_ref[0, r0:r0 + 2, :, cols].reshape(kw, LANES)
                valid = mask_lead if r % 2 == 0 else mask_tail
            else:
                k0 = max(n - 1, 0) * BLOCK
                kc = k_ref[0, r, k0:k0 + kw, cols]
                vc = v_ref[0, r, k0:k0 + kw, cols]
                valid = mask_lead if n == 0 else mask_band
            outs, lses = [], []
            for hh in range(2):
                qm = jnp.where(lo_half if hh == 0 else ~lo_half, q, jnp.zeros_like(q))
                s = lax.dot_general(qm, kc, _NT, preferred_element_type=F32)
                s = jnp.where(valid, s, NEG_INF)
                m = jnp.max(s, axis=1, keepdims=True)
                p = jnp.exp(s - m)
                den = jnp.sum(p, axis=1, keepdims=True)
                o = jnp.dot(p.astype(BF16), vc, preferred_element_type=F32)
                outs.append(o * (1.0 / den))
                lses.append(m + jnp.log(den))
            o_ref[0, r, rows, cols] = jnp.where(lo_half, outs[0], outs[1]).astype(BF16)
            lse_ref[0, r, rows, cols] = jnp.where(lo_half, lses[0], lses[1])

    for r in range(d):
        for n in range(nb):
            tiles(r, n)


def _dilated_group(q, k, v):
    B, d, L, W = q.shape
    spec = pl.BlockSpec((1, d, L, W), lambda b: (b, 0, 0, 0))
    return pl.pallas_call(
        functools.partial(_dilated_kernel, d=d, nb=L // BLOCK),
        grid=(B,),
        in_specs=[spec, spec, spec],
        out_specs=[spec, spec],
        out_shape=[jax.ShapeDtypeStruct(q.shape, BF16), jax.ShapeDtypeStruct(q.shape, F32)],
        compiler_params=pltpu.CompilerParams(
            dimension_semantics=("arbitrary",), vmem_limit_bytes=VMEM_LIMIT),
        name=f"dilated_d{d}",
    )(q, k, v)


def _fox_kernel(q_ref, k_ref, v_ref, ft_ref, o_ref, vt_ref, fcol_ref, *, tq, nblk, lookahead):
    p = pl.program_id(1)
    seq = nblk * tq
    lane = lax.broadcasted_iota(jnp.int32, (tq, LANES), 1)
    lo_half = lane < HEAD_DIM
    key = lax.broadcasted_iota(jnp.int32, (tq, tq), 0)
    qry = lax.broadcasted_iota(jnp.int32, (tq, tq), 1)
    causal = key <= qry

    for c in range(seq // tq):
        cs = slice(c * tq, (c + 1) * tq)
        vt = v_ref[0, cs, :].astype(F32).T.astype(BF16)
        for hh in range(2):
            vt_ref[hh, :HEAD_DIM, cs] = vt[hh * HEAD_DIM:(hh + 1) * HEAD_DIM]
    for hh in range(2):
        vt_ref[hh, HEAD_DIM:, :] = jnp.ones((VT_ROWS - HEAD_DIM, seq), BF16)
    frows = [ft_ref[0, pl.ds(2 * p + hh, 1), :] * LOG2E for hh in range(2)]
    for hh in range(2):
        for c in range(seq // LANES):
            cs = slice(c * LANES, (c + 1) * LANES)
            fcol_ref[hh, cs, :] = jnp.broadcast_to(frows[hh][:, cs], (LANES, LANES)).T

    def scores(t, i, hh):
        j = i - t
        ks = slice(j * tq, (j + 1) * tq)
        q = q_ref[0, i * tq:(i + 1) * tq, :]
        qm = jnp.where(lo_half if hh == 0 else ~lo_half, q, jnp.zeros_like(q))
        c0 = frows[hh][:, i * tq:i * tq + 1]
        bias = c0 - fcol_ref[hh, ks, :]
        st = lax.dot_general(k_ref[0, ks, :], qm, _NT, preferred_element_type=F32)
        st = st + jnp.concatenate([bias] * (tq // LANES), axis=1)
        return jnp.where(causal, st, NEG_INF) if t == 0 else st

    state = {}

    def update(t, i, hh, st):
        j = i - t
        vt = vt_ref[hh, :, j * tq:(j + 1) * tq]
        if t == 0:
            m = jnp.max(st, axis=0, keepdims=True)
            pe = jnp.exp2(st - m)
            acc = jnp.dot(vt, pe.astype(BF16), preferred_element_type=F32)
        else:
            m_old, acc_old = state[i, hh]
            m = jnp.maximum(m_old, jnp.max(st, axis=0, keepdims=True))
            pe = jnp.exp2(st - m)
            acc = jnp.exp2(m_old - m) * acc_old + jnp.dot(vt, pe.astype(BF16),
                                                          preferred_element_type=F32)
        state[i, hh] = (m, acc)
        if j == 0 and hh == 1:
            out_t = jnp.concatenate(
                [state[i, h][1][:HEAD_DIM] * (1.0 / state[i, h][1][HEAD_DIM:HEAD_DIM + 1])
                 for h in range(2)], axis=0)
            o_ref[0, i * tq:(i + 1) * tq, :] = out_t.T.astype(BF16)

    units = [(t, i, hh) for t in range(nblk) for i in range(t, nblk) for hh in range(2)]
    ready = {}
    for n in range(len(units) + lookahead):
        if n < len(units):
            ready[n] = scores(*units[n])
        if n >= lookahead:
            update(*units[n - lookahead], ready.pop(n - lookahead))


def _fox(q, k, v, ft, tq=256, lookahead=4):
    B, S, W = q.shape
    spec = pl.BlockSpec((1, S, LANES), lambda b, p: (b, 0, p))
    return pl.pallas_call(
        functools.partial(_fox_kernel, tq=tq, nblk=S // tq, lookahead=lookahead),
        grid=(B, W // LANES),
        in_specs=[spec, spec, spec,
                  pl.BlockSpec((1, F_ROWS, S), lambda b, p: (b, 0, 0))],
        out_specs=spec,
        out_shape=jax.ShapeDtypeStruct((B, S, W), BF16),
        scratch_shapes=[pltpu.VMEM((2, VT_ROWS, S), BF16), pltpu.VMEM((2, S, LANES), F32)],
        compiler_params=pltpu.CompilerParams(
            dimension_semantics=("arbitrary", "arbitrary"), vmem_limit_bytes=VMEM_LIMIT),
        name="fox",
    )(q, k, v, ft)


def _mix_kernel(x_ref, o0, o1, o2, l0, l1, l2, ob_ref, sga_ref, sgb_ref,
                wa_ref, wb_ref, wo_ref, gain_ref, y_ref, *stage_refs, tm):
    stage = iter(stage_refs)

    def tokens(ref):
        d = ref.shape[1]
        if d == 1:
            return ref[0, 0].astype(F32)
        st = next(stage)
        for r in range(d):
            for c in range(GROUP_WIDTH // LANES):
                st[c, pl.ds(r, tm // d, stride=d), :] = ref[
                    0, r, :, c * LANES:(c + 1) * LANES].astype(F32)
        return jnp.concatenate([st[c] for c in range(GROUP_WIDTH // LANES)], axis=1)

    la, lb, lc = tokens(l0), tokens(l1), tokens(l2)
    mx = jnp.maximum(jnp.maximum(la, lb), lc)
    e0, e1, e2 = jnp.exp(la - mx), jnp.exp(lb - mx), jnp.exp(lc - mx)
    inv = 1.0 / (e0 + e1 + e2)
    oa = ((e0 * tokens(o0) + e1 * tokens(o1) + e2 * tokens(o2)) * inv).astype(BF16)
    a = jnp.dot(oa, wa_ref[...], preferred_element_type=F32)
    b = jnp.dot(ob_ref[0], wb_ref[...], preferred_element_type=F32)
    merged = (sga_ref[0].astype(F32) * a + sgb_ref[0].astype(F32) * b).astype(BF16)
    mix = jnp.dot(merged, wo_ref[...], preferred_element_type=F32)
    y_ref[0] = x_ref[0] + _rmsnorm(mix, gain_ref[...])


def _mix(x, outs, lses, ob, sga, sgb, wa, wb, wo, gain, tm=512):
    B, S, D = x.shape
    row = lambda w: pl.BlockSpec((1, tm, w), lambda b, i: (b, i, 0))
    const = lambda shape: pl.BlockSpec(shape, lambda b, i: (0, 0))
    res = lambda t: pl.BlockSpec((1, t.shape[1], tm // t.shape[1], GROUP_WIDTH),
                                 lambda b, i: (b, 0, i, 0))
    n_stage = 2 * sum(1 for t in outs if t.shape[1] > 1)
    return pl.pallas_call(
        functools.partial(_mix_kernel, tm=tm),
        grid=(B, S // tm),
        in_specs=([row(D)] + [res(t) for t in outs] + [res(t) for t in lses]
                  + [row(FOX_WIDTH), row(D), row(D),
                     const((GROUP_WIDTH, D)), const((FOX_WIDTH, D)), const((D, D)),
                     const((1, D))]),
        out_specs=row(D),
        out_shape=jax.ShapeDtypeStruct((B, S, D), F32),
        scratch_shapes=[pltpu.VMEM((GROUP_WIDTH // LANES, tm, LANES), F32)] * n_stage,
        compiler_params=pltpu.CompilerParams(
            dimension_semantics=("arbitrary", "arbitrary"), vmem_limit_bytes=VMEM_LIMIT),
        name="mix",
    )(x, *outs, *lses, ob, sga, sgb, wa, wb, wo, gain)


def _ffn_kernel(x_ref, g1_ref, g2_ref, wg_ref, wu_ref, wd_ref, o_ref):
    x = x_ref[...]
    h = _rmsnorm(x, g1_ref[...]).astype(BF16)
    acc = jnp.zeros(x.shape, F32)
    for c in range(D_FF // MXU_COLS):
        cols = slice(c * MXU_COLS, (c + 1) * MXU_COLS)
        g = jnp.dot(h, wg_ref[:, cols], preferred_element_type=F32)
        u = jnp.dot(h, wu_ref[:, cols], preferred_element_type=F32)
        a = (g * jax.nn.sigmoid(g) * u).astype(BF16)
        acc = acc + jnp.dot(a, wd_ref[cols, :], preferred_element_type=F32)
    o_ref[...] = x + _rmsnorm(acc, g2_ref[...])


def _ffn(x2, g1, g2, wg, wu, wd, tm=512):
    T, D = x2.shape
    row = pl.BlockSpec((tm, D), lambda i: (i, 0))
    const = lambda shape: pl.BlockSpec(shape, lambda i: (0, 0))
    return pl.pallas_call(
        _ffn_kernel,
        grid=(T // tm,),
        in_specs=[row, const((1, D)), const((1, D)), const((D, D_FF)), const((D, D_FF)),
                  const((D_FF, D))],
        out_specs=row,
        out_shape=jax.ShapeDtypeStruct((T, D), F32),
        compiler_params=pltpu.CompilerParams(
            dimension_semantics=("arbitrary",), vmem_limit_bytes=VMEM_LIMIT),
        name="ffn",
    )(x2, g1, g2, wg, wu, wd)


def kernel(x, w_in, w_proj_a, w_proj_b, w_out, b_forget, w_ffn_gate, w_ffn_up, w_ffn_down,
           norm_mix_pre, norm_mix_post, norm_ffn_pre, norm_ffn_post):
    B, S, D = x.shape
    scale = 1.0 / np.sqrt(HEAD_DIM)
    for layer in range(w_in.shape[0]):
        w = w_in[layer]
        o = 0
        secs = []
        for width in (DIL_WIDTH, DIL_WIDTH, DIL_WIDTH, FOX_WIDTH, FOX_WIDTH, FOX_WIDTH,
                      N_FOX_HEADS, D, D):
            secs.append(w[:, o:o + width])
            o += width
        wqa, wka, wva, wqb, wkb, wvb, wf, wga, wgb = secs
        w_main = jnp.concatenate([wqa * scale, wka, wva, wqb * (scale * LOG2E), wkb, wvb, wga, wgb],
                                 axis=1).astype(BF16)
        wft = jnp.zeros((F_ROWS, D), F32).at[:N_FOX_HEADS].set(wf.T).astype(BF16)
        bf = jnp.zeros((F_ROWS, 1), F32).at[:N_FOX_HEADS, 0].set(b_forget[layer])

        (qa0, qa1, qa2, ka0, ka1, ka2, va0, va1, va2, qb, kb, vb, sga, sgb, ft) = _in_proj(
            x, norm_mix_pre[layer][None, :], w_main, wft, bf)

        outs, lses = [], []
        for q, k, v in ((qa0, ka0, va0), (qa1, ka1, va1), (qa2, ka2, va2)):
            o_g, l_g = _dilated_group(q, k, v)
            outs.append(o_g)
            lses.append(l_g)
        ob = _fox(qb, kb, vb, ft)

        x = _mix(x, outs, lses, ob, sga, sgb,
                 w_proj_a[layer].astype(BF16), w_proj_b[layer].astype(BF16),
                 w_out[layer].astype(BF16), norm_mix_post[layer][None, :])
        x = _ffn(x.reshape(B * S, D), norm_ffn_pre[layer][None, :], norm_ffn_post[layer][None, :],
                 w_ffn_gate[layer].astype(BF16), w_ffn_up[layer].astype(BF16),
                 w_ffn_down[layer].astype(BF16)).reshape(B, S, D)
    return x
```

```python
import functools

import numpy as np
import jax
import jax.numpy as jnp
from jax import lax
from jax.experimental import pallas as pl
from jax.experimental.pallas import tpu as pltpu

D_MODEL = 1024
HEAD_DIM = 64
DIL_CONFIGS = ((128, 1), (512, 4), (2048, 16))
N_DIL_GROUPS = 3
GROUP_WIDTH = 256
N_FOX_HEADS = 8
FOX_WIDTH = 512
BLOCK = 128
ROPE_THETA = 500000.0
ROPE_DIM = 16
D_FF = 2816
EPS = 1e-6
NEG_INF = -1e30

DIL_WIDTH = N_DIL_GROUPS * GROUP_WIDTH
QKV_COLS = 3 * DIL_WIDTH + 3 * FOX_WIDTH
F_ROWS = 16
LANES = 128
MXU_COLS = 256
VMEM_LIMIT = 56 * 1024 * 1024
VT_ROWS = HEAD_DIM + 16
LOG2E = 1.4426950408889634

F32 = jnp.float32
BF16 = jnp.bfloat16
_NT = (((1,), (1,)), ((), ()))


def _rmsnorm(x, gain):
    return x * lax.rsqrt(jnp.mean(x * x, axis=-1, keepdims=True) + EPS) * gain


def _log_sigmoid(z):
    return jnp.minimum(z, 0.0) - jnp.log1p(jnp.exp(-jnp.abs(z)))


def _in_proj_kernel(x_ref, gain_ref, w_ref, wft_ref, bf_ref, cos_ref, sa_ref, sb_ref,
                    qa0, qa1, qa2, ka0, ka1, ka2, va0, va1, va2,
                    qb_ref, kb_ref, vb_ref, ft_ref, carry_ref, *stage_refs, tm):
    i = pl.program_id(1)
    h = _rmsnorm(x_ref[0], gain_ref[...]).astype(BF16)

    z = lax.dot_general(wft_ref[...], h, _NT, preferred_element_type=F32) + bf_ref[...]
    logf = _log_sigmoid(z)
    lane = lax.broadcasted_iota(jnp.int32, (F_ROWS, LANES), 1)
    sums = []
    for j in range(tm // LANES):
        c = logf[:, j * LANES:(j + 1) * LANES]
        k = 1
        while k < LANES:
            c = c + jnp.where(lane >= k, pltpu.roll(c, k, 1), 0.0)
            k *= 2
        sums.append(c)
    carry = jnp.where(i == 0, 0.0, carry_ref[...])
    for j, c in enumerate(sums):
        c = c + carry
        ft_ref[0, :, j * LANES:(j + 1) * LANES] = c
        carry = jnp.broadcast_to(c[:, LANES - 1:LANES], (F_ROWS, LANES))
    carry_ref[...] = carry

    def proj(c0, width=MXU_COLS):
        return jnp.dot(h, w_ref[:, c0:c0 + width], preferred_element_type=F32)

    cos, sa, sb = cos_ref[...], sa_ref[...], sb_ref[...]

    def rope(y):
        halves = []
        for c in range(MXU_COLS // LANES):
            yc = y[:, c * LANES:(c + 1) * LANES]
            halves.append(yc * cos + pltpu.roll(yc, LANES - ROPE_DIM // 2, 1) * sa
                          + pltpu.roll(yc, ROPE_DIM // 2, 1) * sb)
        return jnp.concatenate(halves, axis=1)

    stage = iter(stage_refs)

    def put(o_ref, y, d):
        if d == 1:
            o_ref[0, 0] = y.astype(BF16)
            return
        st = next(stage)
        for c in range(GROUP_WIDTH // LANES):
            st[c] = y[:, c * LANES:(c + 1) * LANES]
        for r in range(d):
            for c in range(GROUP_WIDTH // LANES):
                o_ref[0, r, :, c * LANES:(c + 1) * LANES] = st[
                    c, pl.ds(r, tm // d, stride=d), :].astype(BF16)

    for g, (q_o, k_o, v_o) in enumerate(((qa0, ka0, va0), (qa1, ka1, va1), (qa2, ka2, va2))):
        d = DIL_CONFIGS[g][1]
        put(q_o, rope(proj(g * GROUP_WIDTH)), d)
        put(k_o, rope(proj(DIL_WIDTH + g * GROUP_WIDTH)), d)
        put(v_o, proj(2 * DIL_WIDTH + g * GROUP_WIDTH), d)
    base = 3 * DIL_WIDTH
    for s, o_ref in enumerate((qb_ref, kb_ref, vb_ref)):
        for c in range(FOX_WIDTH // MXU_COLS):
            o_ref[0, :, c * MXU_COLS:(c + 1) * MXU_COLS] = proj(
                base + s * FOX_WIDTH + c * MXU_COLS).astype(BF16)


def _rope_tables(seq):
    half = ROPE_DIM // 2
    inv_freq = np.power(ROPE_THETA, -np.arange(0, ROPE_DIM, 2, dtype=np.float64) / ROPE_DIM)
    ang = np.arange(seq, dtype=np.float64)[:, None] * inv_freq[None, :]
    cos = np.ones((seq, HEAD_DIM)); sa = np.zeros((seq, HEAD_DIM)); sb = np.zeros((seq, HEAD_DIM))
    cos[:, :half] = np.cos(ang); cos[:, half:ROPE_DIM] = np.cos(ang)
    sa[:, :half] = -np.sin(ang)
    sb[:, half:ROPE_DIM] = np.sin(ang)
    rep = LANES // HEAD_DIM
    return tuple(jnp.asarray(np.tile(t, (1, rep)), dtype=F32) for t in (cos, sa, sb))


def _in_proj(x, gain, w_main, wft, bf, tm=512):
    B, S, D = x.shape
    cos, sa, sb = _rope_tables(S)
    const = lambda shape: pl.BlockSpec(shape, lambda b, i: (0,) * len(shape))
    row = lambda w: pl.BlockSpec((1, tm, w), lambda b, i: (b, i, 0))
    tab = pl.BlockSpec((tm, LANES), lambda b, i: (i, 0))
    dils = [d for _, d in DIL_CONFIGS]
    res_shape = [jax.ShapeDtypeStruct((B, d, S // d, GROUP_WIDTH), BF16) for d in dils] * 3
    res_spec = [pl.BlockSpec((1, d, tm // d, GROUP_WIDTH), lambda b, i: (b, 0, i, 0))
                for d in dils] * 3
    out_shape = (res_shape
                 + [jax.ShapeDtypeStruct((B, S, FOX_WIDTH), BF16)] * 3
                 + [jax.ShapeDtypeStruct((B, F_ROWS, S), F32)])
    out_specs = (res_spec + [row(FOX_WIDTH)] * 3
                 + [pl.BlockSpec((1, F_ROWS, tm), lambda b, i: (b, 0, i))])
    n_stage = 3 * sum(1 for d in dils if d > 1)
    return pl.pallas_call(
        functools.partial(_in_proj_kernel, tm=tm),
        grid=(B, S // tm),
        in_specs=[row(D), const((1, D)), const((D, QKV_COLS)), const((F_ROWS, D)),
                  const((F_ROWS, 1)), tab, tab, tab],
        out_specs=out_specs,
        out_shape=out_shape,
        scratch_shapes=([pltpu.VMEM((F_ROWS, LANES), F32)]
                        + [pltpu.VMEM((GROUP_WIDTH // LANES, tm, LANES), F32)] * n_stage),
        compiler_params=pltpu.CompilerParams(
            dimension_semantics=("arbitrary", "arbitrary"), vmem_limit_bytes=VMEM_LIMIT),
        name="in_proj",
    )(x, gain, w_main, wft, bf, cos, sa, sb)


def _dilated_kernel(q_ref, k_ref, v_ref, o_ref, lse_ref, *, d, nb):
    lane = lax.broadcasted_iota(jnp.int32, (BLOCK, LANES), 1)
    lo_half = lane < HEAD_DIM

    kw = 2 * BLOCK
    qi = lax.broadcasted_iota(jnp.int32, (BLOCK, kw), 0)
    kj = lax.broadcasted_iota(jnp.int32, (BLOCK, kw), 1)
    mask_lead = kj <= qi
    mask_band = (kj >= qi) & (kj <= qi + BLOCK)
    mask_tail = (kj >= BLOCK) & (kj - BLOCK <= qi)

    def tiles(r, n):
        rows = slice(n * BLOCK, (n + 1) * BLOCK)
        for hp in range(GROUP_WIDTH // LANES):
            cols = slice(hp * LANES, (hp + 1) * LANES)
            q = q_ref[0, r, rows, cols]
            if nb == 1:
                r0 = r - r % 2
                kc = k_ref[0, r0:r0 + 2, :, cols].reshape(kw, LANES)
                vc = v_ref[0, r0:r0 + 2, :, cols].reshape(kw, LANES)
                valid = mask_lead if r % 2 == 0 else mask_tail
            else:
                k0 = max(n - 1, 0) * BLOCK
                kc = k_ref[0, r, k0:k0 + kw, cols]
                vc = v_ref[0, r, k0:k0 + kw, cols]
                valid = mask_lead if n == 0 else mask_band
            outs, lses = [], []
            for hh in range(2):
                qm = jnp.where(lo_half if hh == 0 else ~lo_half, q, jnp.zeros_like(q))
                s = lax.dot_general(qm, kc, _NT, preferred_element_type=F32)
                s = jnp.where(valid, s, NEG_INF)
                m = jnp.max(s, axis=1, keepdims=True)
                p = jnp.exp(s - m)
                den = jnp.sum(p, axis=1, keepdims=True)
                o = jnp.dot(p.astype(BF16), vc, preferred_element_type=F32)
                outs.append(o * (1.0 / den))
                lses.append(m + jnp.log(den))
            o_ref[0, r, rows, cols] = jnp.where(lo_half, outs[0], outs[1]).astype(BF16)
            lse_ref[0, r, rows, cols] = jnp.where(lo_half, lses[0], lses[1])

    for r in range(d):
        for n in range(nb):
            tiles(r, n)


def _dilated_group(q, k, v):
    B, d, L, W = q.shape
    spec = pl.BlockSpec((1, d, L, W), lambda b: (b, 0, 0, 0))
    return pl.pallas_call(
        functools.partial(_dilated_kernel, d=d, nb=L // BLOCK),
        grid=(B,),
        in_specs=[spec, spec, spec],
        out_specs=[spec, spec],
        out_shape=[jax.ShapeDtypeStruct(q.shape, BF16), jax.ShapeDtypeStruct(q.shape, F32)],
        compiler_params=pltpu.CompilerParams(
            dimension_semantics=("arbitrary",), vmem_limit_bytes=VMEM_LIMIT),
        name=f"dilated_d{d}",
    )(q, k, v)


def _fox_kernel(q_ref, k_ref, v_ref, ft_ref, o_ref, vt_ref, fcol_ref, *, tq, nblk, lookahead):
    p = pl.program_id(1)
    seq = nblk * tq
    lane = lax.broadcasted_iota(jnp.int32, (tq, LANES), 1)
    lo_half = lane < HEAD_DIM
    key = lax.broadcasted_iota(jnp.int32, (tq, tq), 0)
    qry = lax.broadcasted_iota(jnp.int32, (tq, tq), 1)
    causal = key <= qry

    for c in range(seq // tq):
        cs = slice(c * tq, (c + 1) * tq)
        vt = v_ref[0, cs, :].astype(F32).T.astype(BF16)
        for hh in range(2):
            vt_ref[hh, :HEAD_DIM, cs] = vt[hh * HEAD_DIM:(hh + 1) * HEAD_DIM]
    for hh in range(2):
        vt_ref[hh, HEAD_DIM:, :] = jnp.ones((VT_ROWS - HEAD_DIM, seq), BF16)
    frows = [ft_ref[0, pl.ds(2 * p + hh, 1), :] * LOG2E for hh in range(2)]
    for hh in range(2):
        for c in range(seq // LANES):
            cs = slice(c * LANES, (c + 1) * LANES)
            fcol_ref[hh, cs, :] = jnp.broadcast_to(frows[hh][:, cs], (LANES, LANES)).T

    def scores(t, i, hh):
        j = i - t
        ks = slice(j * tq, (j + 1) * tq)
        q = q_ref[0, i * tq:(i + 1) * tq, :]
        qm = jnp.where(lo_half if hh == 0 else ~lo_half, q, jnp.zeros_like(q))
        c0 = frows[hh][:, i * tq:i * tq + 1]
        bias = c0 - fcol_ref[hh, ks, :]
        st = lax.dot_general(k_ref[0, ks, :], qm, _NT, preferred_element_type=F32)
        st = st + jnp.concatenate([bias] * (tq // LANES), axis=1)
        return jnp.where(causal, st, NEG_INF) if t == 0 else st

    state = {}

    def update(t, i, hh, st):
        j = i - t
        vt = vt_ref[hh, :, j * tq:(j + 1) * tq]
        if t == 0:
            m = jnp.max(st, axis=0, keepdims=True)
            pe = jnp.exp2(st - m)
            acc = jnp.dot(vt, pe.astype(BF16), preferred_element_type=F32)
        else:
            m_old, acc_old = state[i, hh]
            m = jnp.maximum(m_old, jnp.max(st, axis=0, keepdims=True))
            pe = jnp.exp2(st - m)
            acc = jnp.exp2(m_old - m) * acc_old + jnp.dot(vt, pe.astype(BF16),
                                                          preferred_element_type=F32)
        state[i, hh] = (m, acc)
        if j == 0 and hh == 1:
            out_t = jnp.concatenate(
                [state[i, h][1][:HEAD_DIM] * (1.0 / state[i, h][1][HEAD_DIM:HEAD_DIM + 1])
                 for h in range(2)], axis=0)
            o_ref[0, i * tq:(i + 1) * tq, :] = out_t.T.astype(BF16)

    units = [(t, i, hh) for t in range(nblk) for i in range(t, nblk) for hh in range(2)]
    ready = {}
    for n in range(len(units) + lookahead):
        if n < len(units):
            ready[n] = scores(*units[n])
        if n >= lookahead:
            update(*units[n - lookahead], ready.pop(n - lookahead))


def _fox(q, k, v, ft, tq=256, lookahead=4):
    B, S, W = q.shape
    spec = pl.BlockSpec((1, S, LANES), lambda b, p: (b, 0, p))
    return pl.pallas_call(
        functools.partial(_fox_kernel, tq=tq, nblk=S // tq, lookahead=lookahead),
        grid=(B, W // LANES),
        in_specs=[spec, spec, spec,
                  pl.BlockSpec((1, F_ROWS, S), lambda b, p: (b, 0, 0))],
        out_specs=spec,
        out_shape=jax.ShapeDtypeStruct((B, S, W), BF16),
        scratch_shapes=[pltpu.VMEM((2, VT_ROWS, S), BF16), pltpu.VMEM((2, S, LANES), F32)],
        compiler_params=pltpu.CompilerParams(
            dimension_semantics=("arbitrary", "arbitrary"), vmem_limit_bytes=VMEM_LIMIT),
        name="fox",
    )(q, k, v, ft)


def _mix_kernel(x_ref, o0, o1, o2, l0, l1, l2, ob_ref, wg_ref, wa_ref, wb_ref, wo_ref,
                gpre_ref, gpost_ref, y_ref, *stage_refs, tm):
    stage = iter(stage_refs)
    x = x_ref[0]
    h = _rmsnorm(x, gpre_ref[...]).astype(BF16)

    def tokens(ref):
        d = ref.shape[1]
        if d == 1:
            return ref[0, 0].astype(F32)
        st = next(stage)
        for r in range(d):
            for c in range(GROUP_WIDTH // LANES):
                st[c, pl.ds(r, tm // d, stride=d), :] = ref[
                    0, r, :, c * LANES:(c + 1) * LANES].astype(F32)
        return jnp.concatenate([st[c] for c in range(GROUP_WIDTH // LANES)], axis=1)

    la, lb, lc = tokens(l0), tokens(l1), tokens(l2)
    mx = jnp.maximum(jnp.maximum(la, lb), lc)
    e0, e1, e2 = jnp.exp(la - mx), jnp.exp(lb - mx), jnp.exp(lc - mx)
    inv = 1.0 / (e0 + e1 + e2)
    oa = ((e0 * tokens(o0) + e1 * tokens(o1) + e2 * tokens(o2)) * inv).astype(BF16)
    ob = ob_ref[0]
    merged = []
    for c in range(D_MODEL // MXU_COLS):
        cols = slice(c * MXU_COLS, (c + 1) * MXU_COLS)
        ga = jnp.dot(h, wg_ref[:, cols], preferred_element_type=F32)
        gb = jnp.dot(h, wg_ref[:, D_MODEL + c * MXU_COLS:D_MODEL + (c + 1) * MXU_COLS],
                     preferred_element_type=F32)
        a = jnp.dot(oa, wa_ref[:, cols], preferred_element_type=F32)
        b = jnp.dot(ob, wb_ref[:, cols], preferred_element_type=F32)
        merged.append((jax.nn.sigmoid(ga) * a + jax.nn.sigmoid(gb) * b).astype(BF16))
    mix = jnp.dot(jnp.concatenate(merged, axis=1), wo_ref[...], preferred_element_type=F32)
    y_ref[0] = x + _rmsnorm(mix, gpost_ref[...])


def _mix(x, outs, lses, ob, wg, wa, wb, wo, gain_pre, gain_post, tm=512):
    B, S, D = x.shape
    row = lambda w: pl.BlockSpec((1, tm, w), lambda b, i: (b, i, 0))
    const = lambda shape: pl.BlockSpec(shape, lambda b, i: (0, 0))
    res = lambda t: pl.BlockSpec((1, t.shape[1], tm // t.shape[1], GROUP_WIDTH),
                                 lambda b, i: (b, 0, i, 0))
    n_stage = 2 * sum(1 for t in outs if t.shape[1] > 1)
    return pl.pallas_call(
        functools.partial(_mix_kernel, tm=tm),
        grid=(B, S // tm),
        in_specs=([row(D)] + [res(t) for t in outs] + [res(t) for t in lses]
                  + [row(FOX_WIDTH), const((D, 2 * D)),
                     const((GROUP_WIDTH, D)), const((FOX_WIDTH, D)), const((D, D)),
                     const((1, D)), const((1, D))]),
        out_specs=row(D),
        out_shape=jax.ShapeDtypeStruct((B, S, D), F32),
        scratch_shapes=[pltpu.VMEM((GROUP_WIDTH // LANES, tm, LANES), F32)] * n_stage,
        compiler_params=pltpu.CompilerParams(
            dimension_semantics=("arbitrary", "arbitrary"), vmem_limit_bytes=VMEM_LIMIT),
        name="mix",
    )(x, *outs, *lses, ob, wg, wa, wb, wo, gain_pre, gain_post)


def _ffn_kernel(x_ref, g1_ref, g2_ref, wg_ref, wu_ref, wd_ref, o_ref):
    x = x_ref[...]
    h = _rmsnorm(x, g1_ref[...]).astype(BF16)
    acc = jnp.zeros(x.shape, F32)
    for c in range(D_FF // MXU_COLS):
        cols = slice(c * MXU_COLS, (c + 1) * MXU_COLS)
        g = jnp.dot(h, wg_ref[:, cols], preferred_element_type=F32)
        u = jnp.dot(h, wu_ref[:, cols], preferred_element_type=F32)
        a = (g * jax.nn.sigmoid(g) * u).astype(BF16)
        acc = acc + jnp.dot(a, wd_ref[cols, :], preferred_element_type=F32)
    o_ref[...] = x + _rmsnorm(acc, g2_ref[...])


def _ffn(x2, g1, g2, wg, wu, wd, tm=512):
    T, D = x2.shape
    row = pl.BlockSpec((tm, D), lambda i: (i, 0))
    const = lambda shape: pl.BlockSpec(shape, lambda i: (0, 0))
    return pl.pallas_call(
        _ffn_kernel,
        grid=(T // tm,),
        in_specs=[row, const((1, D)), const((1, D)), const((D, D_FF)), const((D, D_FF)),
                  const((D_FF, D))],
        out_specs=row,
        out_shape=jax.ShapeDtypeStruct((T, D), F32),
        compiler_params=pltpu.CompilerParams(
            dimension_semantics=("arbitrary",), vmem_limit_bytes=VMEM_LIMIT),
        name="ffn",
    )(x2, g1, g2, wg, wu, wd)


def kernel(x, w_in, w_proj_a, w_proj_b, w_out, b_forget, w_ffn_gate, w_ffn_up, w_ffn_down,
           norm_mix_pre, norm_mix_post, norm_ffn_pre, norm_ffn_post):
    B, S, D = x.shape
    scale = 1.0 / np.sqrt(HEAD_DIM)
    col_scale = np.ones((QKV_COLS,), np.float32)
    col_scale[:DIL_WIDTH] = scale
    col_scale[3 * DIL_WIDTH:3 * DIL_WIDTH + FOX_WIDTH] = scale * LOG2E
    f0 = QKV_COLS
    g0 = QKV_COLS + N_FOX_HEADS
    for layer in range(w_in.shape[0]):
        w = w_in[layer]
        w_qkv = (w[:, :QKV_COLS] * col_scale).astype(BF16)
        w_gates = w[:, g0:].astype(BF16)
        wft = jnp.zeros((F_ROWS, D), F32).at[:N_FOX_HEADS].set(w[:, f0:g0].T).astype(BF16)
        bf = jnp.zeros((F_ROWS, 1), F32).at[:N_FOX_HEADS, 0].set(b_forget[layer])

        (qa0, qa1, qa2, ka0, ka1, ka2, va0, va1, va2, qb, kb, vb, ft) = _in_proj(
            x, norm_mix_pre[layer][None, :], w_qkv, wft, bf)

        outs, lses = [], []
        for q, k, v in ((qa0, ka0, va0), (qa1, ka1, va1), (qa2, ka2, va2)):
            o_g, l_g = _dilated_group(q, k, v)
            outs.append(o_g)
            lses.append(l_g)
        ob = _fox(qb, kb, vb, ft)

        x = _mix(x, outs, lses, ob, w_gates,
                 w_proj_a[layer].astype(BF16), w_proj_b[layer].astype(BF16),
                 w_out[layer].astype(BF16), norm_mix_pre[layer][None, :],
                 norm_mix_post[layer][None, :])
        x = _ffn(x.reshape(B * S, D), norm_ffn_pre[layer][None, :], norm_ffn_post[layer][None, :],
                 w_ffn_gate[layer].astype(BF16), w_ffn_up[layer].astype(BF16),
                 w_ffn_down[layer].astype(BF16)).reshape(B, S, D)
    return x
```

```python
import functools

import numpy as np
import jax
import jax.numpy as jnp
from jax import lax
from jax.experimental import pallas as pl
from jax.experimental.pallas import tpu as pltpu

D_MODEL = 1024
HEAD_DIM = 64
DIL_CONFIGS = ((128, 1), (512, 4), (2048, 16))
N_DIL_GROUPS = 3
GROUP_WIDTH = 256
N_FOX_HEADS = 8
FOX_WIDTH = 512
BLOCK = 128
ROPE_THETA = 500000.0
ROPE_DIM = 16
D_FF = 2816
EPS = 1e-6
NEG_INF = -1e30

DIL_WIDTH = N_DIL_GROUPS * GROUP_WIDTH
QKV_COLS = 3 * DIL_WIDTH + 3 * FOX_WIDTH
GATE_COL0 = QKV_COLS + N_FOX_HEADS
F_ROWS = 16
LANES = 128
MXU_COLS = 256
VMEM_LIMIT = 56 * 1024 * 1024
VT_ROWS = HEAD_DIM + 16
LOG2E = 1.4426950408889634

F32 = jnp.float32
BF16 = jnp.bfloat16
_NT = (((1,), (1,)), ((), ()))


def _rmsnorm(x, gain):
    return x * lax.rsqrt(jnp.mean(x * x, axis=-1, keepdims=True) + EPS) * gain


def _log_sigmoid(z):
    return jnp.minimum(z, 0.0) - jnp.log1p(jnp.exp(-jnp.abs(z)))


def _prep_w_in_kernel(w_ref, scale_ref, qkv_ref, gates_ref, wf_ref):
    w = w_ref[0]
    qkv_ref[...] = (w[:, :QKV_COLS] * scale_ref[...]).astype(BF16)
    gates_ref[...] = w[:, GATE_COL0:].astype(BF16)
    lane = lax.broadcasted_iota(jnp.int32, (w.shape[0], LANES), 1)
    wf_ref[...] = jnp.where(lane < N_FOX_HEADS, w[:, QKV_COLS:QKV_COLS + LANES], 0.0).astype(BF16)


def _prep_w_in(w_in_layer, col_scale, rows=128):
    _, D, C = w_in_layer.shape
    return pl.pallas_call(
        _prep_w_in_kernel,
        grid=(D // rows,),
        in_specs=[pl.BlockSpec((1, rows, C), lambda i: (0, i, 0)),
                  pl.BlockSpec((1, QKV_COLS), lambda i: (0, 0))],
        out_specs=[pl.BlockSpec((rows, QKV_COLS), lambda i: (i, 0)),
                   pl.BlockSpec((rows, 2 * D), lambda i: (i, 0)),
                   pl.BlockSpec((rows, LANES), lambda i: (i, 0))],
        out_shape=[jax.ShapeDtypeStruct((D, QKV_COLS), BF16),
                   jax.ShapeDtypeStruct((D, 2 * D), BF16),
                   jax.ShapeDtypeStruct((D, LANES), BF16)],
        compiler_params=pltpu.CompilerParams(dimension_semantics=("arbitrary",)),
        name="prep_w_in",
    )(w_in_layer, col_scale)


def _in_proj_kernel(x_ref, gain_ref, w_ref, wf_ref, bf_ref, cos_ref, sa_ref, sb_ref,
                    qa0, qa1, qa2, ka0, ka1, ka2, va0, va1, va2,
                    qb_ref, kb_ref, vb_ref, ft_ref, carry_ref, *stage_refs, tm):
    i = pl.program_id(1)
    h = _rmsnorm(x_ref[0], gain_ref[...]).astype(BF16)

    z = jnp.dot(h, wf_ref[...], preferred_element_type=F32).T[:F_ROWS] + bf_ref[...]
    logf = _log_sigmoid(z)
    lane = lax.broadcasted_iota(jnp.int32, (F_ROWS, LANES), 1)
    sums = []
    for j in range(tm // LANES):
        c = logf[:, j * LANES:(j + 1) * LANES]
        k = 1
        while k < LANES:
            c = c + jnp.where(lane >= k, pltpu.roll(c, k, 1), 0.0)
            k *= 2
        sums.append(c)
    carry = jnp.where(i == 0, 0.0, carry_ref[...])
    for j, c in enumerate(sums):
        c = c + carry
        ft_ref[0, :, j * LANES:(j + 1) * LANES] = c
        carry = jnp.broadcast_to(c[:, LANES - 1:LANES], (F_ROWS, LANES))
    carry_ref[...] = carry

    def proj(c0, width=MXU_COLS):
        return jnp.dot(h, w_ref[:, c0:c0 + width], preferred_element_type=F32)

    cos, sa, sb = cos_ref[...], sa_ref[...], sb_ref[...]

    def rope(y):
        halves = []
        for c in range(MXU_COLS // LANES):
            yc = y[:, c * LANES:(c + 1) * LANES]
            halves.append(yc * cos + pltpu.roll(yc, LANES - ROPE_DIM // 2, 1) * sa
                          + pltpu.roll(yc, ROPE_DIM // 2, 1) * sb)
        return jnp.concatenate(halves, axis=1)

    stage = iter(stage_refs)

    def put(o_ref, y, d):
        if d == 1:
            o_ref[0, 0] = y.astype(BF16)
            return
        st = next(stage)
        for c in range(GROUP_WIDTH // LANES):
            st[c] = y[:, c * LANES:(c + 1) * LANES]
        for r in range(d):
            for c in range(GROUP_WIDTH // LANES):
                o_ref[0, r, :, c * LANES:(c + 1) * LANES] = st[
                    c, pl.ds(r, tm // d, stride=d), :].astype(BF16)

    for g, (q_o, k_o, v_o) in enumerate(((qa0, ka0, va0), (qa1, ka1, va1), (qa2, ka2, va2))):
        d = DIL_CONFIGS[g][1]
        put(q_o, rope(proj(g * GROUP_WIDTH)), d)
        put(k_o, rope(proj(DIL_WIDTH + g * GROUP_WIDTH)), d)
        put(v_o, proj(2 * DIL_WIDTH + g * GROUP_WIDTH), d)
    base = 3 * DIL_WIDTH
    for s, o_ref in enumerate((qb_ref, kb_ref, vb_ref)):
        for c in range(FOX_WIDTH // MXU_COLS):
            o_ref[0, :, c * MXU_COLS:(c + 1) * MXU_COLS] = proj(
                base + s * FOX_WIDTH + c * MXU_COLS).astype(BF16)


def _rope_tables(seq):
    half = ROPE_DIM // 2
    inv_freq = np.power(ROPE_THETA, -np.arange(0, ROPE_DIM, 2, dtype=np.float64) / ROPE_DIM)
    ang = np.arange(seq, dtype=np.float64)[:, None] * inv_freq[None, :]
    cos = np.ones((seq, HEAD_DIM)); sa = np.zeros((seq, HEAD_DIM)); sb = np.zeros((seq, HEAD_DIM))
    cos[:, :half] = np.cos(ang); cos[:, half:ROPE_DIM] = np.cos(ang)
    sa[:, :half] = -np.sin(ang)
    sb[:, half:ROPE_DIM] = np.sin(ang)
    rep = LANES // HEAD_DIM
    return tuple(jnp.asarray(np.tile(t, (1, rep)), dtype=F32) for t in (cos, sa, sb))


def _in_proj(x, gain, w_main, wf, bf, tm=512):
    B, S, D = x.shape
    cos, sa, sb = _rope_tables(S)
    const = lambda shape: pl.BlockSpec(shape, lambda b, i: (0,) * len(shape))
    row = lambda w: pl.BlockSpec((1, tm, w), lambda b, i: (b, i, 0))
    tab = pl.BlockSpec((tm, LANES), lambda b, i: (i, 0))
    dils = [d for _, d in DIL_CONFIGS]
    res_shape = [jax.ShapeDtypeStruct((B, d, S // d, GROUP_WIDTH), BF16) for d in dils] * 3
    res_spec = [pl.BlockSpec((1, d, tm // d, GROUP_WIDTH), lambda b, i: (b, 0, i, 0))
                for d in dils] * 3
    out_shape = (res_shape
                 + [jax.ShapeDtypeStruct((B, S, FOX_WIDTH), BF16)] * 3
                 + [jax.ShapeDtypeStruct((B, F_ROWS, S), F32)])
    out_specs = (res_spec + [row(FOX_WIDTH)] * 3
                 + [pl.BlockSpec((1, F_ROWS, tm), lambda b, i: (b, 0, i))])
    n_stage = 3 * sum(1 for d in dils if d > 1)
    return pl.pallas_call(
        functools.partial(_in_proj_kernel, tm=tm),
        grid=(B, S // tm),
        in_specs=[row(D), const((1, D)), const((D, QKV_COLS)), const((D, LANES)),
                  const((F_ROWS, 1)), tab, tab, tab],
        out_specs=out_specs,
        out_shape=out_shape,
        scratch_shapes=([pltpu.VMEM((F_ROWS, LANES), F32)]
                        + [pltpu.VMEM((GROUP_WIDTH // LANES, tm, LANES), F32)] * n_stage),
        compiler_params=pltpu.CompilerParams(
            dimension_semantics=("arbitrary", "arbitrary"), vmem_limit_bytes=VMEM_LIMIT),
        name="in_proj",
    )(x, gain, w_main, wf, bf, cos, sa, sb)


def _dilated_kernel(q_ref, k_ref, v_ref, o_ref, lse_ref, *, d, nb):
    lane = lax.broadcasted_iota(jnp.int32, (BLOCK, LANES), 1)
    lo_half = lane < HEAD_DIM

    kw = 2 * BLOCK
    qi = lax.broadcasted_iota(jnp.int32, (BLOCK, kw), 0)
    kj = lax.broadcasted_iota(jnp.int32, (BLOCK, kw), 1)
    mask_lead = kj <= qi
    mask_band = (kj >= qi) & (kj <= qi + BLOCK)
    mask_tail = (kj >= BLOCK) & (kj - BLOCK <= qi)

    def tiles(r, n):
        rows = slice(n * BLOCK, (n + 1) * BLOCK)
        for hp in range(GROUP_WIDTH // LANES):
            cols = slice(hp * LANES, (hp + 1) * LANES)
            q = q_ref[0, r, rows, cols]
            if nb == 1:
                r0 = r - r % 2
                kc = k_ref[0, r0:r0 + 2, :, cols].reshape(kw, LANES)
                vc = v_ref[0, r0:r0 + 2, :, cols].reshape(kw, LANES)
                valid = mask_lead if r % 2 == 0 else mask_tail
            else:
                k0 = max(n - 1, 0) * BLOCK
                kc = k_ref[0, r, k0:k0 + kw, cols]
                vc = v_ref[0, r, k0:k0 + kw, cols]
                valid = mask_lead if n == 0 else mask_band
            outs, lses = [], []
            for hh in range(2):
                qm = jnp.where(lo_half if hh == 0 else ~lo_half, q, jnp.zeros_like(q))
                s = lax.dot_general(qm, kc, _NT, preferred_element_type=F32)
                s = jnp.where(valid, s, NEG_INF)
                m = jnp.max(s, axis=1, keepdims=True)
                p = jnp.exp(s - m)
                den = jnp.sum(p, axis=1, keepdims=True)
                o = jnp.dot(p.astype(BF16), vc, preferred_element_type=F32)
                outs.append(o * (1.0 / den))
                lses.append(m + jnp.log(den))
            o_ref[0, r, rows, cols] = jnp.where(lo_half, outs[0], outs[1]).astype(BF16)
            lse_ref[0, r, rows, cols] = jnp.where(lo_half, lses[0], lses[1])

    for r in range(d):
        for n in range(nb):
            tiles(r, n)


def _dilated_group(q, k, v):
    B, d, L, W = q.shape
    spec = pl.BlockSpec((1, d, L, W), lambda b: (b, 0, 0, 0))
    return pl.pallas_call(
        functools.partial(_dilated_kernel, d=d, nb=L // BLOCK),
        grid=(B,),
        in_specs=[spec, spec, spec],
        out_specs=[spec, spec],
        out_shape=[jax.ShapeDtypeStruct(q.shape, BF16), jax.ShapeDtypeStruct(q.shape, F32)],
        compiler_params=pltpu.CompilerParams(
            dimension_semantics=("arbitrary",), vmem_limit_bytes=VMEM_LIMIT),
        name=f"dilated_d{d}",
    )(q, k, v)


def _fox_kernel(q_ref, k_ref, v_ref, ft_ref, o_ref, vt_ref, fcol_ref, *, tq, nblk, lookahead):
    p = pl.program_id(1)
    seq = nblk * tq
    lane = lax.broadcasted_iota(jnp.int32, (tq, LANES), 1)
    lo_half = lane < HEAD_DIM
    key = lax.broadcasted_iota(jnp.int32, (tq, tq), 0)
    qry = lax.broadcasted_iota(jnp.int32, (tq, tq), 1)
    causal = key <= qry

    for c in range(seq // tq):
        cs = slice(c * tq, (c + 1) * tq)
        vt = v_ref[0, cs, :].astype(F32).T.astype(BF16)
        for hh in range(2):
            vt_ref[hh, :HEAD_DIM, cs] = vt[hh * HEAD_DIM:(hh + 1) * HEAD_DIM]
    for hh in range(2):
        vt_ref[hh, HEAD_DIM:, :] = jnp.ones((VT_ROWS - HEAD_DIM, seq), BF16)
    frows = [ft_ref[0, pl.ds(2 * p + hh, 1), :] * LOG2E for hh in range(2)]
    for hh in range(2):
        for c in range(seq // LANES):
            cs = slice(c * LANES, (c + 1) * LANES)
            fcol_ref[hh, cs, :] = jnp.broadcast_to(frows[hh][:, cs], (LANES, LANES)).T

    def scores(t, i, hh):
        j = i - t
        ks = slice(j * tq, (j + 1) * tq)
        q = q_ref[0, i * tq:(i + 1) * tq, :]
        qm = jnp.where(lo_half if hh == 0 else ~lo_half, q, jnp.zeros_like(q))
        c0 = frows[hh][:, i * tq:i * tq + 1]
        bias = c0 - fcol_ref[hh, ks, :]
        st = lax.dot_general(k_ref[0, ks, :], qm, _NT, preferred_element_type=F32)
        st = st + jnp.concatenate([bias] * (tq // LANES), axis=1)
        return jnp.where(causal, st, NEG_INF) if t == 0 else st

    state = {}

    def update(t, i, hh, st):
        j = i - t
        vt = vt_ref[hh, :, j * tq:(j + 1) * tq]
        if t == 0:
            m = jnp.max(st, axis=0, keepdims=True)
            pe = jnp.exp2(st - m)
            acc = jnp.dot(vt, pe.astype(BF16), preferred_element_type=F32)
        else:
            m_old, acc_old = state[i, hh]
            m = jnp.maximum(m_old, jnp.max(st, axis=0, keepdims=True))
            pe = jnp.exp2(st - m)
            acc = jnp.exp2(m_old - m) * acc_old + jnp.dot(vt, pe.astype(BF16),
                                                          preferred_element_type=F32)
        state[i, hh] = (m, acc)
        if j == 0 and hh == 1:
            out_t = jnp.concatenate(
                [state[i, h][1][:HEAD_DIM] * (1.0 / state[i, h][1][HEAD_DIM:HEAD_DIM + 1])
                 for h in range(2)], axis=0)
            o_ref[0, i * tq:(i + 1) * tq, :] = out_t.T.astype(BF16)

    units = [(t, i, hh) for t in range(nblk) for i in range(t, nblk) for hh in range(2)]
    ready = {}
    for n in range(len(units) + lookahead):
        if n < len(units):
            ready[n] = scores(*units[n])
        if n >= lookahead:
            update(*units[n - lookahead], ready.pop(n - lookahead))


def _fox(q, k, v, ft, tq=256, lookahead=4):
    B, S, W = q.shape
    spec = pl.BlockSpec((1, S, LANES), lambda b, p: (b, 0, p))
    return pl.pallas_call(
        functools.partial(_fox_kernel, tq=tq, nblk=S // tq, lookahead=lookahead),
        grid=(B, W // LANES),
        in_specs=[spec, spec, spec,
                  pl.BlockSpec((1, F_ROWS, S), lambda b, p: (b, 0, 0))],
        out_specs=spec,
        out_shape=jax.ShapeDtypeStruct((B, S, W), BF16),
        scratch_shapes=[pltpu.VMEM((2, VT_ROWS, S), BF16), pltpu.VMEM((2, S, LANES), F32)],
        compiler_params=pltpu.CompilerParams(
            dimension_semantics=("arbitrary", "arbitrary"), vmem_limit_bytes=VMEM_LIMIT),
        name="fox",
    )(q, k, v, ft)


def _mix_kernel(x_ref, o0, o1, o2, l0, l1, l2, ob_ref, wg_ref, wa_ref, wb_ref, wo_ref,
                gpre_ref, gpost_ref, y_ref, *stage_refs, tm):
    stage = iter(stage_refs)
    x = x_ref[0]
    h = _rmsnorm(x, gpre_ref[...]).astype(BF16)

    def tokens(ref):
        d = ref.shape[1]
        if d == 1:
            return ref[0, 0].astype(F32)
        st = next(stage)
        for r in range(d):
            for c in range(GROUP_WIDTH // LANES):
                st[c, pl.ds(r, tm // d, stride=d), :] = ref[
                    0, r, :, c * LANES:(c + 1) * LANES].astype(F32)
        return jnp.concatenate([st[c] for c in range(GROUP_WIDTH // LANES)], axis=1)

    la, lb, lc = tokens(l0), tokens(l1), tokens(l2)
    mx = jnp.maximum(jnp.maximum(la, lb), lc)
    e0, e1, e2 = jnp.exp(la - mx), jnp.exp(lb - mx), jnp.exp(lc - mx)
    inv = 1.0 / (e0 + e1 + e2)
    oa = ((e0 * tokens(o0) + e1 * tokens(o1) + e2 * tokens(o2)) * inv).astype(BF16)
    ob = ob_ref[0]
    merged = []
    for c in range(D_MODEL // MXU_COLS):
        cols = slice(c * MXU_COLS, (c + 1) * MXU_COLS)
        ga = jnp.dot(h, wg_ref[:, cols], preferred_element_type=F32)
        gb = jnp.dot(h, wg_ref[:, D_MODEL + c * MXU_COLS:D_MODEL + (c + 1) * MXU_COLS],
                     preferred_element_type=F32)
        a = jnp.dot(oa, wa_ref[:, cols], preferred_element_type=F32)
        b = jnp.dot(ob, wb_ref[:, cols], preferred_element_type=F32)
        merged.append((jax.nn.sigmoid(ga) * a + jax.nn.sigmoid(gb) * b).astype(BF16))
    mix = jnp.dot(jnp.concatenate(merged, axis=1), wo_ref[...], preferred_element_type=F32)
    y_ref[0] = x + _rmsnorm(mix, gpost_ref[...])


def _mix(x, outs, lses, ob, wg, wa, wb, wo, gain_pre, gain_post, tm=512):
    B, S, D = x.shape
    row = lambda w: pl.BlockSpec((1, tm, w), lambda b, i: (b, i, 0))
    const = lambda shape: pl.BlockSpec(shape, lambda b, i: (0, 0))
    res = lambda t: pl.BlockSpec((1, t.shape[1], tm // t.shape[1], GROUP_WIDTH),
                                 lambda b, i: (b, 0, i, 0))
    n_stage = 2 * sum(1 for t in outs if t.shape[1] > 1)
    return pl.pallas_call(
        functools.partial(_mix_kernel, tm=tm),
        grid=(B, S // tm),
        in_specs=([row(D)] + [res(t) for t in outs] + [res(t) for t in lses]
                  + [row(FOX_WIDTH), const((D, 2 * D)),
                     const((GROUP_WIDTH, D)), const((FOX_WIDTH, D)), const((D, D)),
                     const((1, D)), const((1, D))]),
        out_specs=row(D),
        out_shape=jax.ShapeDtypeStruct((B, S, D), F32),
        scratch_shapes=[pltpu.VMEM((GROUP_WIDTH // LANES, tm, LANES), F32)] * n_stage,
        compiler_params=pltpu.CompilerParams(
            dimension_semantics=("arbitrary", "arbitrary"), vmem_limit_bytes=VMEM_LIMIT),
        name="mix",
    )(x, *outs, *lses, ob, wg, wa, wb, wo, gain_pre, gain_post)


def _ffn_kernel(x_ref, g1_ref, g2_ref, wg_ref, wu_ref, wd_ref, o_ref):
    x = x_ref[...]
    h = _rmsnorm(x, g1_ref[...]).astype(BF16)
    acc = jnp.zeros(x.shape, F32)
    for c in range(D_FF // MXU_COLS):
        cols = slice(c * MXU_COLS, (c + 1) * MXU_COLS)
        g = jnp.dot(h, wg_ref[:, cols], preferred_element_type=F32)
        u = jnp.dot(h, wu_ref[:, cols], preferred_element_type=F32)
        a = (g * jax.nn.sigmoid(g) * u).astype(BF16)
        acc = acc + jnp.dot(a, wd_ref[cols, :], preferred_element_type=F32)
    o_ref[...] = x + _rmsnorm(acc, g2_ref[...])


def _ffn(x2, g1, g2, wg, wu, wd, tm=512):
    T, D = x2.shape
    row = pl.BlockSpec((tm, D), lambda i: (i, 0))
    const = lambda shape: pl.BlockSpec(shape, lambda i: (0, 0), pipeline_mode=pl.Buffered(1))
    return pl.pallas_call(
        _ffn_kernel,
        grid=(T // tm,),
        in_specs=[row, const((1, D)), const((1, D)), const((D, D_FF)), const((D, D_FF)),
                  const((D_FF, D))],
        out_specs=row,
        out_shape=jax.ShapeDtypeStruct((T, D), F32),
        compiler_params=pltpu.CompilerParams(
            dimension_semantics=("arbitrary",), vmem_limit_bytes=VMEM_LIMIT),
        name="ffn",
    )(x2, g1, g2, wg, wu, wd)


def kernel(x, w_in, w_proj_a, w_proj_b, w_out, b_forget, w_ffn_gate, w_ffn_up, w_ffn_down,
           norm_mix_pre, norm_mix_post, norm_ffn_pre, norm_ffn_post):
    B, S, D = x.shape
    scale = 1.0 / np.sqrt(HEAD_DIM)
    col_scale = np.ones((1, QKV_COLS), np.float32)
    col_scale[:, :DIL_WIDTH] = scale
    col_scale[:, 3 * DIL_WIDTH:3 * DIL_WIDTH + FOX_WIDTH] = scale * LOG2E
    col_scale = jnp.asarray(col_scale)
    for layer in range(w_in.shape[0]):
        w_qkv, w_gates, wf = _prep_w_in(w_in[layer:layer + 1], col_scale)
        bf = jnp.zeros((F_ROWS, 1), F32).at[:N_FOX_HEADS, 0].set(b_forget[layer])

        (qa0, qa1, qa2, ka0, ka1, ka2, va0, va1, va2, qb, kb, vb, ft) = _in_proj(
            x, norm_mix_pre[layer][None, :], w_qkv, wf, bf)

        outs, lses = [], []
        for q, k, v in ((qa0, ka0, va0), (qa1, ka1, va1), (qa2, ka2, va2)):
            o_g, l_g = _dilated_group(q, k, v)
            outs.append(o_g)
            lses.append(l_g)
        ob = _fox(qb, kb, vb, ft)

        x = _mix(x, outs, lses, ob, w_gates,
                 w_proj_a[layer].astype(BF16), w_proj_b[layer].astype(BF16),
                 w_out[layer].astype(BF16), norm_mix_pre[layer][None, :],
                 norm_mix_post[layer][None, :])
        x = _ffn(x.reshape(B * S, D), norm_ffn_pre[layer][None, :], norm_ffn_post[layer][None, :],
                 w_ffn_gate[layer].astype(BF16), w_ffn_up[layer].astype(BF16),
                 w_ffn_down[layer].astype(BF16)).reshape(B, S, D)
    return x
```

```python
import functools

import numpy as np
import jax
import jax.numpy as jnp
from jax import lax
from jax.experimental import pallas as pl
from jax.experimental.pallas import tpu as pltpu

D_MODEL = 1024
HEAD_DIM = 64
DIL_CONFIGS = ((128, 1), (512, 4), (2048, 16))
N_DIL_GROUPS = 3
GROUP_WIDTH = 256
N_FOX_HEADS = 8
FOX_WIDTH = 512
BLOCK = 128
ROPE_THETA = 500000.0
ROPE_DIM = 16
D_FF = 2816
EPS = 1e-6
NEG_INF = -1e30

DIL_WIDTH = N_DIL_GROUPS * GROUP_WIDTH
QKV_COLS = 3 * DIL_WIDTH + 3 * FOX_WIDTH
GATE_COL0 = QKV_COLS + N_FOX_HEADS
F_ROWS = 16
LANES = 128
MXU_COLS = 256
VMEM_LIMIT = 56 * 1024 * 1024
VT_ROWS = HEAD_DIM + 16
LOG2E = 1.4426950408889634

F32 = jnp.float32
BF16 = jnp.bfloat16
_NT = (((1,), (1,)), ((), ()))


def _rmsnorm(x, gain):
    return x * lax.rsqrt(jnp.mean(x * x, axis=-1, keepdims=True) + EPS) * gain


def _log_sigmoid(z):
    return jnp.minimum(z, 0.0) - jnp.log1p(jnp.exp(-jnp.abs(z)))


def _prep_w_in_kernel(w_ref, scale_ref, qkv_ref, gates_ref, wf_ref):
    w = w_ref[0]
    qkv_ref[...] = (w[:, :QKV_COLS] * scale_ref[...]).astype(BF16)
    gates_ref[...] = w[:, GATE_COL0:].astype(BF16)
    lane = lax.broadcasted_iota(jnp.int32, (w.shape[0], LANES), 1)
    wf_ref[...] = jnp.where(lane < N_FOX_HEADS, w[:, QKV_COLS:QKV_COLS + LANES], 0.0).astype(BF16)


def _prep_w_in(w_in_layer, col_scale, rows=128):
    _, D, C = w_in_layer.shape
    return pl.pallas_call(
        _prep_w_in_kernel,
        grid=(D // rows,),
        in_specs=[pl.BlockSpec((1, rows, C), lambda i: (0, i, 0)),
                  pl.BlockSpec((1, QKV_COLS), lambda i: (0, 0))],
        out_specs=[pl.BlockSpec((rows, QKV_COLS), lambda i: (i, 0)),
                   pl.BlockSpec((rows, 2 * D), lambda i: (i, 0)),
                   pl.BlockSpec((rows, LANES), lambda i: (i, 0))],
        out_shape=[jax.ShapeDtypeStruct((D, QKV_COLS), BF16),
                   jax.ShapeDtypeStruct((D, 2 * D), BF16),
                   jax.ShapeDtypeStruct((D, LANES), BF16)],
        compiler_params=pltpu.CompilerParams(dimension_semantics=("arbitrary",)),
        name="prep_w_in",
    )(w_in_layer, col_scale)


def _in_proj_kernel(x_ref, gain_ref, w_ref, wf_ref, bf_ref, cos_ref, sa_ref, sb_ref,
                    qa0, qa1, qa2, ka0, ka1, ka2, va0, va1, va2,
                    qb_ref, kb_ref, vb_ref, ft_ref, carry_ref, *stage_refs, tm):
    i = pl.program_id(1)
    h = _rmsnorm(x_ref[0], gain_ref[...]).astype(BF16)

    z = jnp.dot(h, wf_ref[...], preferred_element_type=F32).T[:F_ROWS] + bf_ref[...]
    logf = _log_sigmoid(z)
    lane = lax.broadcasted_iota(jnp.int32, (F_ROWS, LANES), 1)
    sums = []
    for j in range(tm // LANES):
        c = logf[:, j * LANES:(j + 1) * LANES]
        k = 1
        while k < LANES:
            c = c + jnp.where(lane >= k, pltpu.roll(c, k, 1), 0.0)
            k *= 2
        sums.append(c)
    carry = jnp.where(i == 0, 0.0, carry_ref[...])
    for j, c in enumerate(sums):
        c = c + carry
        ft_ref[0, :, j * LANES:(j + 1) * LANES] = c
        carry = jnp.broadcast_to(c[:, LANES - 1:LANES], (F_ROWS, LANES))
    carry_ref[...] = carry

    def proj(c0, width=MXU_COLS):
        return jnp.dot(h, w_ref[:, c0:c0 + width], preferred_element_type=F32)

    cos, sa, sb = cos_ref[...], sa_ref[...], sb_ref[...]

    def rope(y):
        halves = []
        for c in range(MXU_COLS // LANES):
            yc = y[:, c * LANES:(c + 1) * LANES]
            halves.append(yc * cos + pltpu.roll(yc, LANES - ROPE_DIM // 2, 1) * sa
                          + pltpu.roll(yc, ROPE_DIM // 2, 1) * sb)
        return jnp.concatenate(halves, axis=1)

    stage = iter(stage_refs)

    def put(o_ref, y, d):
        if d == 1:
            o_ref[0, 0] = y.astype(BF16)
            return
        st = next(stage)
        for c in range(GROUP_WIDTH // LANES):
            st[c] = y[:, c * LANES:(c + 1) * LANES]
        for r in range(d):
            for c in range(GROUP_WIDTH // LANES):
                o_ref[0, r, :, c * LANES:(c + 1) * LANES] = st[
                    c, pl.ds(r, tm // d, stride=d), :].astype(BF16)

    for g, (q_o, k_o, v_o) in enumerate(((qa0, ka0, va0), (qa1, ka1, va1), (qa2, ka2, va2))):
        d = DIL_CONFIGS[g][1]
        put(q_o, rope(proj(g * GROUP_WIDTH)), d)
        put(k_o, rope(proj(DIL_WIDTH + g * GROUP_WIDTH)), d)
        put(v_o, proj(2 * DIL_WIDTH + g * GROUP_WIDTH), d)
    base = 3 * DIL_WIDTH
    for s, o_ref in enumerate((qb_ref, kb_ref, vb_ref)):
        for c in range(FOX_WIDTH // MXU_COLS):
            o_ref[0, :, c * MXU_COLS:(c + 1) * MXU_COLS] = proj(
                base + s * FOX_WIDTH + c * MXU_COLS).astype(BF16)


def _rope_tables(seq):
    half = ROPE_DIM // 2
    inv_freq = np.power(ROPE_THETA, -np.arange(0, ROPE_DIM, 2, dtype=np.float64) / ROPE_DIM)
    ang = np.arange(seq, dtype=np.float64)[:, None] * inv_freq[None, :]
    cos = np.ones((seq, HEAD_DIM)); sa = np.zeros((seq, HEAD_DIM)); sb = np.zeros((seq, HEAD_DIM))
    cos[:, :half] = np.cos(ang); cos[:, half:ROPE_DIM] = np.cos(ang)
    sa[:, :half] = -np.sin(ang)
    sb[:, half:ROPE_DIM] = np.sin(ang)
    rep = LANES // HEAD_DIM
    return tuple(jnp.asarray(np.tile(t, (1, rep)), dtype=F32) for t in (cos, sa, sb))


def _in_proj(x, gain, w_main, wf, bf, tm=1024):
    B, S, D = x.shape
    cos, sa, sb = _rope_tables(S)
    const = lambda shape: pl.BlockSpec(shape, lambda b, i: (0,) * len(shape),
                                       pipeline_mode=pl.Buffered(1))
    row = lambda w: pl.BlockSpec((1, tm, w), lambda b, i: (b, i, 0))
    tab = pl.BlockSpec((tm, LANES), lambda b, i: (i, 0))
    dils = [d for _, d in DIL_CONFIGS]
    res_shape = [jax.ShapeDtypeStruct((B, d, S // d, GROUP_WIDTH), BF16) for d in dils] * 3
    res_spec = [pl.BlockSpec((1, d, tm // d, GROUP_WIDTH), lambda b, i: (b, 0, i, 0))
                for d in dils] * 3
    out_shape = (res_shape
                 + [jax.ShapeDtypeStruct((B, S, FOX_WIDTH), BF16)] * 3
                 + [jax.ShapeDtypeStruct((B, F_ROWS, S), F32)])
    out_specs = (res_spec + [row(FOX_WIDTH)] * 3
                 + [pl.BlockSpec((1, F_ROWS, tm), lambda b, i: (b, 0, i))])
    n_stage = 3 * sum(1 for d in dils if d > 1)
    return pl.pallas_call(
        functools.partial(_in_proj_kernel, tm=tm),
        grid=(B, S // tm),
        in_specs=[row(D), const((1, D)), const((D, QKV_COLS)), const((D, LANES)),
                  const((F_ROWS, 1)), tab, tab, tab],
        out_specs=out_specs,
        out_shape=out_shape,
        scratch_shapes=([pltpu.VMEM((F_ROWS, LANES), F32)]
                        + [pltpu.VMEM((GROUP_WIDTH // LANES, tm, LANES), F32)] * n_stage),
        compiler_params=pltpu.CompilerParams(
            dimension_semantics=("arbitrary", "arbitrary"), vmem_limit_bytes=VMEM_LIMIT),
        name="in_proj",
    )(x, gain, w_main, wf, bf, cos, sa, sb)


def _dilated_kernel(q_ref, k_ref, v_ref, o_ref, lse_ref, *, d, nb):
    lane = lax.broadcasted_iota(jnp.int32, (BLOCK, LANES), 1)
    lo_half = lane < HEAD_DIM

    kw = 2 * BLOCK
    qi = lax.broadcasted_iota(jnp.int32, (BLOCK, kw), 0)
    kj = lax.broadcasted_iota(jnp.int32, (BLOCK, kw), 1)
    mask_lead = kj <= qi
    mask_band = (kj >= qi) & (kj <= qi + BLOCK)
    mask_tail = (kj >= BLOCK) & (kj - BLOCK <= qi)

    def tiles(r, n):
        rows = slice(n * BLOCK, (n + 1) * BLOCK)
        for hp in range(GROUP_WIDTH // LANES):
            cols = slice(hp * LANES, (hp + 1) * LANES)
            q = q_ref[0, r, rows, cols]
            if nb == 1:
                r0 = r - r % 2
                kc = k_ref[0, r0:r0 + 2, :, cols].reshape(kw, LANES)
                vc = v_ref[0, r0:r0 + 2, :, cols].reshape(kw, LANES)
                valid = mask_lead if r % 2 == 0 else mask_tail
            else:
                k0 = max(n - 1, 0) * BLOCK
                kc = k_ref[0, r, k0:k0 + kw, cols]
                vc = v_ref[0, r, k0:k0 + kw, cols]
                valid = mask_lead if n == 0 else mask_band
            outs, lses = [], []
            for hh in range(2):
                qm = jnp.where(lo_half if hh == 0 else ~lo_half, q, jnp.zeros_like(q))
                s = lax.dot_general(qm, kc, _NT, preferred_element_type=F32)
                s = jnp.where(valid, s, NEG_INF)
                m = jnp.max(s, axis=1, keepdims=True)
                p = jnp.exp(s - m)
                den = jnp.sum(p, axis=1, keepdims=True)
                o = jnp.dot(p.astype(BF16), vc, preferred_element_type=F32)
                outs.append(o * (1.0 / den))
                lses.append(m + jnp.log(den))
            o_ref[0, r, rows, cols] = jnp.where(lo_half, outs[0], outs[1]).astype(BF16)
            lse_ref[0, r, rows, cols] = jnp.where(lo_half, lses[0], lses[1])

    for r in range(d):
        for n in range(nb):
            tiles(r, n)


def _dilated_group(q, k, v):
    B, d, L, W = q.shape
    spec = pl.BlockSpec((1, d, L, W), lambda b: (b, 0, 0, 0))
    return pl.pallas_call(
        functools.partial(_dilated_kernel, d=d, nb=L // BLOCK),
        grid=(B,),
        in_specs=[spec, spec, spec],
        out_specs=[spec, spec],
        out_shape=[jax.ShapeDtypeStruct(q.shape, BF16), jax.ShapeDtypeStruct(q.shape, F32)],
        compiler_params=pltpu.CompilerParams(
            dimension_semantics=("arbitrary",), vmem_limit_bytes=VMEM_LIMIT),
        name=f"dilated_d{d}",
    )(q, k, v)


def _fox_kernel(q_ref, k_ref, v_ref, ft_ref, o_ref, vt_ref, fcol_ref, *, tq, nblk, lookahead):
    p = pl.program_id(1)
    seq = nblk * tq
    lane = lax.broadcasted_iota(jnp.int32, (tq, LANES), 1)
    lo_half = lane < HEAD_DIM
    key = lax.broadcasted_iota(jnp.int32, (tq, tq), 0)
    qry = lax.broadcasted_iota(jnp.int32, (tq, tq), 1)
    causal = key <= qry

    for c in range(seq // tq):
        cs = slice(c * tq, (c + 1) * tq)
        vt = v_ref[0, cs, :].astype(F32).T.astype(BF16)
        for hh in range(2):
            vt_ref[hh, :HEAD_DIM, cs] = vt[hh * HEAD_DIM:(hh + 1) * HEAD_DIM]
    for hh in range(2):
        vt_ref[hh, HEAD_DIM:, :] = jnp.ones((VT_ROWS - HEAD_DIM, seq), BF16)
    frows = [ft_ref[0, pl.ds(2 * p + hh, 1), :] * LOG2E for hh in range(2)]
    for hh in range(2):
        for c in range(seq // LANES):
            cs = slice(c * LANES, (c + 1) * LANES)
            fcol_ref[hh, cs, :] = jnp.broadcast_to(frows[hh][:, cs], (LANES, LANES)).T

    def scores(t, i, hh):
        j = i - t
        ks = slice(j * tq, (j + 1) * tq)
        q = q_ref[0, i * tq:(i + 1) * tq, :]
        qm = jnp.where(lo_half if hh == 0 else ~lo_half, q, jnp.zeros_like(q))
        c0 = frows[hh][:, i * tq:i * tq + 1]
        bias = c0 - fcol_ref[hh, ks, :]
        st = lax.dot_general(k_ref[0, ks, :], qm, _NT, preferred_element_type=F32)
        st = st + jnp.concatenate([bias] * (tq // LANES), axis=1)
        return jnp.where(causal, st, NEG_INF) if t == 0 else st

    state = {}

    def update(t, i, hh, st):
        j = i - t
        vt = vt_ref[hh, :, j * tq:(j + 1) * tq]
        if t == 0:
            m = jnp.max(st, axis=0, keepdims=True)
            pe = jnp.exp2(st - m)
            acc = jnp.dot(vt, pe.astype(BF16), preferred_element_type=F32)
        else:
            m_old, acc_old = state[i, hh]
            m = jnp.maximum(m_old, jnp.max(st, axis=0, keepdims=True))
            pe = jnp.exp2(st - m)
            acc = jnp.exp2(m_old - m) * acc_old + jnp.dot(vt, pe.astype(BF16),
                                                          preferred_element_type=F32)
        state[i, hh] = (m, acc)
        if j == 0 and hh == 1:
            out_t = jnp.concatenate(
                [state[i, h][1][:HEAD_DIM] * (1.0 / state[i, h][1][HEAD_DIM:HEAD_DIM + 1])
                 for h in range(2)], axis=0)
            o_ref[0, i * tq:(i + 1) * tq, :] = out_t.T.astype(BF16)

    units = [(t, i, hh) for t in range(nblk) for i in range(t, nblk) for hh in range(2)]
    ready = {}
    for n in range(len(units) + lookahead):
        if n < len(units):
            ready[n] = scores(*units[n])
        if n >= lookahead:
            update(*units[n - lookahead], ready.pop(n - lookahead))


def _fox(q, k, v, ft, tq=256, lookahead=4):
    B, S, W = q.shape
    spec = pl.BlockSpec((1, S, LANES), lambda b, p: (b, 0, p))
    return pl.pallas_call(
        functools.partial(_fox_kernel, tq=tq, nblk=S // tq, lookahead=lookahead),
        grid=(B, W // LANES),
        in_specs=[spec, spec, spec,
                  pl.BlockSpec((1, F_ROWS, S), lambda b, p: (b, 0, 0))],
        out_specs=spec,
        out_shape=jax.ShapeDtypeStruct((B, S, W), BF16),
        scratch_shapes=[pltpu.VMEM((2, VT_ROWS, S), BF16), pltpu.VMEM((2, S, LANES), F32)],
        compiler_params=pltpu.CompilerParams(
            dimension_semantics=("arbitrary", "arbitrary"), vmem_limit_bytes=VMEM_LIMIT),
        name="fox",
    )(q, k, v, ft)


def _mix_kernel(x_ref, o0, o1, o2, l0, l1, l2, ob_ref, wg_ref, wa_ref, wb_ref, wo_ref,
                gpre_ref, gpost_ref, y_ref, *stage_refs, tm):
    stage = iter(stage_refs)
    x = x_ref[0]
    h = _rmsnorm(x, gpre_ref[...]).astype(BF16)

    def tokens(ref):
        d = ref.shape[1]
        if d == 1:
            return ref[0, 0].astype(F32)
        st = next(stage)
        for r in range(d):
            for c in range(GROUP_WIDTH // LANES):
                st[c, pl.ds(r, tm // d, stride=d), :] = ref[
                    0, r, :, c * LANES:(c + 1) * LANES].astype(F32)
        return jnp.concatenate([st[c] for c in range(GROUP_WIDTH // LANES)], axis=1)

    la, lb, lc = tokens(l0), tokens(l1), tokens(l2)
    mx = jnp.maximum(jnp.maximum(la, lb), lc)
    e0, e1, e2 = jnp.exp(la - mx), jnp.exp(lb - mx), jnp.exp(lc - mx)
    inv = 1.0 / (e0 + e1 + e2)
    oa = ((e0 * tokens(o0) + e1 * tokens(o1) + e2 * tokens(o2)) * inv).astype(BF16)
    ob = ob_ref[0]
    merged = []
    for c in range(D_MODEL // MXU_COLS):
        cols = slice(c * MXU_COLS, (c + 1) * MXU_COLS)
        ga = jnp.dot(h, wg_ref[:, cols], preferred_element_type=F32)
        gb = jnp.dot(h, wg_ref[:, D_MODEL + c * MXU_COLS:D_MODEL + (c + 1) * MXU_COLS],
                     preferred_element_type=F32)
        a = jnp.dot(oa, wa_ref[:, cols], preferred_element_type=F32)
        b = jnp.dot(ob, wb_ref[:, cols], preferred_element_type=F32)
        merged.append((jax.nn.sigmoid(ga) * a + jax.nn.sigmoid(gb) * b).astype(BF16))
    mix = jnp.dot(jnp.concatenate(merged, axis=1), wo_ref[...], preferred_element_type=F32)
    y_ref[0] = x + _rmsnorm(mix, gpost_ref[...])


def _mix(x, outs, lses, ob, wg, wa, wb, wo, gain_pre, gain_post, tm=1024):
    B, S, D = x.shape
    row = lambda w: pl.BlockSpec((1, tm, w), lambda b, i: (b, i, 0))
    const = lambda shape: pl.BlockSpec(shape, lambda b, i: (0, 0), pipeline_mode=pl.Buffered(1))
    res = lambda t: pl.BlockSpec((1, t.shape[1], tm // t.shape[1], GROUP_WIDTH),
                                 lambda b, i: (b, 0, i, 0))
    n_stage = 2 * sum(1 for t in outs if t.shape[1] > 1)
    return pl.pallas_call(
        functools.partial(_mix_kernel, tm=tm),
        grid=(B, S // tm),
        in_specs=([row(D)] + [res(t) for t in outs] + [res(t) for t in lses]
                  + [row(FOX_WIDTH), const((D, 2 * D)),
                     const((GROUP_WIDTH, D)), const((FOX_WIDTH, D)), const((D, D)),
                     const((1, D)), const((1, D))]),
        out_specs=row(D),
        out_shape=jax.ShapeDtypeStruct((B, S, D), F32),
        scratch_shapes=[pltpu.VMEM((GROUP_WIDTH // LANES, tm, LANES), F32)] * n_stage,
        compiler_params=pltpu.CompilerParams(
            dimension_semantics=("arbitrary", "arbitrary"), vmem_limit_bytes=VMEM_LIMIT),
        name="mix",
    )(x, *outs, *lses, ob, wg, wa, wb, wo, gain_pre, gain_post)


def _ffn_kernel(x_ref, g1_ref, g2_ref, wg_ref, wu_ref, wd_ref, o_ref):
    x = x_ref[...]
    h = _rmsnorm(x, g1_ref[...]).astype(BF16)
    acc = jnp.zeros(x.shape, F32)
    for c in range(D_FF // MXU_COLS):
        cols = slice(c * MXU_COLS, (c + 1) * MXU_COLS)
        g = jnp.dot(h, wg_ref[:, cols], preferred_element_type=F32)
        u = jnp.dot(h, wu_ref[:, cols], preferred_element_type=F32)
        a = (g * jax.nn.sigmoid(g) * u).astype(BF16)
        acc = acc + jnp.dot(a, wd_ref[cols, :], preferred_element_type=F32)
    o_ref[...] = x + _rmsnorm(acc, g2_ref[...])


def _ffn(x2, g1, g2, wg, wu, wd, tm=1024):
    T, D = x2.shape
    row = pl.BlockSpec((tm, D), lambda i: (i, 0))
    const = lambda shape: pl.BlockSpec(shape, lambda i: (0, 0), pipeline_mode=pl.Buffered(1))
    return pl.pallas_call(
        _ffn_kernel,
        grid=(T // tm,),
        in_specs=[row, const((1, D)), const((1, D)), const((D, D_FF)), const((D, D_FF)),
                  const((D_FF, D))],
        out_specs=row,
        out_shape=jax.ShapeDtypeStruct((T, D), F32),
        compiler_params=pltpu.CompilerParams(
            dimension_semantics=("arbitrary",), vmem_limit_bytes=VMEM_LIMIT),
        name="ffn",
    )(x2, g1, g2, wg, wu, wd)


def kernel(x, w_in, w_proj_a, w_proj_b, w_out, b_forget, w_ffn_gate, w_ffn_up, w_ffn_down,
           norm_mix_pre, norm_mix_post, norm_ffn_pre, norm_ffn_post):
    B, S, D = x.shape
    scale = 1.0 / np.sqrt(HEAD_DIM)
    col_scale = np.ones((1, QKV_COLS), np.float32)
    col_scale[:, :DIL_WIDTH] = scale
    col_scale[:, 3 * DIL_WIDTH:3 * DIL_WIDTH + FOX_WIDTH] = scale * LOG2E
    col_scale = jnp.asarray(col_scale)
    for layer in range(w_in.shape[0]):
        w_qkv, w_gates, wf = _prep_w_in(w_in[layer:layer + 1], col_scale)
        bf = jnp.zeros((F_ROWS, 1), F32).at[:N_FOX_HEADS, 0].set(b_forget[layer])

        (qa0, qa1, qa2, ka0, ka1, ka2, va0, va1, va2, qb, kb, vb, ft) = _in_proj(
            x, norm_mix_pre[layer][None, :], w_qkv, wf, bf)

        outs, lses = [], []
        for q, k, v in ((qa0, ka0, va0), (qa1, ka1, va1), (qa2, ka2, va2)):
            o_g, l_g = _dilated_group(q, k, v)
            outs.append(o_g)
            lses.append(l_g)
        ob = _fox(qb, kb, vb, ft)

        x = _mix(x, outs, lses, ob, w_gates,
                 w_proj_a[layer].astype(BF16), w_proj_b[layer].astype(BF16),
                 w_out[layer].astype(BF16), norm_mix_pre[layer][None, :],
                 norm_mix_post[layer][None, :])
        x = _ffn(x.reshape(B * S, D), norm_ffn_pre[layer][None, :], norm_ffn_post[layer][None, :],
                 w_ffn_gate[layer].astype(BF16), w_ffn_up[layer].astype(BF16),
                 w_ffn_down[layer].astype(BF16)).reshape(B, S, D)
    return x
```

```python
import functools

import numpy as np
import jax
import jax.numpy as jnp
from jax import lax
from jax.experimental import pallas as pl
from jax.experimental.pallas import tpu as pltpu

D_MODEL = 1024
HEAD_DIM = 64
DIL_CONFIGS = ((128, 1), (512, 4), (2048, 16))
N_DIL_GROUPS = 3
GROUP_WIDTH = 256
N_FOX_HEADS = 8
FOX_WIDTH = 512
BLOCK = 128
ROPE_THETA = 500000.0
ROPE_DIM = 16
D_FF = 2816
EPS = 1e-6
NEG_INF = -1e30

DIL_WIDTH = N_DIL_GROUPS * GROUP_WIDTH
QKV_COLS = 3 * DIL_WIDTH + 3 * FOX_WIDTH
GATE_COL0 = QKV_COLS + N_FOX_HEADS
F_ROWS = 16
LANES = 128
MXU_COLS = 256
VMEM_LIMIT = 56 * 1024 * 1024
VT_ROWS = HEAD_DIM + 16
LOG2E = 1.4426950408889634

F32 = jnp.float32
BF16 = jnp.bfloat16
_NT = (((1,), (1,)), ((), ()))


def _rmsnorm(x, gain):
    return x * lax.rsqrt(jnp.mean(x * x, axis=-1, keepdims=True) + EPS) * gain


def _log_sigmoid(z):
    return jnp.minimum(z, 0.0) - jnp.log1p(jnp.exp(-jnp.abs(z)))


def _prep_w_in_kernel(w_ref, scale_ref, qkv_ref, gates_ref, wf_ref):
    w = w_ref[0]
    qkv_ref[...] = (w[:, :QKV_COLS] * scale_ref[...]).astype(BF16)
    gates_ref[...] = w[:, GATE_COL0:].astype(BF16)
    lane = lax.broadcasted_iota(jnp.int32, (w.shape[0], LANES), 1)
    wf_ref[...] = jnp.where(lane < N_FOX_HEADS, w[:, QKV_COLS:QKV_COLS + LANES], 0.0).astype(BF16)


def _prep_w_in(w_in_layer, col_scale, rows=128):
    _, D, C = w_in_layer.shape
    return pl.pallas_call(
        _prep_w_in_kernel,
        grid=(D // rows,),
        in_specs=[pl.BlockSpec((1, rows, C), lambda i: (0, i, 0)),
                  pl.BlockSpec((1, QKV_COLS), lambda i: (0, 0))],
        out_specs=[pl.BlockSpec((rows, QKV_COLS), lambda i: (i, 0)),
                   pl.BlockSpec((rows, 2 * D), lambda i: (i, 0)),
                   pl.BlockSpec((rows, LANES), lambda i: (i, 0))],
        out_shape=[jax.ShapeDtypeStruct((D, QKV_COLS), BF16),
                   jax.ShapeDtypeStruct((D, 2 * D), BF16),
                   jax.ShapeDtypeStruct((D, LANES), BF16)],
        compiler_params=pltpu.CompilerParams(dimension_semantics=("arbitrary",)),
        name="prep_w_in",
    )(w_in_layer, col_scale)


def _in_proj_kernel(x_ref, gain_ref, w_ref, wf_ref, bf_ref, cos_ref, sa_ref, sb_ref,
                    qa0, qa1, qa2, ka0, ka1, ka2, va0, va1, va2,
                    qb_ref, kb_ref, vb_ref, ft_ref, carry_ref, *stage_refs, tm):
    i = pl.program_id(1)
    h = _rmsnorm(x_ref[0], gain_ref[...]).astype(BF16)

    z = jnp.dot(h, wf_ref[...], preferred_element_type=F32).T[:F_ROWS] + bf_ref[...]
    logf = _log_sigmoid(z)
    lane = lax.broadcasted_iota(jnp.int32, (F_ROWS, LANES), 1)
    sums = []
    for j in range(tm // LANES):
        c = logf[:, j * LANES:(j + 1) * LANES]
        k = 1
        while k < LANES:
            c = c + jnp.where(lane >= k, pltpu.roll(c, k, 1), 0.0)
            k *= 2
        sums.append(c)
    carry = jnp.where(i == 0, 0.0, carry_ref[...])
    for j, c in enumerate(sums):
        c = c + carry
        ft_ref[0, :, j * LANES:(j + 1) * LANES] = c
        carry = jnp.broadcast_to(c[:, LANES - 1:LANES], (F_ROWS, LANES))
    carry_ref[...] = carry

    def proj(c0, width=MXU_COLS):
        return jnp.dot(h, w_ref[:, c0:c0 + width], preferred_element_type=F32)

    cos, sa, sb = cos_ref[...], sa_ref[...], sb_ref[...]

    def rope(y):
        halves = []
        for c in range(MXU_COLS // LANES):
            yc = y[:, c * LANES:(c + 1) * LANES]
            halves.append(yc * cos + pltpu.roll(yc, LANES - ROPE_DIM // 2, 1) * sa
                          + pltpu.roll(yc, ROPE_DIM // 2, 1) * sb)
        return jnp.concatenate(halves, axis=1)

    stage = iter(stage_refs)

    def put(o_ref, y, d):
        if d == 1:
            o_ref[0, 0] = y.astype(BF16)
            return
        st = next(stage)
        for c in range(GROUP_WIDTH // LANES):
            st[c] = y[:, c * LANES:(c + 1) * LANES]
        for r in range(d):
            for c in range(GROUP_WIDTH // LANES):
                o_ref[0, r, :, c * LANES:(c + 1) * LANES] = st[
                    c, pl.ds(r, tm // d, stride=d), :].astype(BF16)

    for g, (q_o, k_o, v_o) in enumerate(((qa0, ka0, va0), (qa1, ka1, va1), (qa2, ka2, va2))):
        d = DIL_CONFIGS[g][1]
        put(q_o, rope(proj(g * GROUP_WIDTH)), d)
        put(k_o, rope(proj(DIL_WIDTH + g * GROUP_WIDTH)), d)
        put(v_o, proj(2 * DIL_WIDTH + g * GROUP_WIDTH), d)
    base = 3 * DIL_WIDTH
    for s, o_ref in enumerate((qb_ref, kb_ref, vb_ref)):
        for c in range(FOX_WIDTH // MXU_COLS):
            o_ref[0, :, c * MXU_COLS:(c + 1) * MXU_COLS] = proj(
                base + s * FOX_WIDTH + c * MXU_COLS).astype(BF16)


def _rope_tables(seq):
    half = ROPE_DIM // 2
    inv_freq = np.power(ROPE_THETA, -np.arange(0, ROPE_DIM, 2, dtype=np.float64) / ROPE_DIM)
    ang = np.arange(seq, dtype=np.float64)[:, None] * inv_freq[None, :]
    cos = np.ones((seq, HEAD_DIM)); sa = np.zeros((seq, HEAD_DIM)); sb = np.zeros((seq, HEAD_DIM))
    cos[:, :half] = np.cos(ang); cos[:, half:ROPE_DIM] = np.cos(ang)
    sa[:, :half] = -np.sin(ang)
    sb[:, half:ROPE_DIM] = np.sin(ang)
    rep = LANES // HEAD_DIM
    return tuple(jnp.asarray(np.tile(t, (1, rep)), dtype=F32) for t in (cos, sa, sb))


def _in_proj(x, gain, w_main, wf, bf, tm=1024):
    B, S, D = x.shape
    cos, sa, sb = _rope_tables(S)
    const = lambda shape: pl.BlockSpec(shape, lambda b, i: (0,) * len(shape),
                                       pipeline_mode=pl.Buffered(1))
    row = lambda w: pl.BlockSpec((1, tm, w), lambda b, i: (b, i, 0))
    tab = pl.BlockSpec((tm, LANES), lambda b, i: (i, 0))
    dils = [d for _, d in DIL_CONFIGS]
    res_shape = [jax.ShapeDtypeStruct((B, d, S // d, GROUP_WIDTH), BF16) for d in dils] * 3
    res_spec = [pl.BlockSpec((1, d, tm // d, GROUP_WIDTH), lambda b, i: (b, 0, i, 0))
                for d in dils] * 3
    out_shape = (res_shape
                 + [jax.ShapeDtypeStruct((B, S, FOX_WIDTH), BF16)] * 3
                 + [jax.ShapeDtypeStruct((B, F_ROWS, S), F32)])
    out_specs = (res_spec + [row(FOX_WIDTH)] * 3
                 + [pl.BlockSpec((1, F_ROWS, tm), lambda b, i: (b, 0, i))])
    n_stage = 3 * sum(1 for d in dils if d > 1)
    return pl.pallas_call(
        functools.partial(_in_proj_kernel, tm=tm),
        grid=(B, S // tm),
        in_specs=[row(D), const((1, D)), const((D, QKV_COLS)), const((D, LANES)),
                  const((F_ROWS, 1)), tab, tab, tab],
        out_specs=out_specs,
        out_shape=out_shape,
        scratch_shapes=([pltpu.VMEM((F_ROWS, LANES), F32)]
                        + [pltpu.VMEM((GROUP_WIDTH // LANES, tm, LANES), F32)] * n_stage),
        compiler_params=pltpu.CompilerParams(
            dimension_semantics=("arbitrary", "arbitrary"), vmem_limit_bytes=VMEM_LIMIT),
        name="in_proj",
    )(x, gain, w_main, wf, bf, cos, sa, sb)


def _dilated_kernel(q_ref, k_ref, v_ref, o_ref, lse_ref, vt_ref, *, d, nb, lookahead):
    lane = lax.broadcasted_iota(jnp.int32, (BLOCK, LANES), 1)
    lo_half = lane < HEAD_DIM
    kw = 2 * BLOCK
    n_pairs = GROUP_WIDTH // LANES

    kj = lax.broadcasted_iota(jnp.int32, (kw, kw), 0)
    qi = lax.broadcasted_iota(jnp.int32, (kw, kw), 1) % BLOCK
    kj1 = lax.broadcasted_iota(jnp.int32, (BLOCK, kw), 0)
    qi1 = lax.broadcasted_iota(jnp.int32, (BLOCK, kw), 1) % BLOCK
    bias_lead = jnp.where(kj1 <= qi1, 0.0, NEG_INF)
    bias_band = jnp.where((kj >= qi) & (kj <= qi + BLOCK), 0.0, NEG_INF)

    for r in range(d):
        for hp in range(n_pairs):
            cols = slice(hp * LANES, (hp + 1) * LANES)
            for n in range(nb):
                rows = slice(n * BLOCK, (n + 1) * BLOCK)
                vt = v_ref[0, r, rows, cols].astype(F32).T.astype(BF16)
                vt_ref[r, hp, 0:HEAD_DIM, rows] = vt[:HEAD_DIM]
                vt_ref[r, hp, VT_ROWS:VT_ROWS + HEAD_DIM, rows] = vt[HEAD_DIM:]
            ones = jnp.ones((VT_ROWS - HEAD_DIM, nb * BLOCK), BF16)
            vt_ref[r, hp, HEAD_DIM:VT_ROWS, :] = ones
            vt_ref[r, hp, VT_ROWS + HEAD_DIM:, :] = ones

    def scores(r, n, hp):
        cols = slice(hp * LANES, (hp + 1) * LANES)
        q = q_ref[0, r, n * BLOCK:(n + 1) * BLOCK, cols]
        zero = jnp.zeros_like(q)
        qm = jnp.concatenate([jnp.where(lo_half, q, zero), jnp.where(lo_half, zero, q)], axis=0)
        keys = slice(0, BLOCK) if n == 0 else slice((n - 1) * BLOCK, (n + 1) * BLOCK)
        st = lax.dot_general(k_ref[0, r, keys, cols], qm, _NT,
                             preferred_element_type=F32)
        return st + (bias_lead if n == 0 else bias_band)

    def finish(r, n, hp, st):
        cols = slice(hp * LANES, (hp + 1) * LANES)
        rows = slice(n * BLOCK, (n + 1) * BLOCK)
        keys = slice(0, BLOCK) if n == 0 else slice((n - 1) * BLOCK, (n + 1) * BLOCK)
        vt = vt_ref[r, hp, :, keys]
        m = jnp.max(st, axis=0, keepdims=True)
        pe = jnp.exp2(st - m).astype(BF16)
        acc = jnp.dot(vt, pe, preferred_element_type=F32)
        out_t, lse_t = [], []
        for hh in range(2):
            a = acc[hh * VT_ROWS:(hh + 1) * VT_ROWS, hh * BLOCK:(hh + 1) * BLOCK]
            den = a[HEAD_DIM:HEAD_DIM + 1]
            out_t.append(a[:HEAD_DIM] * (1.0 / den))
            lse = m[:, hh * BLOCK:(hh + 1) * BLOCK] + jnp.log2(den)
            lse_t.append(jnp.broadcast_to(lse, (HEAD_DIM, BLOCK)))
        o_ref[0, r, rows, cols] = jnp.concatenate(out_t, axis=0).T.astype(BF16)
        lse_ref[0, r, rows, cols] = jnp.concatenate(lse_t, axis=0).T

    units = [(r, n, hp) for r in range(d) for n in range(nb) for hp in range(n_pairs)]
    ready = {}
    for idx in range(len(units) + lookahead):
        if idx < len(units):
            ready[idx] = scores(*units[idx])
        if idx >= lookahead:
            finish(*units[idx - lookahead], ready.pop(idx - lookahead))


def _dilated_group(q, k, v, lookahead=4):
    B, d, L, W = q.shape
    spec = pl.BlockSpec((1, d, L, W), lambda b: (b, 0, 0, 0))
    return pl.pallas_call(
        functools.partial(_dilated_kernel, d=d, nb=L // BLOCK, lookahead=lookahead),
        grid=(B,),
        in_specs=[spec, spec, spec],
        out_specs=[spec, spec],
        out_shape=[jax.ShapeDtypeStruct(q.shape, BF16), jax.ShapeDtypeStruct(q.shape, F32)],
        scratch_shapes=[pltpu.VMEM((d, W // LANES, 2 * VT_ROWS, L), BF16)],
        compiler_params=pltpu.CompilerParams(
            dimension_semantics=("arbitrary",), vmem_limit_bytes=VMEM_LIMIT),
        name=f"dilated_d{d}",
    )(q, k, v)


def _fox_kernel(q_ref, k_ref, v_ref, ft_ref, o_ref, vt_ref, fcol_ref, *, tq, nblk, lookahead):
    p = pl.program_id(1)
    seq = nblk * tq
    lane = lax.broadcasted_iota(jnp.int32, (tq, LANES), 1)
    lo_half = lane < HEAD_DIM
    key = lax.broadcasted_iota(jnp.int32, (tq, tq), 0)
    qry = lax.broadcasted_iota(jnp.int32, (tq, tq), 1)
    causal = key <= qry

    for c in range(seq // tq):
        cs = slice(c * tq, (c + 1) * tq)
        vt = v_ref[0, cs, :].astype(F32).T.astype(BF16)
        for hh in range(2):
            vt_ref[hh, :HEAD_DIM, cs] = vt[hh * HEAD_DIM:(hh + 1) * HEAD_DIM]
    for hh in range(2):
        vt_ref[hh, HEAD_DIM:, :] = jnp.ones((VT_ROWS - HEAD_DIM, seq), BF16)
    frows = [ft_ref[0, pl.ds(2 * p + hh, 1), :] * LOG2E for hh in range(2)]
    for hh in range(2):
        for c in range(seq // LANES):
            cs = slice(c * LANES, (c + 1) * LANES)
            fcol_ref[hh, cs, :] = jnp.broadcast_to(frows[hh][:, cs], (LANES, LANES)).T

    def scores(t, i, hh):
        j = i - t
        ks = slice(j * tq, (j + 1) * tq)
        q = q_ref[0, i * tq:(i + 1) * tq, :]
        qm = jnp.where(lo_half if hh == 0 else ~lo_half, q, jnp.zeros_like(q))
        c0 = frows[hh][:, i * tq:i * tq + 1]
        bias = c0 - fcol_ref[hh, ks, :]
        st = lax.dot_general(k_ref[0, ks, :], qm, _NT, preferred_element_type=F32)
        st = st + jnp.concatenate([bias] * (tq // LANES), axis=1)
        return jnp.where(causal, st, NEG_INF) if t == 0 else st

    state = {}

    def update(t, i, hh, st):
        j = i - t
        vt = vt_ref[hh, :, j * tq:(j + 1) * tq]
        if t == 0:
            m = jnp.max(st, axis=0, keepdims=True)
            pe = jnp.exp2(st - m)
            acc = jnp.dot(vt, pe.astype(BF16), preferred_element_type=F32)
        else:
            m_old, acc_old = state[i, hh]
            m = jnp.maximum(m_old, jnp.max(st, axis=0, keepdims=True))
            pe = jnp.exp2(st - m)
            acc = jnp.exp2(m_old - m) * acc_old + jnp.dot(vt, pe.astype(BF16),
                                                          preferred_element_type=F32)
        state[i, hh] = (m, acc)
        if j == 0 and hh == 1:
            out_t = jnp.concatenate(
                [state[i, h][1][:HEAD_DIM] * (1.0 / state[i, h][1][HEAD_DIM:HEAD_DIM + 1])
                 for h in range(2)], axis=0)
            o_ref[0, i * tq:(i + 1) * tq, :] = out_t.T.astype(BF16)

    units = [(t, i, hh) for t in range(nblk) for i in range(t, nblk) for hh in range(2)]
    ready = {}
    for n in range(len(units) + lookahead):
        if n < len(units):
            ready[n] = scores(*units[n])
        if n >= lookahead:
            update(*units[n - lookahead], ready.pop(n - lookahead))


def _fox(q, k, v, ft, tq=256, lookahead=4):
    B, S, W = q.shape
    spec = pl.BlockSpec((1, S, LANES), lambda b, p: (b, 0, p))
    return pl.pallas_call(
        functools.partial(_fox_kernel, tq=tq, nblk=S // tq, lookahead=lookahead),
        grid=(B, W // LANES),
        in_specs=[spec, spec, spec,
                  pl.BlockSpec((1, F_ROWS, S), lambda b, p: (b, 0, 0))],
        out_specs=spec,
        out_shape=jax.ShapeDtypeStruct((B, S, W), BF16),
        scratch_shapes=[pltpu.VMEM((2, VT_ROWS, S), BF16), pltpu.VMEM((2, S, LANES), F32)],
        compiler_params=pltpu.CompilerParams(
            dimension_semantics=("arbitrary", "arbitrary"), vmem_limit_bytes=VMEM_LIMIT),
        name="fox",
    )(q, k, v, ft)


def _mix_kernel(x_ref, o0, o1, o2, l0, l1, l2, ob_ref, wg_ref, wa_ref, wb_ref, wo_ref,
                gpre_ref, gpost_ref, y_ref, *stage_refs, tm):
    stage = iter(stage_refs)
    x = x_ref[0]
    h = _rmsnorm(x, gpre_ref[...]).astype(BF16)

    def tokens(ref):
        d = ref.shape[1]
        if d == 1:
            return ref[0, 0].astype(F32)
        st = next(stage)
        for r in range(d):
            for c in range(GROUP_WIDTH // LANES):
                st[c, pl.ds(r, tm // d, stride=d), :] = ref[
                    0, r, :, c * LANES:(c + 1) * LANES].astype(F32)
        return jnp.concatenate([st[c] for c in range(GROUP_WIDTH // LANES)], axis=1)

    la, lb, lc = tokens(l0), tokens(l1), tokens(l2)
    mx = jnp.maximum(jnp.maximum(la, lb), lc)
    e0, e1, e2 = jnp.exp2(la - mx), jnp.exp2(lb - mx), jnp.exp2(lc - mx)
    inv = 1.0 / (e0 + e1 + e2)
    oa = ((e0 * tokens(o0) + e1 * tokens(o1) + e2 * tokens(o2)) * inv).astype(BF16)
    ob = ob_ref[0]
    merged = []
    for c in range(D_MODEL // MXU_COLS):
        cols = slice(c * MXU_COLS, (c + 1) * MXU_COLS)
        ga = jnp.dot(h, wg_ref[:, cols], preferred_element_type=F32)
        gb = jnp.dot(h, wg_ref[:, D_MODEL + c * MXU_COLS:D_MODEL + (c + 1) * MXU_COLS],
                     preferred_element_type=F32)
        a = jnp.dot(oa, wa_ref[:, cols], preferred_element_type=F32)
        b = jnp.dot(ob, wb_ref[:, cols], preferred_element_type=F32)
        merged.append((jax.nn.sigmoid(ga) * a + jax.nn.sigmoid(gb) * b).astype(BF16))
    mix = jnp.dot(jnp.concatenate(merged, axis=1), wo_ref[...], preferred_element_type=F32)
    y_ref[0] = x + _rmsnorm(mix, gpost_ref[...])


def _mix(x, outs, lses, ob, wg, wa, wb, wo, gain_pre, gain_post, tm=1024):
    B, S, D = x.shape
    row = lambda w: pl.BlockSpec((1, tm, w), lambda b, i: (b, i, 0))
    const = lambda shape: pl.BlockSpec(shape, lambda b, i: (0, 0), pipeline_mode=pl.Buffered(1))
    res = lambda t: pl.BlockSpec((1, t.shape[1], tm // t.shape[1], GROUP_WIDTH),
                                 lambda b, i: (b, 0, i, 0))
    n_stage = 2 * sum(1 for t in outs if t.shape[1] > 1)
    return pl.pallas_call(
        functools.partial(_mix_kernel, tm=tm),
        grid=(B, S // tm),
        in_specs=([row(D)] + [res(t) for t in outs] + [res(t) for t in lses]
                  + [row(FOX_WIDTH), const((D, 2 * D)),
                     const((GROUP_WIDTH, D)), const((FOX_WIDTH, D)), const((D, D)),
                     const((1, D)), const((1, D))]),
        out_specs=row(D),
        out_shape=jax.ShapeDtypeStruct((B, S, D), F32),
        scratch_shapes=[pltpu.VMEM((GROUP_WIDTH // LANES, tm, LANES), F32)] * n_stage,
        compiler_params=pltpu.CompilerParams(
            dimension_semantics=("arbitrary", "arbitrary"), vmem_limit_bytes=VMEM_LIMIT),
        name="mix",
    )(x, *outs, *lses, ob, wg, wa, wb, wo, gain_pre, gain_post)


def _ffn_kernel(x_ref, g1_ref, g2_ref, wg_ref, wu_ref, wd_ref, o_ref):
    x = x_ref[...]
    h = _rmsnorm(x, g1_ref[...]).astype(BF16)
    acc = jnp.zeros(x.shape, F32)
    for c in range(D_FF // MXU_COLS):
        cols = slice(c * MXU_COLS, (c + 1) * MXU_COLS)
        g = jnp.dot(h, wg_ref[:, cols], preferred_element_type=F32)
        u = jnp.dot(h, wu_ref[:, cols], preferred_element_type=F32)
        a = (g * jax.nn.sigmoid(g) * u).astype(BF16)
        acc = acc + jnp.dot(a, wd_ref[cols, :], preferred_element_type=F32)
    o_ref[...] = x + _rmsnorm(acc, g2_ref[...])


def _ffn(x2, g1, g2, wg, wu, wd, tm=1024):
    T, D = x2.shape
    row = pl.BlockSpec((tm, D), lambda i: (i, 0))
    const = lambda shape: pl.BlockSpec(shape, lambda i: (0, 0), pipeline_mode=pl.Buffered(1))
    return pl.pallas_call(
        _ffn_kernel,
        grid=(T // tm,),
        in_specs=[row, const((1, D)), const((1, D)), const((D, D_FF)), const((D, D_FF)),
                  const((D_FF, D))],
        out_specs=row,
        out_shape=jax.ShapeDtypeStruct((T, D), F32),
        compiler_params=pltpu.CompilerParams(
            dimension_semantics=("arbitrary",), vmem_limit_bytes=VMEM_LIMIT),
        name="ffn",
    )(x2, g1, g2, wg, wu, wd)


def kernel(x, w_in, w_proj_a, w_proj_b, w_out, b_forget, w_ffn_gate, w_ffn_up, w_ffn_down,
           norm_mix_pre, norm_mix_post, norm_ffn_pre, norm_ffn_post):
    B, S, D = x.shape
    scale = 1.0 / np.sqrt(HEAD_DIM)
    col_scale = np.ones((1, QKV_COLS), np.float32)
    col_scale[:, :DIL_WIDTH] = scale * LOG2E
    col_scale[:, 3 * DIL_WIDTH:3 * DIL_WIDTH + FOX_WIDTH] = scale * LOG2E
    col_scale = jnp.asarray(col_scale)
    for layer in range(w_in.shape[0]):
        w_qkv, w_gates, wf = _prep_w_in(w_in[layer:layer + 1], col_scale)
        bf = jnp.zeros((F_ROWS, 1), F32).at[:N_FOX_HEADS, 0].set(b_forget[layer])

        (qa0, qa1, qa2, ka0, ka1, ka2, va0, va1, va2, qb, kb, vb, ft) = _in_proj(
            x, norm_mix_pre[layer][None, :], w_qkv, wf, bf)

        outs, lses = [], []
        for q, k, v in ((qa0, ka0, va0), (qa1, ka1, va1), (qa2, ka2, va2)):
            o_g, l_g = _dilated_group(q, k, v)
            outs.append(o_g)
            lses.append(l_g)
        ob = _fox(qb, kb, vb, ft)

        x = _mix(x, outs, lses, ob, w_gates,
                 w_proj_a[layer].astype(BF16), w_proj_b[layer].astype(BF16),
                 w_out[layer].astype(BF16), norm_mix_pre[layer][None, :],
                 norm_mix_post[layer][None, :])
        x = _ffn(x.reshape(B * S, D), norm_ffn_pre[layer][None, :], norm_ffn_post[layer][None, :],
                 w_ffn_gate[layer].astype(BF16), w_ffn_up[layer].astype(BF16),
                 w_ffn_down[layer].astype(BF16)).reshape(B, S, D)
    return x
```

```python
import functools

import numpy as np
import jax
import jax.numpy as jnp
from jax import lax
from jax.experimental import pallas as pl
from jax.experimental.pallas import tpu as pltpu

D_MODEL = 1024
HEAD_DIM = 64
DIL_CONFIGS = ((128, 1), (512, 4), (2048, 16))
N_DIL_GROUPS = 3
GROUP_WIDTH = 256
N_FOX_HEADS = 8
FOX_WIDTH = 512
BLOCK = 128
ROPE_THETA = 500000.0
ROPE_DIM = 16
D_FF = 2816
EPS = 1e-6
NEG_INF = -1e30

DIL_WIDTH = N_DIL_GROUPS * GROUP_WIDTH
QKV_COLS = 3 * DIL_WIDTH + 3 * FOX_WIDTH
GATE_COL0 = QKV_COLS + N_FOX_HEADS
F_ROWS = 16
LANES = 128
SUBLANES = 8
MXU_COLS = 256
VMEM_LIMIT = 56 * 1024 * 1024
VT_ROWS = HEAD_DIM + 16
LOG2E = 1.4426950408889634

F32 = jnp.float32
BF16 = jnp.bfloat16
_NT = (((1,), (1,)), ((), ()))


def _rmsnorm(x, gain):
    return x * lax.rsqrt(jnp.mean(x * x, axis=-1, keepdims=True) + EPS) * gain


def _log_sigmoid(z):
    return jnp.minimum(z, 0.0) - jnp.log1p(jnp.exp(-jnp.abs(z)))


def _stage_groups(d):
    return max(d // SUBLANES, 1)


def _stage_shape(d, tm):
    g = _stage_groups(d)
    return (GROUP_WIDTH // LANES, g, tm // g, LANES)


def _stage_write_tokens(st, c, slab, d):
    g = _stage_groups(d)
    tm = slab.shape[0]
    dealt = slab.reshape(tm // (SUBLANES * g), g, SUBLANES, LANES)
    for j in range(g):
        st[c, j] = dealt[:, j].reshape(tm // g, LANES)


def _stage_read_tokens(st, c, d):
    g = _stage_groups(d)
    rows = st.shape[2]
    parts = [st[c, j].reshape(rows // SUBLANES, SUBLANES, LANES) for j in range(g)]
    return jnp.stack(parts, axis=1).reshape(rows * g, LANES)


def _stage_residue_index(c, r, d, tm):
    stride = d // _stage_groups(d)
    return (c, r // stride, pl.ds(r % stride, tm // d, stride=stride), slice(None))


def _prep_w_in_kernel(w_ref, scale_ref, qkv_ref, gates_ref, wf_ref):
    w = w_ref[0]
    qkv_ref[...] = (w[:, :QKV_COLS] * scale_ref[...]).astype(BF16)
    gates_ref[...] = w[:, GATE_COL0:].astype(BF16)
    lane = lax.broadcasted_iota(jnp.int32, (w.shape[0], LANES), 1)
    wf_ref[...] = jnp.where(lane < N_FOX_HEADS, w[:, QKV_COLS:QKV_COLS + LANES], 0.0).astype(BF16)


def _prep_w_in(w_in_layer, col_scale, rows=128):
    _, D, C = w_in_layer.shape
    return pl.pallas_call(
        _prep_w_in_kernel,
        grid=(D // rows,),
        in_specs=[pl.BlockSpec((1, rows, C), lambda i: (0, i, 0)),
                  pl.BlockSpec((1, QKV_COLS), lambda i: (0, 0))],
        out_specs=[pl.BlockSpec((rows, QKV_COLS), lambda i: (i, 0)),
                   pl.BlockSpec((rows, 2 * D), lambda i: (i, 0)),
                   pl.BlockSpec((rows, LANES), lambda i: (i, 0))],
        out_shape=[jax.ShapeDtypeStruct((D, QKV_COLS), BF16),
                   jax.ShapeDtypeStruct((D, 2 * D), BF16),
                   jax.ShapeDtypeStruct((D, LANES), BF16)],
        compiler_params=pltpu.CompilerParams(dimension_semantics=("arbitrary",)),
        name="prep_w_in",
    )(w_in_layer, col_scale)


def _in_proj_kernel(x_ref, gain_ref, w_ref, wf_ref, bf_ref, cos_ref, sa_ref, sb_ref,
                    qa0, qa1, qa2, ka0, ka1, ka2, va0, va1, va2,
                    qb_ref, kb_ref, vb_ref, ft_ref, carry_ref, *stage_refs, tm):
    i = pl.program_id(1)
    h = _rmsnorm(x_ref[0], gain_ref[...]).astype(BF16)

    z = jnp.dot(h, wf_ref[...], preferred_element_type=F32).T[:F_ROWS] + bf_ref[...]
    logf = _log_sigmoid(z)
    lane = lax.broadcasted_iota(jnp.int32, (F_ROWS, LANES), 1)
    sums = []
    for j in range(tm // LANES):
        c = logf[:, j * LANES:(j + 1) * LANES]
        k = 1
        while k < LANES:
            c = c + jnp.where(lane >= k, pltpu.roll(c, k, 1), 0.0)
            k *= 2
        sums.append(c)
    carry = jnp.where(i == 0, 0.0, carry_ref[...])
    for j, c in enumerate(sums):
        c = c + carry
        ft_ref[0, :, j * LANES:(j + 1) * LANES] = c
        carry = jnp.broadcast_to(c[:, LANES - 1:LANES], (F_ROWS, LANES))
    carry_ref[...] = carry

    def proj(c0, width=MXU_COLS):
        return jnp.dot(h, w_ref[:, c0:c0 + width], preferred_element_type=F32)

    cos, sa, sb = cos_ref[...], sa_ref[...], sb_ref[...]

    def rope(y):
        halves = []
        for c in range(MXU_COLS // LANES):
            yc = y[:, c * LANES:(c + 1) * LANES]
            halves.append(yc * cos + pltpu.roll(yc, LANES - ROPE_DIM // 2, 1) * sa
                          + pltpu.roll(yc, ROPE_DIM // 2, 1) * sb)
        return jnp.concatenate(halves, axis=1)

    stage = iter(stage_refs)

    def put(o_ref, y, d):
        if d == 1:
            o_ref[0, 0] = y.astype(BF16)
            return
        st = next(stage)
        for c in range(GROUP_WIDTH // LANES):
            _stage_write_tokens(st, c, y[:, c * LANES:(c + 1) * LANES], d)
        for r in range(d):
            for c in range(GROUP_WIDTH // LANES):
                o_ref[0, r, :, c * LANES:(c + 1) * LANES] = st[
                    _stage_residue_index(c, r, d, tm)].astype(BF16)

    for g, (q_o, k_o, v_o) in enumerate(((qa0, ka0, va0), (qa1, ka1, va1), (qa2, ka2, va2))):
        d = DIL_CONFIGS[g][1]
        put(q_o, rope(proj(g * GROUP_WIDTH)), d)
        put(k_o, rope(proj(DIL_WIDTH + g * GROUP_WIDTH)), d)
        put(v_o, proj(2 * DIL_WIDTH + g * GROUP_WIDTH), d)
    base = 3 * DIL_WIDTH
    for s, o_ref in enumerate((qb_ref, kb_ref, vb_ref)):
        for c in range(FOX_WIDTH // MXU_COLS):
            o_ref[0, :, c * MXU_COLS:(c + 1) * MXU_COLS] = proj(
                base + s * FOX_WIDTH + c * MXU_COLS).astype(BF16)


def _rope_tables(seq):
    half = ROPE_DIM // 2
    inv_freq = np.power(ROPE_THETA, -np.arange(0, ROPE_DIM, 2, dtype=np.float64) / ROPE_DIM)
    ang = np.arange(seq, dtype=np.float64)[:, None] * inv_freq[None, :]
    cos = np.ones((seq, HEAD_DIM)); sa = np.zeros((seq, HEAD_DIM)); sb = np.zeros((seq, HEAD_DIM))
    cos[:, :half] = np.cos(ang); cos[:, half:ROPE_DIM] = np.cos(ang)
    sa[:, :half] = -np.sin(ang)
    sb[:, half:ROPE_DIM] = np.sin(ang)
    rep = LANES // HEAD_DIM
    return tuple(jnp.asarray(np.tile(t, (1, rep)), dtype=F32) for t in (cos, sa, sb))


def _in_proj(x, gain, w_main, wf, bf, tm=1024):
    B, S, D = x.shape
    cos, sa, sb = _rope_tables(S)
    const = lambda shape: pl.BlockSpec(shape, lambda b, i: (0,) * len(shape),
                                       pipeline_mode=pl.Buffered(1))
    row = lambda w: pl.BlockSpec((1, tm, w), lambda b, i: (b, i, 0))
    tab = pl.BlockSpec((tm, LANES), lambda b, i: (i, 0))
    dils = [d for _, d in DIL_CONFIGS]
    res_shape = [jax.ShapeDtypeStruct((B, d, S // d, GROUP_WIDTH), BF16) for d in dils] * 3
    res_spec = [pl.BlockSpec((1, d, tm // d, GROUP_WIDTH), lambda b, i: (b, 0, i, 0))
                for d in dils] * 3
    out_shape = (res_shape
                 + [jax.ShapeDtypeStruct((B, S, FOX_WIDTH), BF16)] * 3
                 + [jax.ShapeDtypeStruct((B, F_ROWS, S), F32)])
    out_specs = (res_spec + [row(FOX_WIDTH)] * 3
                 + [pl.BlockSpec((1, F_ROWS, tm), lambda b, i: (b, 0, i))])
    stages = [pltpu.VMEM(_stage_shape(d, tm), F32) for d in dils if d > 1 for _ in range(3)]
    return pl.pallas_call(
        functools.partial(_in_proj_kernel, tm=tm),
        grid=(B, S // tm),
        in_specs=[row(D), const((1, D)), const((D, QKV_COLS)), const((D, LANES)),
                  const((F_ROWS, 1)), tab, tab, tab],
        out_specs=out_specs,
        out_shape=out_shape,
        scratch_shapes=[pltpu.VMEM((F_ROWS, LANES), F32)] + stages,
        compiler_params=pltpu.CompilerParams(
            dimension_semantics=("arbitrary", "arbitrary"), vmem_limit_bytes=VMEM_LIMIT),
        name="in_proj",
    )(x, gain, w_main, wf, bf, cos, sa, sb)


def _dilated_kernel(q_ref, k_ref, v_ref, o_ref, lse_ref, vt_ref, *, d, nb, lookahead):
    lane = lax.broadcasted_iota(jnp.int32, (BLOCK, LANES), 1)
    lo_half = lane < HEAD_DIM
    kw = 2 * BLOCK
    n_pairs = GROUP_WIDTH // LANES

    kj = lax.broadcasted_iota(jnp.int32, (kw, kw), 0)
    qi = lax.broadcasted_iota(jnp.int32, (kw, kw), 1) % BLOCK
    kj1 = lax.broadcasted_iota(jnp.int32, (BLOCK, kw), 0)
    qi1 = lax.broadcasted_iota(jnp.int32, (BLOCK, kw), 1) % BLOCK
    bias_lead = jnp.where(kj1 <= qi1, 0.0, NEG_INF)
    bias_band = jnp.where((kj >= qi) & (kj <= qi + BLOCK), 0.0, NEG_INF)

    for r in range(d):
        for hp in range(n_pairs):
            cols = slice(hp * LANES, (hp + 1) * LANES)
            for n in range(nb):
                rows = slice(n * BLOCK, (n + 1) * BLOCK)
                vt = v_ref[0, r, rows, cols].astype(F32).T.astype(BF16)
                vt_ref[r, hp, 0:HEAD_DIM, rows] = vt[:HEAD_DIM]
                vt_ref[r, hp, VT_ROWS:VT_ROWS + HEAD_DIM, rows] = vt[HEAD_DIM:]
            ones = jnp.ones((VT_ROWS - HEAD_DIM, nb * BLOCK), BF16)
            vt_ref[r, hp, HEAD_DIM:VT_ROWS, :] = ones
            vt_ref[r, hp, VT_ROWS + HEAD_DIM:, :] = ones

    def scores(r, n, hp):
        cols = slice(hp * LANES, (hp + 1) * LANES)
        q = q_ref[0, r, n * BLOCK:(n + 1) * BLOCK, cols]
        zero = jnp.zeros_like(q)
        qm = jnp.concatenate([jnp.where(lo_half, q, zero), jnp.where(lo_half, zero, q)], axis=0)
        keys = slice(0, BLOCK) if n == 0 else slice((n - 1) * BLOCK, (n + 1) * BLOCK)
        st = lax.dot_general(k_ref[0, r, keys, cols], qm, _NT,
                             preferred_element_type=F32)
        return st + (bias_lead if n == 0 else bias_band)

    def finish(r, n, hp, st):
        cols = slice(hp * LANES, (hp + 1) * LANES)
        rows = slice(n * BLOCK, (n + 1) * BLOCK)
        keys = slice(0, BLOCK) if n == 0 else slice((n - 1) * BLOCK, (n + 1) * BLOCK)
        vt = vt_ref[r, hp, :, keys]
        m = jnp.max(st, axis=0, keepdims=True)
        pe = jnp.exp2(st - m).astype(BF16)
        acc = jnp.dot(vt, pe, preferred_element_type=F32)
        out_t, lse_t = [], []
        for hh in range(2):
            a = acc[hh * VT_ROWS:(hh + 1) * VT_ROWS, hh * BLOCK:(hh + 1) * BLOCK]
            den = a[HEAD_DIM:HEAD_DIM + 1]
            out_t.append(a[:HEAD_DIM] * (1.0 / den))
            lse = m[:, hh * BLOCK:(hh + 1) * BLOCK] + jnp.log2(den)
            lse_t.append(jnp.broadcast_to(lse, (HEAD_DIM, BLOCK)))
        o_ref[0, r, rows, cols] = jnp.concatenate(out_t, axis=0).T.astype(BF16)
        lse_ref[0, r, rows, cols] = jnp.concatenate(lse_t, axis=0).T

    units = [(r, n, hp) for r in range(d) for n in range(nb) for hp in range(n_pairs)]
    ready = {}
    for idx in range(len(units) + lookahead):
        if idx < len(units):
            ready[idx] = scores(*units[idx])
        if idx >= lookahead:
            finish(*units[idx - lookahead], ready.pop(idx - lookahead))


def _dilated_group(q, k, v, lookahead=4):
    B, d, L, W = q.shape
    spec = pl.BlockSpec((1, d, L, W), lambda b: (b, 0, 0, 0))
    return pl.pallas_call(
        functools.partial(_dilated_kernel, d=d, nb=L // BLOCK, lookahead=lookahead),
        grid=(B,),
        in_specs=[spec, spec, spec],
        out_specs=[spec, spec],
        out_shape=[jax.ShapeDtypeStruct(q.shape, BF16), jax.ShapeDtypeStruct(q.shape, F32)],
        scratch_shapes=[pltpu.VMEM((d, W // LANES, 2 * VT_ROWS, L), BF16)],
        compiler_params=pltpu.CompilerParams(
            dimension_semantics=("arbitrary",), vmem_limit_bytes=VMEM_LIMIT),
        name=f"dilated_d{d}",
    )(q, k, v)


def _fox_kernel(q_ref, k_ref, v_ref, ft_ref, o_ref, vt_ref, fcol_ref, *, tq, nblk, lookahead):
    p = pl.program_id(1)
    seq = nblk * tq
    lane = lax.broadcasted_iota(jnp.int32, (tq, LANES), 1)
    lo_half = lane < HEAD_DIM
    key = lax.broadcasted_iota(jnp.int32, (tq, tq), 0)
    qry = lax.broadcasted_iota(jnp.int32, (tq, tq), 1)
    causal = key <= qry

    for c in range(seq // tq):
        cs = slice(c * tq, (c + 1) * tq)
        vt = v_ref[0, cs, :].astype(F32).T.astype(BF16)
        for hh in range(2):
            vt_ref[hh, :HEAD_DIM, cs] = vt[hh * HEAD_DIM:(hh + 1) * HEAD_DIM]
    for hh in range(2):
        vt_ref[hh, HEAD_DIM:, :] = jnp.ones((VT_ROWS - HEAD_DIM, seq), BF16)
    frows = [ft_ref[0, pl.ds(2 * p + hh, 1), :] * LOG2E for hh in range(2)]
    for hh in range(2):
        for c in range(seq // LANES):
            cs = slice(c * LANES, (c + 1) * LANES)
            fcol_ref[hh, cs, :] = jnp.broadcast_to(frows[hh][:, cs], (LANES, LANES)).T

    def scores(t, i, hh):
        j = i - t
        ks = slice(j * tq, (j + 1) * tq)
        q = q_ref[0, i * tq:(i + 1) * tq, :]
        qm = jnp.where(lo_half if hh == 0 else ~lo_half, q, jnp.zeros_like(q))
        c0 = frows[hh][:, i * tq:i * tq + 1]
        bias = c0 - fcol_ref[hh, ks, :]
        st = lax.dot_general(k_ref[0, ks, :], qm, _NT, preferred_element_type=F32)
        st = st + jnp.concatenate([bias] * (tq // LANES), axis=1)
        return jnp.where(causal, st, NEG_INF) if t == 0 else st

    state = {}

    def update(t, i, hh, st):
        j = i - t
        vt = vt_ref[hh, :, j * tq:(j + 1) * tq]
        if t == 0:
            m = jnp.max(st, axis=0, keepdims=True)
            pe = jnp.exp2(st - m)
            acc = jnp.dot(vt, pe.astype(BF16), preferred_element_type=F32)
        else:
            m_old, acc_old = state[i, hh]
            m = jnp.maximum(m_old, jnp.max(st, axis=0, keepdims=True))
            pe = jnp.exp2(st - m)
            acc = jnp.exp2(m_old - m) * acc_old + jnp.dot(vt, pe.astype(BF16),
                                                          preferred_element_type=F32)
        state[i, hh] = (m, acc)
        if j == 0 and hh == 1:
            out_t = jnp.concatenate(
                [state[i, h][1][:HEAD_DIM] * (1.0 / state[i, h][1][HEAD_DIM:HEAD_DIM + 1])
                 for h in range(2)], axis=0)
            o_ref[0, i * tq:(i + 1) * tq, :] = out_t.T.astype(BF16)

    units = [(t, i, hh) for t in range(nblk) for i in range(t, nblk) for hh in range(2)]
    ready = {}
    for n in range(len(units) + lookahead):
        if n < len(units):
            ready[n] = scores(*units[n])
        if n >= lookahead:
            update(*units[n - lookahead], ready.pop(n - lookahead))


def _fox(q, k, v, ft, tq=256, lookahead=4):
    B, S, W = q.shape
    spec = pl.BlockSpec((1, S, LANES), lambda b, p: (b, 0, p))
    return pl.pallas_call(
        functools.partial(_fox_kernel, tq=tq, nblk=S // tq, lookahead=lookahead),
        grid=(B, W // LANES),
        in_specs=[spec, spec, spec,
                  pl.BlockSpec((1, F_ROWS, S), lambda b, p: (b, 0, 0))],
        out_specs=spec,
        out_shape=jax.ShapeDtypeStruct((B, S, W), BF16),
        scratch_shapes=[pltpu.VMEM((2, VT_ROWS, S), BF16), pltpu.VMEM((2, S, LANES), F32)],
        compiler_params=pltpu.CompilerParams(
            dimension_semantics=("arbitrary", "arbitrary"), vmem_limit_bytes=VMEM_LIMIT),
        name="fox",
    )(q, k, v, ft)


def _mix_kernel(x_ref, o0, o1, o2, l0, l1, l2, ob_ref, wg_ref, wa_ref, wb_ref, wo_ref,
                gpre_ref, gpost_ref, y_ref, *stage_refs, tm):
    stage = iter(stage_refs)
    x = x_ref[0]
    h = _rmsnorm(x, gpre_ref[...]).astype(BF16)

    def tokens(ref):
        d = ref.shape[1]
        if d == 1:
            return ref[0, 0].astype(F32)
        st = next(stage)
        for r in range(d):
            for c in range(GROUP_WIDTH // LANES):
                st[_stage_residue_index(c, r, d, tm)] = ref[
                    0, r, :, c * LANES:(c + 1) * LANES].astype(F32)
        return jnp.concatenate(
            [_stage_read_tokens(st, c, d) for c in range(GROUP_WIDTH // LANES)], axis=1)

    la, lb, lc = tokens(l0), tokens(l1), tokens(l2)
    mx = jnp.maximum(jnp.maximum(la, lb), lc)
    e0, e1, e2 = jnp.exp2(la - mx), jnp.exp2(lb - mx), jnp.exp2(lc - mx)
    inv = 1.0 / (e0 + e1 + e2)
    oa = ((e0 * tokens(o0) + e1 * tokens(o1) + e2 * tokens(o2)) * inv).astype(BF16)
    ob = ob_ref[0]
    merged = []
    for c in range(D_MODEL // MXU_COLS):
        cols = slice(c * MXU_COLS, (c + 1) * MXU_COLS)
        ga = jnp.dot(h, wg_ref[:, cols], preferred_element_type=F32)
        gb = jnp.dot(h, wg_ref[:, D_MODEL + c * MXU_COLS:D_MODEL + (c + 1) * MXU_COLS],
                     preferred_element_type=F32)
        a = jnp.dot(oa, wa_ref[:, cols], preferred_element_type=F32)
        b = jnp.dot(ob, wb_ref[:, cols], preferred_element_type=F32)
        merged.append((jax.nn.sigmoid(ga) * a + jax.nn.sigmoid(gb) * b).astype(BF16))
    mix = jnp.dot(jnp.concatenate(merged, axis=1), wo_ref[...], preferred_element_type=F32)
    y_ref[0] = x + _rmsnorm(mix, gpost_ref[...])


def _mix(x, outs, lses, ob, wg, wa, wb, wo, gain_pre, gain_post, tm=1024):
    B, S, D = x.shape
    row = lambda w: pl.BlockSpec((1, tm, w), lambda b, i: (b, i, 0))
    const = lambda shape: pl.BlockSpec(shape, lambda b, i: (0, 0), pipeline_mode=pl.Buffered(1))
    res = lambda t: pl.BlockSpec((1, t.shape[1], tm // t.shape[1], GROUP_WIDTH),
                                 lambda b, i: (b, 0, i, 0))
    stages = [pltpu.VMEM(_stage_shape(t.shape[1], tm), F32)
              for group in (lses, outs) for t in group if t.shape[1] > 1]
    return pl.pallas_call(
        functools.partial(_mix_kernel, tm=tm),
        grid=(B, S // tm),
        in_specs=([row(D)] + [res(t) for t in outs] + [res(t) for t in lses]
                  + [row(FOX_WIDTH), const((D, 2 * D)),
                     const((GROUP_WIDTH, D)), const((FOX_WIDTH, D)), const((D, D)),
                     const((1, D)), const((1, D))]),
        out_specs=row(D),
        out_shape=jax.ShapeDtypeStruct((B, S, D), F32),
        scratch_shapes=stages,
        compiler_params=pltpu.CompilerParams(
            dimension_semantics=("arbitrary", "arbitrary"), vmem_limit_bytes=VMEM_LIMIT),
        name="mix",
    )(x, *outs, *lses, ob, wg, wa, wb, wo, gain_pre, gain_post)


def _ffn_kernel(x_ref, g1_ref, g2_ref, wg_ref, wu_ref, wd_ref, o_ref):
    x = x_ref[...]
    h = _rmsnorm(x, g1_ref[...]).astype(BF16)
    acc = jnp.zeros(x.shape, F32)
    for c in range(D_FF // MXU_COLS):
        cols = slice(c * MXU_COLS, (c + 1) * MXU_COLS)
        g = jnp.dot(h, wg_ref[:, cols], preferred_element_type=F32)
        u = jnp.dot(h, wu_ref[:, cols], preferred_element_type=F32)
        a = (g * jax.nn.sigmoid(g) * u).astype(BF16)
        acc = acc + jnp.dot(a, wd_ref[cols, :], preferred_element_type=F32)
    o_ref[...] = x + _rmsnorm(acc, g2_ref[...])


def _ffn(x2, g1, g2, wg, wu, wd, tm=1024):
    T, D = x2.shape
    row = pl.BlockSpec((tm, D), lambda i: (i, 0))
    const = lambda shape: pl.BlockSpec(shape, lambda i: (0, 0), pipeline_mode=pl.Buffered(1))
    return pl.pallas_call(
        _ffn_kernel,
        grid=(T // tm,),
        in_specs=[row, const((1, D)), const((1, D)), const((D, D_FF)), const((D, D_FF)),
                  const((D_FF, D))],
        out_specs=row,
        out_shape=jax.ShapeDtypeStruct((T, D), F32),
        compiler_params=pltpu.CompilerParams(
            dimension_semantics=("arbitrary",), vmem_limit_bytes=VMEM_LIMIT),
        name="ffn",
    )(x2, g1, g2, wg, wu, wd)


def kernel(x, w_in, w_proj_a, w_proj_b, w_out, b_forget, w_ffn_gate, w_ffn_up, w_ffn_down,
           norm_mix_pre, norm_mix_post, norm_ffn_pre, norm_ffn_post):
    B, S, D = x.shape
    scale = 1.0 / np.sqrt(HEAD_DIM)
    col_scale = np.ones((1, QKV_COLS), np.float32)
    col_scale[:, :DIL_WIDTH] = scale * LOG2E
    col_scale[:, 3 * DIL_WIDTH:3 * DIL_WIDTH + FOX_WIDTH] = scale * LOG2E
    col_scale = jnp.asarray(col_scale)
    for layer in range(w_in.shape[0]):
        w_qkv, w_gates, wf = _prep_w_in(w_in[layer:layer + 1], col_scale)
        bf = jnp.zeros((F_ROWS, 1), F32).at[:N_FOX_HEADS, 0].set(b_forget[layer])

        (qa0, qa1, qa2, ka0, ka1, ka2, va0, va1, va2, qb, kb, vb, ft) = _in_proj(
            x, norm_mix_pre[layer][None, :], w_qkv, wf, bf)

        outs, lses = [], []
        for q, k, v in ((qa0, ka0, va0), (qa1, ka1, va1), (qa2, ka2, va2)):
            o_g, l_g = _dilated_group(q, k, v)
            outs.append(o_g)
            lses.append(l_g)
        ob = _fox(qb, kb, vb, ft)

        x = _mix(x, outs, lses, ob, w_gates,
                 w_proj_a[layer].astype(BF16), w_proj_b[layer].astype(BF16),
                 w_out[layer].astype(BF16), norm_mix_pre[layer][None, :],
                 norm_mix_post[layer][None, :])
        x = _ffn(x.reshape(B * S, D), norm_ffn_pre[layer][None, :], norm_ffn_post[layer][None, :],
                 w_ffn_gate[layer].astype(BF16), w_ffn_up[layer].astype(BF16),
                 w_ffn_down[layer].astype(BF16)).reshape(B, S, D)
    return x
```

```python
import functools

import numpy as np
import jax
import jax.numpy as jnp
from jax import lax
from jax.experimental import pallas as pl
from jax.experimental.pallas import tpu as pltpu

D_MODEL = 1024
HEAD_DIM = 64
DIL_CONFIGS = ((128, 1), (512, 4), (2048, 16))
N_DIL_GROUPS = 3
GROUP_WIDTH = 256
N_FOX_HEADS = 8
FOX_WIDTH = 512
BLOCK = 128
ROPE_THETA = 500000.0
ROPE_DIM = 16
D_FF = 2816
EPS = 1e-6
NEG_INF = -1e30

DIL_WIDTH = N_DIL_GROUPS * GROUP_WIDTH
QKV_COLS = 3 * DIL_WIDTH + 3 * FOX_WIDTH
GATE_COL0 = QKV_COLS + N_FOX_HEADS
F_ROWS = 16
LANES = 128
SUBLANES = 8
MXU_COLS = 256
VMEM_LIMIT = 56 * 1024 * 1024
VT_ROWS = HEAD_DIM + 16
LOG2E = 1.4426950408889634

F32 = jnp.float32
BF16 = jnp.bfloat16
_NT = (((1,), (1,)), ((), ()))


def _rmsnorm(x, gain):
    return x * lax.rsqrt(jnp.mean(x * x, axis=-1, keepdims=True) + EPS) * gain


def _log_sigmoid(z):
    return jnp.minimum(z, 0.0) - jnp.log1p(jnp.exp(-jnp.abs(z)))


def _stage_groups(d):
    return max(d // SUBLANES, 1)


def _stage_shape(d, tm):
    g = _stage_groups(d)
    return (GROUP_WIDTH // LANES, g, tm // g, LANES)


def _stage_write_tokens(st, c, slab, d):
    g = _stage_groups(d)
    tm = slab.shape[0]
    dealt = slab.reshape(tm // (SUBLANES * g), g, SUBLANES, LANES)
    for j in range(g):
        st[c, j] = dealt[:, j].reshape(tm // g, LANES)


def _stage_read_tokens(st, c, d):
    g = _stage_groups(d)
    rows = st.shape[2]
    parts = [st[c, j].reshape(rows // SUBLANES, SUBLANES, LANES) for j in range(g)]
    return jnp.stack(parts, axis=1).reshape(rows * g, LANES)


def _stage_residue_index(c, r, d, tm):
    stride = d // _stage_groups(d)
    return (c, r // stride, pl.ds(r % stride, tm // d, stride=stride), slice(None))


def _prep_w_in_kernel(wt_ref, qkv_ref, gates_ref, wf_ref, *, q_scale):
    def block(row0):
        return wt_ref[row0:row0 + LANES, :].T

    for c in range(QKV_COLS // LANES):
        is_q = c * LANES < DIL_WIDTH or 3 * DIL_WIDTH <= c * LANES < 3 * DIL_WIDTH + FOX_WIDTH
        blk = block(c * LANES)
        qkv_ref[:, c * LANES:(c + 1) * LANES] = (blk * q_scale if is_q else blk).astype(BF16)
    for c in range(2 * D_MODEL // LANES):
        gates_ref[:, c * LANES:(c + 1) * LANES] = block(GATE_COL0 + c * LANES).astype(BF16)
    lane = lax.broadcasted_iota(jnp.int32, (LANES, LANES), 1)
    wf_ref[...] = jnp.where(lane < N_FOX_HEADS, block(QKV_COLS), 0.0).astype(BF16)


def _prep_w_in(w_in_t, q_scale):
    C, D = w_in_t.shape
    return pl.pallas_call(
        functools.partial(_prep_w_in_kernel, q_scale=q_scale),
        grid=(D // LANES,),
        in_specs=[pl.BlockSpec((C, LANES), lambda i: (0, i))],
        out_specs=[pl.BlockSpec((LANES, QKV_COLS), lambda i: (i, 0)),
                   pl.BlockSpec((LANES, 2 * D), lambda i: (i, 0)),
                   pl.BlockSpec((LANES, LANES), lambda i: (i, 0))],
        out_shape=[jax.ShapeDtypeStruct((D, QKV_COLS), BF16),
                   jax.ShapeDtypeStruct((D, 2 * D), BF16),
                   jax.ShapeDtypeStruct((D, LANES), BF16)],
        compiler_params=pltpu.CompilerParams(dimension_semantics=("arbitrary",)),
        name="prep_w_in",
    )(w_in_t)


def _in_proj_kernel(x_ref, gain_ref, w_ref, wf_ref, bf_ref, cos_ref, sa_ref, sb_ref,
                    qa0, qa1, qa2, ka0, ka1, ka2, va0, va1, va2,
                    qb_ref, kb_ref, vb_ref, ft_ref, carry_ref, *stage_refs, tm):
    i = pl.program_id(1)
    h = _rmsnorm(x_ref[0], gain_ref[...]).astype(BF16)

    z = jnp.dot(h, wf_ref[...], preferred_element_type=F32).T[:F_ROWS] + bf_ref[...]
    logf = _log_sigmoid(z)
    lane = lax.broadcasted_iota(jnp.int32, (F_ROWS, LANES), 1)
    sums = []
    for j in range(tm // LANES):
        c = logf[:, j * LANES:(j + 1) * LANES]
        k = 1
        while k < LANES:
            c = c + jnp.where(lane >= k, pltpu.roll(c, k, 1), 0.0)
            k *= 2
        sums.append(c)
    carry = jnp.where(i == 0, 0.0, carry_ref[...])
    for j, c in enumerate(sums):
        c = c + carry
        ft_ref[0, :, j * LANES:(j + 1) * LANES] = c
        carry = jnp.broadcast_to(c[:, LANES - 1:LANES], (F_ROWS, LANES))
    carry_ref[...] = carry

    def proj(c0, width=MXU_COLS):
        return jnp.dot(h, w_ref[:, c0:c0 + width], preferred_element_type=F32)

    cos, sa, sb = cos_ref[...], sa_ref[...], sb_ref[...]

    def rope(y):
        halves = []
        for c in range(MXU_COLS // LANES):
            yc = y[:, c * LANES:(c + 1) * LANES]
            halves.append(yc * cos + pltpu.roll(yc, LANES - ROPE_DIM // 2, 1) * sa
                          + pltpu.roll(yc, ROPE_DIM // 2, 1) * sb)
        return jnp.concatenate(halves, axis=1)

    stage = iter(stage_refs)

    def put(o_ref, y, d):
        if d == 1:
            o_ref[0, 0] = y.astype(BF16)
            return
        st = next(stage)
        for c in range(GROUP_WIDTH // LANES):
            _stage_write_tokens(st, c, y[:, c * LANES:(c + 1) * LANES], d)
        for r in range(d):
            for c in range(GROUP_WIDTH // LANES):
                o_ref[0, r, :, c * LANES:(c + 1) * LANES] = st[
                    _stage_residue_index(c, r, d, tm)].astype(BF16)

    for g, (q_o, k_o, v_o) in enumerate(((qa0, ka0, va0), (qa1, ka1, va1), (qa2, ka2, va2))):
        d = DIL_CONFIGS[g][1]
        put(q_o, rope(proj(g * GROUP_WIDTH)), d)
        put(k_o, rope(proj(DIL_WIDTH + g * GROUP_WIDTH)), d)
        put(v_o, proj(2 * DIL_WIDTH + g * GROUP_WIDTH), d)
    base = 3 * DIL_WIDTH
    for s, o_ref in enumerate((qb_ref, kb_ref, vb_ref)):
        for c in range(FOX_WIDTH // MXU_COLS):
            o_ref[0, :, c * MXU_COLS:(c + 1) * MXU_COLS] = proj(
                base + s * FOX_WIDTH + c * MXU_COLS).astype(BF16)


def _rope_tables(seq):
    half = ROPE_DIM // 2
    inv_freq = np.power(ROPE_THETA, -np.arange(0, ROPE_DIM, 2, dtype=np.float64) / ROPE_DIM)
    ang = np.arange(seq, dtype=np.float64)[:, None] * inv_freq[None, :]
    cos = np.ones((seq, HEAD_DIM)); sa = np.zeros((seq, HEAD_DIM)); sb = np.zeros((seq, HEAD_DIM))
    cos[:, :half] = np.cos(ang); cos[:, half:ROPE_DIM] = np.cos(ang)
    sa[:, :half] = -np.sin(ang)
    sb[:, half:ROPE_DIM] = np.sin(ang)
    rep = LANES // HEAD_DIM
    return tuple(jnp.asarray(np.tile(t, (1, rep)), dtype=F32) for t in (cos, sa, sb))


def _in_proj(x, gain, w_main, wf, bf, tm=1024):
    B, S, D = x.shape
    cos, sa, sb = _rope_tables(S)
    const = lambda shape: pl.BlockSpec(shape, lambda b, i: (0,) * len(shape),
                                       pipeline_mode=pl.Buffered(1))
    row = lambda w: pl.BlockSpec((1, tm, w), lambda b, i: (b, i, 0))
    tab = pl.BlockSpec((tm, LANES), lambda b, i: (i, 0))
    dils = [d for _, d in DIL_CONFIGS]
    res_shape = [jax.ShapeDtypeStruct((B, d, S // d, GROUP_WIDTH), BF16) for d in dils] * 3
    res_spec = [pl.BlockSpec((1, d, tm // d, GROUP_WIDTH), lambda b, i: (b, 0, i, 0))
                for d in dils] * 3
    out_shape = (res_shape
                 + [jax.ShapeDtypeStruct((B, S, FOX_WIDTH), BF16)] * 3
                 + [jax.ShapeDtypeStruct((B, F_ROWS, S), F32)])
    out_specs = (res_spec + [row(FOX_WIDTH)] * 3
                 + [pl.BlockSpec((1, F_ROWS, tm), lambda b, i: (b, 0, i))])
    stages = [pltpu.VMEM(_stage_shape(d, tm), F32) for d in dils if d > 1 for _ in range(3)]
    return pl.pallas_call(
        functools.partial(_in_proj_kernel, tm=tm),
        grid=(B, S // tm),
        in_specs=[row(D), const((1, D)), const((D, QKV_COLS)), const((D, LANES)),
                  const((F_ROWS, 1)), tab, tab, tab],
        out_specs=out_specs,
        out_shape=out_shape,
        scratch_shapes=[pltpu.VMEM((F_ROWS, LANES), F32)] + stages,
        compiler_params=pltpu.CompilerParams(
            dimension_semantics=("arbitrary", "arbitrary"), vmem_limit_bytes=VMEM_LIMIT),
        name="in_proj",
    )(x, gain, w_main, wf, bf, cos, sa, sb)


def _dilated_kernel(q_ref, k_ref, v_ref, o_ref, lse_ref, vt_ref, *, d, nb, lookahead):
    lane = lax.broadcasted_iota(jnp.int32, (BLOCK, LANES), 1)
    lo_half = lane < HEAD_DIM
    kw = 2 * BLOCK
    n_pairs = GROUP_WIDTH // LANES

    kj = lax.broadcasted_iota(jnp.int32, (kw, kw), 0)
    qi = lax.broadcasted_iota(jnp.int32, (kw, kw), 1) % BLOCK
    kj1 = lax.broadcasted_iota(jnp.int32, (BLOCK, kw), 0)
    qi1 = lax.broadcasted_iota(jnp.int32, (BLOCK, kw), 1) % BLOCK
    bias_lead = jnp.where(kj1 <= qi1, 0.0, NEG_INF)
    bias_band = jnp.where((kj >= qi) & (kj <= qi + BLOCK), 0.0, NEG_INF)

    for r in range(d):
        for hp in range(n_pairs):
            cols = slice(hp * LANES, (hp + 1) * LANES)
            for n in range(nb):
                rows = slice(n * BLOCK, (n + 1) * BLOCK)
                vt = v_ref[0, r, rows, cols].astype(F32).T.astype(BF16)
                vt_ref[r, hp, 0:HEAD_DIM, rows] = vt[:HEAD_DIM]
                vt_ref[r, hp, VT_ROWS:VT_ROWS + HEAD_DIM, rows] = vt[HEAD_DIM:]
            ones = jnp.ones((VT_ROWS - HEAD_DIM, nb * BLOCK), BF16)
            vt_ref[r, hp, HEAD_DIM:VT_ROWS, :] = ones
            vt_ref[r, hp, VT_ROWS + HEAD_DIM:, :] = ones

    def scores(r, n, hp):
        cols = slice(hp * LANES, (hp + 1) * LANES)
        q = q_ref[0, r, n * BLOCK:(n + 1) * BLOCK, cols]
        zero = jnp.zeros_like(q)
        qm = jnp.concatenate([jnp.where(lo_half, q, zero), jnp.where(lo_half, zero, q)], axis=0)
        keys = slice(0, BLOCK) if n == 0 else slice((n - 1) * BLOCK, (n + 1) * BLOCK)
        st = lax.dot_general(k_ref[0, r, keys, cols], qm, _NT,
                             preferred_element_type=F32)
        return st + (bias_lead if n == 0 else bias_band)

    def finish(r, n, hp, st):
        cols = slice(hp * LANES, (hp + 1) * LANES)
        rows = slice(n * BLOCK, (n + 1) * BLOCK)
        keys = slice(0, BLOCK) if n == 0 else slice((n - 1) * BLOCK, (n + 1) * BLOCK)
        vt = vt_ref[r, hp, :, keys]
        m = jnp.max(st, axis=0, keepdims=True)
        pe = jnp.exp2(st - m).astype(BF16)
        acc = jnp.dot(vt, pe, preferred_element_type=F32)
        out_t, lse_t = [], []
        for hh in range(2):
            a = acc[hh * VT_ROWS:(hh + 1) * VT_ROWS, hh * BLOCK:(hh + 1) * BLOCK]
            den = a[HEAD_DIM:HEAD_DIM + 1]
            out_t.append(a[:HEAD_DIM] * (1.0 / den))
            lse = m[:, hh * BLOCK:(hh + 1) * BLOCK] + jnp.log2(den)
            lse_t.append(jnp.broadcast_to(lse, (HEAD_DIM, BLOCK)))
        o_ref[0, r, rows, cols] = jnp.concatenate(out_t, axis=0).T.astype(BF16)
        lse_ref[0, r, rows, cols] = jnp.concatenate(lse_t, axis=0).T

    units = [(r, n, hp) for r in range(d) for n in range(nb) for hp in range(n_pairs)]
    ready = {}
    for idx in range(len(units) + lookahead):
        if idx < len(units):
            ready[idx] = scores(*units[idx])
        if idx >= lookahead:
            finish(*units[idx - lookahead], ready.pop(idx - lookahead))


def _dilated_group(q, k, v, lookahead=4):
    B, d, L, W = q.shape
    spec = pl.BlockSpec((1, d, L, W), lambda b: (b, 0, 0, 0))
    return pl.pallas_call(
        functools.partial(_dilated_kernel, d=d, nb=L // BLOCK, lookahead=lookahead),
        grid=(B,),
        in_specs=[spec, spec, spec],
        out_specs=[spec, spec],
        out_shape=[jax.ShapeDtypeStruct(q.shape, BF16), jax.ShapeDtypeStruct(q.shape, F32)],
        scratch_shapes=[pltpu.VMEM((d, W // LANES, 2 * VT_ROWS, L), BF16)],
        compiler_params=pltpu.CompilerParams(
            dimension_semantics=("arbitrary",), vmem_limit_bytes=VMEM_LIMIT),
        name=f"dilated_d{d}",
    )(q, k, v)


def _fox_kernel(q_ref, k_ref, v_ref, ft_ref, o_ref, vt_ref, nfcol_ref, *, tq, nblk, lookahead):
    p = pl.program_id(1)
    seq = nblk * tq
    lane = lax.broadcasted_iota(jnp.int32, (tq, LANES), 1)
    lo_half = lane < HEAD_DIM
    key = lax.broadcasted_iota(jnp.int32, (tq, tq), 0)
    qry = lax.broadcasted_iota(jnp.int32, (tq, tq), 1)
    causal = key <= qry

    for c in range(seq // tq):
        cs = slice(c * tq, (c + 1) * tq)
        vt = v_ref[0, cs, :].astype(F32).T.astype(BF16)
        for hh in range(2):
            vt_ref[hh, :HEAD_DIM, cs] = vt[hh * HEAD_DIM:(hh + 1) * HEAD_DIM]
    for hh in range(2):
        vt_ref[hh, HEAD_DIM:, :] = jnp.ones((VT_ROWS - HEAD_DIM, seq), BF16)
    for hh in range(2):
        nfrow = ft_ref[0, pl.ds(2 * p + hh, 1), :] * (-LOG2E)
        for c in range(seq // LANES):
            cs = slice(c * LANES, (c + 1) * LANES)
            nfcol_ref[hh, cs, :] = jnp.broadcast_to(nfrow[:, cs], (LANES, LANES)).T

    def scores(t, i, hh):
        j = i - t
        ks = slice(j * tq, (j + 1) * tq)
        q = q_ref[0, i * tq:(i + 1) * tq, :]
        qm = jnp.where(lo_half if hh == 0 else ~lo_half, q, jnp.zeros_like(q))
        st = lax.dot_general(k_ref[0, ks, :], qm, _NT, preferred_element_type=F32)
        st = st + jnp.concatenate([nfcol_ref[hh, ks, :]] * (tq // LANES), axis=1)
        return jnp.where(causal, st, NEG_INF) if t == 0 else st

    state = {}

    def update(t, i, hh, st):
        j = i - t
        vt = vt_ref[hh, :, j * tq:(j + 1) * tq]
        if t == 0:
            m = jnp.max(st, axis=0, keepdims=True)
            pe = jnp.exp2(st - m)
            acc = jnp.dot(vt, pe.astype(BF16), preferred_element_type=F32)
        else:
            m_old, acc_old = state[i, hh]
            m = jnp.maximum(m_old, jnp.max(st, axis=0, keepdims=True))
            pe = jnp.exp2(st - m)
            acc = jnp.exp2(m_old - m) * acc_old + jnp.dot(vt, pe.astype(BF16),
                                                          preferred_element_type=F32)
        state[i, hh] = (m, acc)
        if j == 0 and hh == 1:
            out_t = jnp.concatenate(
                [state[i, h][1][:HEAD_DIM] * (1.0 / state[i, h][1][HEAD_DIM:HEAD_DIM + 1])
                 for h in range(2)], axis=0)
            o_ref[0, i * tq:(i + 1) * tq, :] = out_t.T.astype(BF16)

    units = [(t, i, hh) for t in range(nblk) for i in range(t, nblk) for hh in range(2)]
    ready = {}
    for n in range(len(units) + lookahead):
        if n < len(units):
            ready[n] = scores(*units[n])
        if n >= lookahead:
            update(*units[n - lookahead], ready.pop(n - lookahead))


def _fox(q, k, v, ft, tq=256, lookahead=4):
    B, S, W = q.shape
    spec = pl.BlockSpec((1, S, LANES), lambda b, p: (b, 0, p))
    return pl.pallas_call(
        functools.partial(_fox_kernel, tq=tq, nblk=S // tq, lookahead=lookahead),
        grid=(B, W // LANES),
        in_specs=[spec, spec, spec,
                  pl.BlockSpec((1, F_ROWS, S), lambda b, p: (b, 0, 0))],
        out_specs=spec,
        out_shape=jax.ShapeDtypeStruct((B, S, W), BF16),
        scratch_shapes=[pltpu.VMEM((2, VT_ROWS, S), BF16), pltpu.VMEM((2, S, LANES), F32)],
        compiler_params=pltpu.CompilerParams(
            dimension_semantics=("arbitrary", "arbitrary"), vmem_limit_bytes=VMEM_LIMIT),
        name="fox",
    )(q, k, v, ft)


def _mix_kernel(x_ref, o0, o1, o2, l0, l1, l2, ob_ref, wg_ref, wa_ref, wb_ref, wo_ref,
                gpre_ref, gpost_ref, y_ref, *stage_refs, tm):
    stage = iter(stage_refs)
    x = x_ref[0]
    h = _rmsnorm(x, gpre_ref[...]).astype(BF16)

    def tokens(ref):
        d = ref.shape[1]
        if d == 1:
            return ref[0, 0].astype(F32)
        st = next(stage)
        for r in range(d):
            for c in range(GROUP_WIDTH // LANES):
                st[_stage_residue_index(c, r, d, tm)] = ref[
                    0, r, :, c * LANES:(c + 1) * LANES].astype(F32)
        return jnp.concatenate(
            [_stage_read_tokens(st, c, d) for c in range(GROUP_WIDTH // LANES)], axis=1)

    la, lb, lc = tokens(l0), tokens(l1), tokens(l2)
    mx = jnp.maximum(jnp.maximum(la, lb), lc)
    e0, e1, e2 = jnp.exp2(la - mx), jnp.exp2(lb - mx), jnp.exp2(lc - mx)
    inv = 1.0 / (e0 + e1 + e2)
    oa = ((e0 * tokens(o0) + e1 * tokens(o1) + e2 * tokens(o2)) * inv).astype(BF16)
    ob = ob_ref[0]
    merged = []
    for c in range(D_MODEL // MXU_COLS):
        cols = slice(c * MXU_COLS, (c + 1) * MXU_COLS)
        ga = jnp.dot(h, wg_ref[:, cols], preferred_element_type=F32)
        gb = jnp.dot(h, wg_ref[:, D_MODEL + c * MXU_COLS:D_MODEL + (c + 1) * MXU_COLS],
                     preferred_element_type=F32)
        a = jnp.dot(oa, wa_ref[:, cols], preferred_element_type=F32)
        b = jnp.dot(ob, wb_ref[:, cols], preferred_element_type=F32)
        merged.append((jax.nn.sigmoid(ga) * a + jax.nn.sigmoid(gb) * b).astype(BF16))
    mix = jnp.dot(jnp.concatenate(merged, axis=1), wo_ref[...], preferred_element_type=F32)
    y_ref[0] = x + _rmsnorm(mix, gpost_ref[...])


def _mix(x, outs, lses, ob, wg, wa, wb, wo, gain_pre, gain_post, tm=1024):
    B, S, D = x.shape
    row = lambda w: pl.BlockSpec((1, tm, w), lambda b, i: (b, i, 0))
    const = lambda shape: pl.BlockSpec(shape, lambda b, i: (0, 0), pipeline_mode=pl.Buffered(1))
    res = lambda t: pl.BlockSpec((1, t.shape[1], tm // t.shape[1], GROUP_WIDTH),
                                 lambda b, i: (b, 0, i, 0))
    stages = [pltpu.VMEM(_stage_shape(t.shape[1], tm), F32)
              for group in (lses, outs) for t in group if t.shape[1] > 1]
    return pl.pallas_call(
        functools.partial(_mix_kernel, tm=tm),
        grid=(B, S // tm),
        in_specs=([row(D)] + [res(t) for t in outs] + [res(t) for t in lses]
                  + [row(FOX_WIDTH), const((D, 2 * D)),
                     const((GROUP_WIDTH, D)), const((FOX_WIDTH, D)), const((D, D)),
                     const((1, D)), const((1, D))]),
        out_specs=row(D),
        out_shape=jax.ShapeDtypeStruct((B, S, D), F32),
        scratch_shapes=stages,
        compiler_params=pltpu.CompilerParams(
            dimension_semantics=("arbitrary", "arbitrary"), vmem_limit_bytes=VMEM_LIMIT),
        name="mix",
    )(x, *outs, *lses, ob, wg, wa, wb, wo, gain_pre, gain_post)


def _ffn_kernel(x_ref, g1_ref, g2_ref, wg_ref, wu_ref, wd_ref, o_ref):
    x = x_ref[...]
    h = _rmsnorm(x, g1_ref[...]).astype(BF16)
    acc = jnp.zeros(x.shape, F32)
    for c in range(D_FF // MXU_COLS):
        cols = slice(c * MXU_COLS, (c + 1) * MXU_COLS)
        g = jnp.dot(h, wg_ref[:, cols], preferred_element_type=F32)
        u = jnp.dot(h, wu_ref[:, cols], preferred_element_type=F32)
        a = (g * jax.nn.sigmoid(g) * u).astype(BF16)
        acc = acc + jnp.dot(a, wd_ref[cols, :], preferred_element_type=F32)
    o_ref[...] = x + _rmsnorm(acc, g2_ref[...])


def _ffn(x2, g1, g2, wg, wu, wd, tm=1024):
    T, D = x2.shape
    row = pl.BlockSpec((tm, D), lambda i: (i, 0))
    const = lambda shape: pl.BlockSpec(shape, lambda i: (0, 0), pipeline_mode=pl.Buffered(1))
    return pl.pallas_call(
        _ffn_kernel,
        grid=(T // tm,),
        in_specs=[row, const((1, D)), const((1, D)), const((D, D_FF)), const((D, D_FF)),
                  const((D_FF, D))],
        out_specs=row,
        out_shape=jax.ShapeDtypeStruct((T, D), F32),
        compiler_params=pltpu.CompilerParams(
            dimension_semantics=("arbitrary",), vmem_limit_bytes=VMEM_LIMIT),
        name="ffn",
    )(x2, g1, g2, wg, wu, wd)


def kernel(x, w_in, w_proj_a, w_proj_b, w_out, b_forget, w_ffn_gate, w_ffn_up, w_ffn_down,
           norm_mix_pre, norm_mix_post, norm_ffn_pre, norm_ffn_post):
    B, S, D = x.shape
    q_scale = float(LOG2E / np.sqrt(HEAD_DIM))
    for layer in range(w_in.shape[0]):
        w_qkv, w_gates, wf = _prep_w_in(w_in[layer].T, q_scale)
        bf = jnp.zeros((F_ROWS, 1), F32).at[:N_FOX_HEADS, 0].set(b_forget[layer])

        (qa0, qa1, qa2, ka0, ka1, ka2, va0, va1, va2, qb, kb, vb, ft) = _in_proj(
            x, norm_mix_pre[layer][None, :], w_qkv, wf, bf)

        outs, lses = [], []
        for q, k, v in ((qa0, ka0, va0), (qa1, ka1, va1), (qa2, ka2, va2)):
            o_g, l_g = _dilated_group(q, k, v)
            outs.append(o_g)
            lses.append(l_g)
        ob = _fox(qb, kb, vb, ft)

        x = _mix(x, outs, lses, ob, w_gates,
                 w_proj_a[layer].astype(BF16), w_proj_b[layer].astype(BF16),
                 w_out[layer].astype(BF16), norm_mix_pre[layer][None, :],
                 norm_mix_post[layer][None, :])
        x = _ffn(x.reshape(B * S, D), norm_ffn_pre[layer][None, :], norm_ffn_post[layer][None, :],
                 w_ffn_gate[layer].astype(BF16), w_ffn_up[layer].astype(BF16),
                 w_ffn_down[layer].astype(BF16)).reshape(B, S, D)
    return x
```

```python
import functools

import numpy as np
import jax
import jax.numpy as jnp
from jax import lax
from jax.experimental import pallas as pl
from jax.experimental.pallas import tpu as pltpu

D_MODEL = 1024
HEAD_DIM = 64
DIL_CONFIGS = ((128, 1), (512, 4), (2048, 16))
N_DIL_GROUPS = 3
GROUP_WIDTH = 256
N_FOX_HEADS = 8
FOX_WIDTH = 512
BLOCK = 128
ROPE_THETA = 500000.0
ROPE_DIM = 16
D_FF = 2816
EPS = 1e-6
NEG_INF = -1e30

DIL_WIDTH = N_DIL_GROUPS * GROUP_WIDTH
QKV_COLS = 3 * DIL_WIDTH + 3 * FOX_WIDTH
GATE_COL0 = QKV_COLS + N_FOX_HEADS
F_ROWS = 16
LANES = 128
SUBLANES = 8
MXU_COLS = 256
VMEM_LIMIT = 56 * 1024 * 1024
VT_ROWS = HEAD_DIM + 16
LOG2E = 1.4426950408889634

F32 = jnp.float32
BF16 = jnp.bfloat16
_NT = (((1,), (1,)), ((), ()))


def _rmsnorm(x, gain):
    return x * lax.rsqrt(jnp.mean(x * x, axis=-1, keepdims=True) + EPS) * gain


def _log_sigmoid(z):
    return jnp.minimum(z, 0.0) - jnp.log1p(jnp.exp(-jnp.abs(z)))


def _stage_groups(d):
    return max(d // SUBLANES, 1)


def _stage_shape(d, tm):
    g = _stage_groups(d)
    return (GROUP_WIDTH // LANES, g, tm // g, LANES)


def _stage_write_tokens(st, c, slab, d):
    g = _stage_groups(d)
    tm = slab.shape[0]
    dealt = slab.reshape(tm // (SUBLANES * g), g, SUBLANES, LANES)
    for j in range(g):
        st[c, j] = dealt[:, j].reshape(tm // g, LANES)


def _stage_read_tokens(st, c, d):
    g = _stage_groups(d)
    rows = st.shape[2]
    parts = [st[c, j].reshape(rows // SUBLANES, SUBLANES, LANES) for j in range(g)]
    return jnp.stack(parts, axis=1).reshape(rows * g, LANES)


def _stage_residue_index(c, r, d, tm):
    stride = d // _stage_groups(d)
    return (c, r // stride, pl.ds(r % stride, tm // d, stride=stride), slice(None))


def _prep_w_in_kernel(wt_ref, qkv_ref, gates_ref, wf_ref, *, q_scale):
    def block(row0):
        return wt_ref[row0:row0 + LANES, :].T

    for c in range(QKV_COLS // LANES):
        is_q = c * LANES < DIL_WIDTH or 3 * DIL_WIDTH <= c * LANES < 3 * DIL_WIDTH + FOX_WIDTH
        blk = block(c * LANES)
        qkv_ref[:, c * LANES:(c + 1) * LANES] = (blk * q_scale if is_q else blk).astype(BF16)
    for c in range(2 * D_MODEL // LANES):
        gates_ref[:, c * LANES:(c + 1) * LANES] = block(GATE_COL0 + c * LANES).astype(BF16)
    lane = lax.broadcasted_iota(jnp.int32, (LANES, LANES), 1)
    wf_ref[...] = jnp.where(lane < N_FOX_HEADS, block(QKV_COLS), 0.0).astype(BF16)


def _prep_w_in(w_in_t, q_scale):
    C, D = w_in_t.shape
    return pl.pallas_call(
        functools.partial(_prep_w_in_kernel, q_scale=q_scale),
        grid=(D // LANES,),
        in_specs=[pl.BlockSpec((C, LANES), lambda i: (0, i))],
        out_specs=[pl.BlockSpec((LANES, QKV_COLS), lambda i: (i, 0)),
                   pl.BlockSpec((LANES, 2 * D), lambda i: (i, 0)),
                   pl.BlockSpec((LANES, LANES), lambda i: (i, 0))],
        out_shape=[jax.ShapeDtypeStruct((D, QKV_COLS), BF16),
                   jax.ShapeDtypeStruct((D, 2 * D), BF16),
                   jax.ShapeDtypeStruct((D, LANES), BF16)],
        compiler_params=pltpu.CompilerParams(dimension_semantics=("arbitrary",)),
        name="prep_w_in",
    )(w_in_t)


def _in_proj_kernel(*refs, tm, n_side):
    (x_ref, gain_ref, w_ref, wf_ref, bf_ref, cos_ref, sa_ref, sb_ref), refs = refs[:8], refs[8:]
    side_in, refs = refs[:n_side], refs[n_side:]
    (qkv0, qkv1, qkv2, qkvb_ref, ft_ref), refs = refs[:5], refs[5:]
    side_out, refs = refs[:n_side], refs[n_side:]
    carry_ref, stage_refs = refs[0], refs[1:]
    i = pl.program_id(1)
    for src, dst in zip(side_in, side_out):
        dst[...] = src[...].astype(BF16)
    h = _rmsnorm(x_ref[0], gain_ref[...]).astype(BF16)

    z = jnp.dot(h, wf_ref[...], preferred_element_type=F32).T[:F_ROWS] + bf_ref[...]
    logf = _log_sigmoid(z)
    lane = lax.broadcasted_iota(jnp.int32, (F_ROWS, LANES), 1)
    sums = []
    for j in range(tm // LANES):
        c = logf[:, j * LANES:(j + 1) * LANES]
        k = 1
        while k < LANES:
            c = c + jnp.where(lane >= k, pltpu.roll(c, k, 1), 0.0)
            k *= 2
        sums.append(c)
    carry = jnp.where(i == 0, 0.0, carry_ref[...])
    for j, c in enumerate(sums):
        c = c + carry
        ft_ref[0, :, j * LANES:(j + 1) * LANES] = c
        carry = jnp.broadcast_to(c[:, LANES - 1:LANES], (F_ROWS, LANES))
    carry_ref[...] = carry

    def proj(c0, width=MXU_COLS):
        return jnp.dot(h, w_ref[:, c0:c0 + width], preferred_element_type=F32)

    cos, sa, sb = cos_ref[...], sa_ref[...], sb_ref[...]

    def rope(y):
        halves = []
        for c in range(MXU_COLS // LANES):
            yc = y[:, c * LANES:(c + 1) * LANES]
            halves.append(yc * cos + pltpu.roll(yc, LANES - ROPE_DIM // 2, 1) * sa
                          + pltpu.roll(yc, ROPE_DIM // 2, 1) * sb)
        return jnp.concatenate(halves, axis=1)

    stage = iter(stage_refs)

    def put(o_ref, col0, y, d):
        if d == 1:
            o_ref[0, 0, :, col0:col0 + GROUP_WIDTH] = y.astype(BF16)
            return
        st = next(stage)
        for c in range(GROUP_WIDTH // LANES):
            _stage_write_tokens(st, c, y[:, c * LANES:(c + 1) * LANES], d)
        for r in range(d):
            for c in range(GROUP_WIDTH // LANES):
                o_ref[0, r, :, col0 + c * LANES:col0 + (c + 1) * LANES] = st[
                    _stage_residue_index(c, r, d, tm)].astype(BF16)

    for g, o_ref in enumerate((qkv0, qkv1, qkv2)):
        d = DIL_CONFIGS[g][1]
        put(o_ref, 0, rope(proj(g * GROUP_WIDTH)), d)
        put(o_ref, GROUP_WIDTH, rope(proj(DIL_WIDTH + g * GROUP_WIDTH)), d)
        put(o_ref, 2 * GROUP_WIDTH, proj(2 * DIL_WIDTH + g * GROUP_WIDTH), d)
    base = 3 * DIL_WIDTH
    for c in range(3 * FOX_WIDTH // MXU_COLS):
        qkvb_ref[0, :, c * MXU_COLS:(c + 1) * MXU_COLS] = proj(base + c * MXU_COLS).astype(BF16)


def _rope_tables(seq):
    half = ROPE_DIM // 2
    inv_freq = np.power(ROPE_THETA, -np.arange(0, ROPE_DIM, 2, dtype=np.float64) / ROPE_DIM)
    ang = np.arange(seq, dtype=np.float64)[:, None] * inv_freq[None, :]
    cos = np.ones((seq, HEAD_DIM)); sa = np.zeros((seq, HEAD_DIM)); sb = np.zeros((seq, HEAD_DIM))
    cos[:, :half] = np.cos(ang); cos[:, half:ROPE_DIM] = np.cos(ang)
    sa[:, :half] = -np.sin(ang)
    sb[:, half:ROPE_DIM] = np.sin(ang)
    rep = LANES // HEAD_DIM
    return tuple(jnp.asarray(np.tile(t, (1, rep)), dtype=F32) for t in (cos, sa, sb))


def _in_proj(x, gain, w_main, wf, bf, side_weights, tm=1024):
    B, S, D = x.shape
    n_i = S // tm
    n_steps = B * n_i
    side_specs = [pl.BlockSpec((w.shape[0] // n_steps, w.shape[1]), lambda b, i: (b * n_i + i, 0))
                  for w in side_weights]
    side_shapes = [jax.ShapeDtypeStruct(w.shape, BF16) for w in side_weights]
    cos, sa, sb = _rope_tables(S)
    const = lambda shape: pl.BlockSpec(shape, lambda b, i: (0,) * len(shape),
                                       pipeline_mode=pl.Buffered(1))
    row = lambda w: pl.BlockSpec((1, tm, w), lambda b, i: (b, i, 0))
    tab = pl.BlockSpec((tm, LANES), lambda b, i: (i, 0))
    dils = [d for _, d in DIL_CONFIGS]
    res_shape = [jax.ShapeDtypeStruct((B, d, S // d, 3 * GROUP_WIDTH), BF16) for d in dils]
    res_spec = [pl.BlockSpec((1, d, tm // d, 3 * GROUP_WIDTH), lambda b, i: (b, 0, i, 0))
                for d in dils]
    out_shape = (res_shape
                 + [jax.ShapeDtypeStruct((B, S, 3 * FOX_WIDTH), BF16)]
                 + [jax.ShapeDtypeStruct((B, F_ROWS, S), F32)])
    out_specs = (res_spec + [row(3 * FOX_WIDTH)]
                 + [pl.BlockSpec((1, F_ROWS, tm), lambda b, i: (b, 0, i))])
    stages = [pltpu.VMEM(_stage_shape(d, tm), F32) for d in dils if d > 1 for _ in range(3)]
    return pl.pallas_call(
        functools.partial(_in_proj_kernel, tm=tm, n_side=len(side_weights)),
        grid=(B, n_i),
        in_specs=[row(D), const((1, D)), const((D, QKV_COLS)), const((D, LANES)),
                  const((F_ROWS, 1)), tab, tab, tab] + side_specs,
        out_specs=out_specs + side_specs,
        out_shape=out_shape + side_shapes,
        scratch_shapes=[pltpu.VMEM((F_ROWS, LANES), F32)] + stages,
        compiler_params=pltpu.CompilerParams(
            dimension_semantics=("arbitrary", "arbitrary"), vmem_limit_bytes=VMEM_LIMIT),
        name="in_proj",
    )(x, gain, w_main, wf, bf, cos, sa, sb, *side_weights)


def _dilated_kernel(qkv_ref, o_ref, lse_ref, vt_ref, *, d, nb, lookahead):
    lane = lax.broadcasted_iota(jnp.int32, (BLOCK, LANES), 1)
    lo_half = lane < HEAD_DIM
    kw = 2 * BLOCK
    n_pairs = GROUP_WIDTH // LANES

    kj = lax.broadcasted_iota(jnp.int32, (kw, kw), 0)
    qi = lax.broadcasted_iota(jnp.int32, (kw, kw), 1) % BLOCK
    kj1 = lax.broadcasted_iota(jnp.int32, (BLOCK, kw), 0)
    qi1 = lax.broadcasted_iota(jnp.int32, (BLOCK, kw), 1) % BLOCK
    bias_lead = jnp.where(kj1 <= qi1, 0.0, NEG_INF)
    bias_band = jnp.where((kj >= qi) & (kj <= qi + BLOCK), 0.0, NEG_INF)

    for r in range(d):
        for hp in range(n_pairs):
            cols = slice(hp * LANES, (hp + 1) * LANES)
            for n in range(nb):
                rows = slice(n * BLOCK, (n + 1) * BLOCK)
                vt = qkv_ref[0, r, rows, 2 * GROUP_WIDTH + hp * LANES:
                             2 * GROUP_WIDTH + (hp + 1) * LANES].astype(F32).T.astype(BF16)
                vt_ref[r, hp, 0:HEAD_DIM, rows] = vt[:HEAD_DIM]
                vt_ref[r, hp, VT_ROWS:VT_ROWS + HEAD_DIM, rows] = vt[HEAD_DIM:]
            ones = jnp.ones((VT_ROWS - HEAD_DIM, nb * BLOCK), BF16)
            vt_ref[r, hp, HEAD_DIM:VT_ROWS, :] = ones
            vt_ref[r, hp, VT_ROWS + HEAD_DIM:, :] = ones

    def scores(r, n, hp):
        cols = slice(hp * LANES, (hp + 1) * LANES)
        q = qkv_ref[0, r, n * BLOCK:(n + 1) * BLOCK, cols]
        zero = jnp.zeros_like(q)
        qm = jnp.concatenate([jnp.where(lo_half, q, zero), jnp.where(lo_half, zero, q)], axis=0)
        keys = slice(0, BLOCK) if n == 0 else slice((n - 1) * BLOCK, (n + 1) * BLOCK)
        k_cols = slice(GROUP_WIDTH + hp * LANES, GROUP_WIDTH + (hp + 1) * LANES)
        st = lax.dot_general(qkv_ref[0, r, keys, k_cols], qm, _NT,
                             preferred_element_type=F32)
        return st + (bias_lead if n == 0 else bias_band)

    def finish(r, n, hp, st):
        cols = slice(hp * LANES, (hp + 1) * LANES)
        rows = slice(n * BLOCK, (n + 1) * BLOCK)
        keys = slice(0, BLOCK) if n == 0 else slice((n - 1) * BLOCK, (n + 1) * BLOCK)
        vt = vt_ref[r, hp, :, keys]
        m = jnp.max(st, axis=0, keepdims=True)
        pe = jnp.exp2(st - m).astype(BF16)
        acc = jnp.dot(vt, pe, preferred_element_type=F32)
        out_t, lse_t = [], []
        for hh in range(2):
            a = acc[hh * VT_ROWS:(hh + 1) * VT_ROWS, hh * BLOCK:(hh + 1) * BLOCK]
            den = a[HEAD_DIM:HEAD_DIM + 1]
            out_t.append(a[:HEAD_DIM] * (1.0 / den))
            lse = m[:, hh * BLOCK:(hh + 1) * BLOCK] + jnp.log2(den)
            lse_t.append(jnp.broadcast_to(lse, (HEAD_DIM, BLOCK)))
        o_ref[0, r, rows, cols] = jnp.concatenate(out_t, axis=0).T.astype(BF16)
        lse_ref[0, r, rows, cols] = jnp.concatenate(lse_t, axis=0).T

    units = [(r, n, hp) for r in range(d) for n in range(nb) for hp in range(n_pairs)]
    ready = {}
    for idx in range(len(units) + lookahead):
        if idx < len(units):
            ready[idx] = scores(*units[idx])
        if idx >= lookahead:
            finish(*units[idx - lookahead], ready.pop(idx - lookahead))


def _dilated_group(qkv, lookahead=4):
    B, d, L, _ = qkv.shape
    W = GROUP_WIDTH
    spec = pl.BlockSpec((1, d, L, W), lambda b: (b, 0, 0, 0))
    return pl.pallas_call(
        functools.partial(_dilated_kernel, d=d, nb=L // BLOCK, lookahead=lookahead),
        grid=(B,),
        in_specs=[pl.BlockSpec((1, d, L, 3 * W), lambda b: (b, 0, 0, 0))],
        out_specs=[spec, spec],
        out_shape=[jax.ShapeDtypeStruct((B, d, L, W), BF16),
                   jax.ShapeDtypeStruct((B, d, L, W), F32)],
        scratch_shapes=[pltpu.VMEM((d, W // LANES, 2 * VT_ROWS, L), BF16)],
        compiler_params=pltpu.CompilerParams(
            dimension_semantics=("arbitrary",), vmem_limit_bytes=VMEM_LIMIT),
        name=f"dilated_d{d}",
    )(qkv)


def _fox_kernel(q_ref, k_ref, v_ref, ft_ref, o_ref, vt_ref, nfcol_ref, *, tq, nblk, lookahead):
    p = pl.program_id(1)
    seq = nblk * tq
    lane = lax.broadcasted_iota(jnp.int32, (tq, LANES), 1)
    lo_half = lane < HEAD_DIM
    key = lax.broadcasted_iota(jnp.int32, (tq, tq), 0)
    qry = lax.broadcasted_iota(jnp.int32, (tq, tq), 1)
    causal = key <= qry

    for c in range(seq // tq):
        cs = slice(c * tq, (c + 1) * tq)
        vt = v_ref[0, cs, :].astype(F32).T.astype(BF16)
        for hh in range(2):
            vt_ref[hh, :HEAD_DIM, cs] = vt[hh * HEAD_DIM:(hh + 1) * HEAD_DIM]
    for hh in range(2):
        vt_ref[hh, HEAD_DIM:, :] = jnp.ones((VT_ROWS - HEAD_DIM, seq), BF16)
    for hh in range(2):
        nfrow = ft_ref[0, pl.ds(2 * p + hh, 1), :] * (-LOG2E)
        for c in range(seq // LANES):
            cs = slice(c * LANES, (c + 1) * LANES)
            nfcol_ref[hh, cs, :] = jnp.broadcast_to(nfrow[:, cs], (LANES, LANES)).T

    def scores(t, i, hh):
        j = i - t
        ks = slice(j * tq, (j + 1) * tq)
        q = q_ref[0, i * tq:(i + 1) * tq, :]
        qm = jnp.where(lo_half if hh == 0 else ~lo_half, q, jnp.zeros_like(q))
        st = lax.dot_general(k_ref[0, ks, :], qm, _NT, preferred_element_type=F32)
        st = st + jnp.concatenate([nfcol_ref[hh, ks, :]] * (tq // LANES), axis=1)
        return jnp.where(causal, st, NEG_INF) if t == 0 else st

    state = {}

    def update(t, i, hh, st):
        j = i - t
        vt = vt_ref[hh, :, j * tq:(j + 1) * tq]
        if t == 0:
            m = jnp.max(st, axis=0, keepdims=True)
            pe = jnp.exp2(st - m)
            acc = jnp.dot(vt, pe.astype(BF16), preferred_element_type=F32)
        else:
            m_old, acc_old = state[i, hh]
            m = jnp.maximum(m_old, jnp.max(st, axis=0, keepdims=True))
            pe = jnp.exp2(st - m)
            acc = jnp.exp2(m_old - m) * acc_old + jnp.dot(vt, pe.astype(BF16),
                                                          preferred_element_type=F32)
        state[i, hh] = (m, acc)
        if j == 0 and hh == 1:
            out_t = jnp.concatenate(
                [state[i, h][1][:HEAD_DIM] * (1.0 / state[i, h][1][HEAD_DIM:HEAD_DIM + 1])
                 for h in range(2)], axis=0)
            o_ref[0, i * tq:(i + 1) * tq, :] = out_t.T.astype(BF16)

    units = [(t, i, hh) for t in range(nblk) for i in range(t, nblk) for hh in range(2)]
    ready = {}
    for n in range(len(units) + lookahead):
        if n < len(units):
            ready[n] = scores(*units[n])
        if n >= lookahead:
            update(*units[n - lookahead], ready.pop(n - lookahead))


def _fox(qkv, ft, tq=256, lookahead=4):
    B, S, _ = qkv.shape
    W = FOX_WIDTH
    n_pairs = W // LANES
    spec = pl.BlockSpec((1, S, LANES), lambda b, p: (b, 0, p))
    part = lambda s: pl.BlockSpec((1, S, LANES), lambda b, p: (b, 0, s * n_pairs + p))
    return pl.pallas_call(
        functools.partial(_fox_kernel, tq=tq, nblk=S // tq, lookahead=lookahead),
        grid=(B, n_pairs),
        in_specs=[part(0), part(1), part(2),
                  pl.BlockSpec((1, F_ROWS, S), lambda b, p: (b, 0, 0))],
        out_specs=spec,
        out_shape=jax.ShapeDtypeStruct((B, S, W), BF16),
        scratch_shapes=[pltpu.VMEM((2, VT_ROWS, S), BF16), pltpu.VMEM((2, S, LANES), F32)],
        compiler_params=pltpu.CompilerParams(
            dimension_semantics=("arbitrary", "arbitrary"), vmem_limit_bytes=VMEM_LIMIT),
        name="fox",
    )(qkv, qkv, qkv, ft)


def _mix_kernel(x_ref, o0, o1, o2, l0, l1, l2, ob_ref, wg_ref, wa_ref, wb_ref, wo_ref,
                gpre_ref, gpost_ref, y_ref, *stage_refs, tm):
    stage = iter(stage_refs)
    x = x_ref[0]
    h = _rmsnorm(x, gpre_ref[...]).astype(BF16)

    def tokens(ref):
        d = ref.shape[1]
        if d == 1:
            return ref[0, 0].astype(F32)
        st = next(stage)
        for r in range(d):
            for c in range(GROUP_WIDTH // LANES):
                st[_stage_residue_index(c, r, d, tm)] = ref[
                    0, r, :, c * LANES:(c + 1) * LANES].astype(F32)
        return jnp.concatenate(
            [_stage_read_tokens(st, c, d) for c in range(GROUP_WIDTH // LANES)], axis=1)

    la, lb, lc = tokens(l0), tokens(l1), tokens(l2)
    mx = jnp.maximum(jnp.maximum(la, lb), lc)
    e0, e1, e2 = jnp.exp2(la - mx), jnp.exp2(lb - mx), jnp.exp2(lc - mx)
    inv = 1.0 / (e0 + e1 + e2)
    oa = ((e0 * tokens(o0) + e1 * tokens(o1) + e2 * tokens(o2)) * inv).astype(BF16)
    ob = ob_ref[0]
    merged = []
    for c in range(D_MODEL // MXU_COLS):
        cols = slice(c * MXU_COLS, (c + 1) * MXU_COLS)
        ga = jnp.dot(h, wg_ref[:, cols], preferred_element_type=F32)
        gb = jnp.dot(h, wg_ref[:, D_MODEL + c * MXU_COLS:D_MODEL + (c + 1) * MXU_COLS],
                     preferred_element_type=F32)
        a = jnp.dot(oa, wa_ref[:, cols], preferred_element_type=F32)
        b = jnp.dot(ob, wb_ref[:, cols], preferred_element_type=F32)
        merged.append((jax.nn.sigmoid(ga) * a + jax.nn.sigmoid(gb) * b).astype(BF16))
    mix = jnp.dot(jnp.concatenate(merged, axis=1), wo_ref[...], preferred_element_type=F32)
    y_ref[0] = x + _rmsnorm(mix, gpost_ref[...])


def _mix(x, outs, lses, ob, wg, wa, wb, wo, gain_pre, gain_post, tm=1024):
    B, S, D = x.shape
    row = lambda w: pl.BlockSpec((1, tm, w), lambda b, i: (b, i, 0))
    const = lambda shape: pl.BlockSpec(shape, lambda b, i: (0, 0), pipeline_mode=pl.Buffered(1))
    res = lambda t: pl.BlockSpec((1, t.shape[1], tm // t.shape[1], GROUP_WIDTH),
                                 lambda b, i: (b, 0, i, 0))
    stages = [pltpu.VMEM(_stage_shape(t.shape[1], tm), F32)
              for group in (lses, outs) for t in group if t.shape[1] > 1]
    return pl.pallas_call(
        functools.partial(_mix_kernel, tm=tm),
        grid=(B, S // tm),
        in_specs=([row(D)] + [res(t) for t in outs] + [res(t) for t in lses]
                  + [row(FOX_WIDTH), const((D, 2 * D)),
                     const((GROUP_WIDTH, D)), const((FOX_WIDTH, D)), const((D, D)),
                     const((1, D)), const((1, D))]),
        out_specs=row(D),
        out_shape=jax.ShapeDtypeStruct((B, S, D), F32),
        scratch_shapes=stages,
        compiler_params=pltpu.CompilerParams(
            dimension_semantics=("arbitrary", "arbitrary"), vmem_limit_bytes=VMEM_LIMIT),
        name="mix",
    )(x, *outs, *lses, ob, wg, wa, wb, wo, gain_pre, gain_post)


def _ffn_kernel(x_ref, g1_ref, g2_ref, wg_ref, wu_ref, wd_ref, o_ref):
    x = x_ref[...]
    h = _rmsnorm(x, g1_ref[...]).astype(BF16)
    acc = jnp.zeros(x.shape, F32)
    for c in range(D_FF // MXU_COLS):
        cols = slice(c * MXU_COLS, (c + 1) * MXU_COLS)
        g = jnp.dot(h, wg_ref[:, cols], preferred_element_type=F32)
        u = jnp.dot(h, wu_ref[:, cols], preferred_element_type=F32)
        a = (g * jax.nn.sigmoid(g) * u).astype(BF16)
        acc = acc + jnp.dot(a, wd_ref[cols, :], preferred_element_type=F32)
    o_ref[...] = x + _rmsnorm(acc, g2_ref[...])


def _ffn(x2, g1, g2, wg, wu, wd, tm=1024):
    T, D = x2.shape
    row = pl.BlockSpec((tm, D), lambda i: (i, 0))
    const = lambda shape: pl.BlockSpec(shape, lambda i: (0, 0), pipeline_mode=pl.Buffered(1))
    return pl.pallas_call(
        _ffn_kernel,
        grid=(T // tm,),
        in_specs=[row, const((1, D)), const((1, D)), const((D, D_FF)), const((D, D_FF)),
                  const((D_FF, D))],
        out_specs=row,
        out_shape=jax.ShapeDtypeStruct((T, D), F32),
        compiler_params=pltpu.CompilerParams(
            dimension_semantics=("arbitrary",), vmem_limit_bytes=VMEM_LIMIT),
        name="ffn",
    )(x2, g1, g2, wg, wu, wd)


def kernel(x, w_in, w_proj_a, w_proj_b, w_out, b_forget, w_ffn_gate, w_ffn_up, w_ffn_down,
           norm_mix_pre, norm_mix_post, norm_ffn_pre, norm_ffn_post):
    B, S, D = x.shape
    q_scale = float(LOG2E / np.sqrt(HEAD_DIM))
    for layer in range(w_in.shape[0]):
        w_qkv, w_gates, wf = _prep_w_in(w_in[layer].T, q_scale)
        bf = jnp.zeros((F_ROWS, 1), F32).at[:N_FOX_HEADS, 0].set(b_forget[layer])

        (qkv0, qkv1, qkv2, qkvb, ft, wa, wb, wo, wg_ffn, wu_ffn, wd_ffn) = _in_proj(
            x, norm_mix_pre[layer][None, :], w_qkv, wf, bf,
            [w_proj_a[layer], w_proj_b[layer], w_out[layer],
             w_ffn_gate[layer], w_ffn_up[layer], w_ffn_down[layer]])

        outs, lses = [], []
        for qkv in (qkv0, qkv1, qkv2):
            o_g, l_g = _dilated_group(qkv)
            outs.append(o_g)
            lses.append(l_g)
        ob = _fox(qkvb, ft)

        x = _mix(x, outs, lses, ob, w_gates, wa, wb, wo, norm_mix_pre[layer][None, :],
                 norm_mix_post[layer][None, :])
        x = _ffn(x.reshape(B * S, D), norm_ffn_pre[layer][None, :], norm_ffn_post[layer][None, :],
                 wg_ffn, wu_ffn, wd_ffn).reshape(B, S, D)
    return x
```

```python
import functools

import numpy as np
import jax
import jax.numpy as jnp
from jax import lax
from jax.experimental import pallas as pl
from jax.experimental.pallas import tpu as pltpu

D_MODEL = 1024
HEAD_DIM = 64
DIL_CONFIGS = ((128, 1), (512, 4), (2048, 16))
N_DIL_GROUPS = 3
GROUP_WIDTH = 256
N_FOX_HEADS = 8
FOX_WIDTH = 512
BLOCK = 128
ROPE_THETA = 500000.0
ROPE_DIM = 16
D_FF = 2816
EPS = 1e-6
NEG_INF = -1e30

DIL_WIDTH = N_DIL_GROUPS * GROUP_WIDTH
QKV_COLS = 3 * DIL_WIDTH + 3 * FOX_WIDTH
GATE_COL0 = QKV_COLS + N_FOX_HEADS
F_ROWS = 16
LANES = 128
SUBLANES = 8
MXU_COLS = 256
VMEM_LIMIT = 56 * 1024 * 1024
VT_ROWS = HEAD_DIM + 16
LOG2E = 1.4426950408889634

F32 = jnp.float32
BF16 = jnp.bfloat16
_NT = (((1,), (1,)), ((), ()))


def _rmsnorm(x, gain):
    return x * lax.rsqrt(jnp.mean(x * x, axis=-1, keepdims=True) + EPS) * gain


def _log_sigmoid(z):
    return jnp.minimum(z, 0.0) - jnp.log1p(jnp.exp(-jnp.abs(z)))


def _stage_groups(d):
    return max(d // SUBLANES, 1)


def _stage_shape(d, tm):
    g = _stage_groups(d)
    return (GROUP_WIDTH // LANES, g, tm // g, LANES)


def _stage_read_tokens(st, c, d):
    g = _stage_groups(d)
    rows = st.shape[2]
    parts = [st[c, j].reshape(rows // SUBLANES, SUBLANES, LANES) for j in range(g)]
    return jnp.stack(parts, axis=1).reshape(rows * g, LANES)


def _stage_residue_index(c, r, d, tm):
    stride = d // _stage_groups(d)
    return (c, r // stride, pl.ds(r % stride, tm // d, stride=stride), slice(None))


def _prep_w_in_kernel(wt_ref, qkv_ref, gates_ref, wf_ref, *, q_scale):
    def block(row0):
        return wt_ref[row0:row0 + LANES, :].T

    for c in range(QKV_COLS // LANES):
        is_q = c * LANES < DIL_WIDTH or 3 * DIL_WIDTH <= c * LANES < 3 * DIL_WIDTH + FOX_WIDTH
        blk = block(c * LANES)
        qkv_ref[:, c * LANES:(c + 1) * LANES] = (blk * q_scale if is_q else blk).astype(BF16)
    for c in range(2 * D_MODEL // LANES):
        gates_ref[:, c * LANES:(c + 1) * LANES] = block(GATE_COL0 + c * LANES).astype(BF16)
    lane = lax.broadcasted_iota(jnp.int32, (LANES, LANES), 1)
    wf_ref[...] = jnp.where(lane < N_FOX_HEADS, block(QKV_COLS), 0.0).astype(BF16)


def _prep_w_in(w_in_t, q_scale):
    C, D = w_in_t.shape
    return pl.pallas_call(
        functools.partial(_prep_w_in_kernel, q_scale=q_scale),
        grid=(D // LANES,),
        in_specs=[pl.BlockSpec((C, LANES), lambda i: (0, i))],
        out_specs=[pl.BlockSpec((LANES, QKV_COLS), lambda i: (i, 0)),
                   pl.BlockSpec((LANES, 2 * D), lambda i: (i, 0)),
                   pl.BlockSpec((LANES, LANES), lambda i: (i, 0))],
        out_shape=[jax.ShapeDtypeStruct((D, QKV_COLS), BF16),
                   jax.ShapeDtypeStruct((D, 2 * D), BF16),
                   jax.ShapeDtypeStruct((D, LANES), BF16)],
        compiler_params=pltpu.CompilerParams(dimension_semantics=("arbitrary",)),
        name="prep_w_in",
    )(w_in_t)


def _in_proj_kernel(*refs, tm, n_side):
    (x_ref, gain_ref, w_ref, wf_ref, bf_ref, cos_ref, sa_ref, sb_ref), refs = refs[:8], refs[8:]
    side_in, refs = refs[:n_side], refs[n_side:]
    (qkv0, qkv1, qkv2, qkvb_ref, ft_ref), refs = refs[:5], refs[5:]
    side_out, refs = refs[:n_side], refs[n_side:]
    (carry_ref,) = refs
    i = pl.program_id(1)
    for src, dst in zip(side_in, side_out):
        dst[...] = src[...].astype(BF16)
    h = _rmsnorm(x_ref[0], gain_ref[...]).astype(BF16)

    z = jnp.dot(h, wf_ref[...], preferred_element_type=F32).T[:F_ROWS] + bf_ref[...]
    logf = _log_sigmoid(z)
    lane = lax.broadcasted_iota(jnp.int32, (F_ROWS, LANES), 1)
    sums = []
    for j in range(tm // LANES):
        c = logf[:, j * LANES:(j + 1) * LANES]
        k = 1
        while k < LANES:
            c = c + jnp.where(lane >= k, pltpu.roll(c, k, 1), 0.0)
            k *= 2
        sums.append(c)
    carry = jnp.where(i == 0, 0.0, carry_ref[...])
    for j, c in enumerate(sums):
        c = c + carry
        ft_ref[0, :, j * LANES:(j + 1) * LANES] = c
        carry = jnp.broadcast_to(c[:, LANES - 1:LANES], (F_ROWS, LANES))
    carry_ref[...] = carry

    def proj(c0, width=MXU_COLS):
        return jnp.dot(h, w_ref[:, c0:c0 + width], preferred_element_type=F32)

    cos, sa, sb = cos_ref[...], sa_ref[...], sb_ref[...]

    def rope(y):
        halves = []
        for c in range(MXU_COLS // LANES):
            yc = y[:, c * LANES:(c + 1) * LANES]
            halves.append(yc * cos + pltpu.roll(yc, LANES - ROPE_DIM // 2, 1) * sa
                          + pltpu.roll(yc, ROPE_DIM // 2, 1) * sb)
        return jnp.concatenate(halves, axis=1)

    def put(o_ref, col0, y, d):
        if d == 1:
            o_ref[0, 0, :, col0:col0 + GROUP_WIDTH] = y.astype(BF16)
            return
        yt = jnp.swapaxes(y.reshape(tm // d, d, GROUP_WIDTH), 0, 1)
        for r in range(d):
            o_ref[0, r, :, col0:col0 + GROUP_WIDTH] = yt[r].astype(BF16)

    for g, o_ref in enumerate((qkv0, qkv1, qkv2)):
        d = DIL_CONFIGS[g][1]
        put(o_ref, 0, rope(proj(g * GROUP_WIDTH)), d)
        put(o_ref, GROUP_WIDTH, rope(proj(DIL_WIDTH + g * GROUP_WIDTH)), d)
        put(o_ref, 2 * GROUP_WIDTH, proj(2 * DIL_WIDTH + g * GROUP_WIDTH), d)
    base = 3 * DIL_WIDTH
    for c in range(3 * FOX_WIDTH // MXU_COLS):
        qkvb_ref[0, :, c * MXU_COLS:(c + 1) * MXU_COLS] = proj(base + c * MXU_COLS).astype(BF16)


def _rope_tables(seq):
    half = ROPE_DIM // 2
    inv_freq = np.power(ROPE_THETA, -np.arange(0, ROPE_DIM, 2, dtype=np.float64) / ROPE_DIM)
    ang = np.arange(seq, dtype=np.float64)[:, None] * inv_freq[None, :]
    cos = np.ones((seq, HEAD_DIM)); sa = np.zeros((seq, HEAD_DIM)); sb = np.zeros((seq, HEAD_DIM))
    cos[:, :half] = np.cos(ang); cos[:, half:ROPE_DIM] = np.cos(ang)
    sa[:, :half] = -np.sin(ang)
    sb[:, half:ROPE_DIM] = np.sin(ang)
    rep = LANES // HEAD_DIM
    return tuple(jnp.asarray(np.tile(t, (1, rep)), dtype=F32) for t in (cos, sa, sb))


def _in_proj(x, gain, w_main, wf, bf, side_weights, tm=1024):
    B, S, D = x.shape
    n_i = S // tm
    n_steps = B * n_i
    side_specs = [pl.BlockSpec((w.shape[0] // n_steps, w.shape[1]), lambda b, i: (b * n_i + i, 0))
                  for w in side_weights]
    side_shapes = [jax.ShapeDtypeStruct(w.shape, BF16) for w in side_weights]
    cos, sa, sb = _rope_tables(S)
    const = lambda shape: pl.BlockSpec(shape, lambda b, i: (0,) * len(shape),
                                       pipeline_mode=pl.Buffered(1))
    row = lambda w: pl.BlockSpec((1, tm, w), lambda b, i: (b, i, 0))
    tab = pl.BlockSpec((tm, LANES), lambda b, i: (i, 0))
    dils = [d for _, d in DIL_CONFIGS]
    res_shape = [jax.ShapeDtypeStruct((B, d, S // d, 3 * GROUP_WIDTH), BF16) for d in dils]
    res_spec = [pl.BlockSpec((1, d, tm // d, 3 * GROUP_WIDTH), lambda b, i: (b, 0, i, 0))
                for d in dils]
    out_shape = (res_shape
                 + [jax.ShapeDtypeStruct((B, S, 3 * FOX_WIDTH), BF16)]
                 + [jax.ShapeDtypeStruct((B, F_ROWS, S), F32)])
    out_specs = (res_spec + [row(3 * FOX_WIDTH)]
                 + [pl.BlockSpec((1, F_ROWS, tm), lambda b, i: (b, 0, i))])
    return pl.pallas_call(
        functools.partial(_in_proj_kernel, tm=tm, n_side=len(side_weights)),
        grid=(B, n_i),
        in_specs=[row(D), const((1, D)), const((D, QKV_COLS)), const((D, LANES)),
                  const((F_ROWS, 1)), tab, tab, tab] + side_specs,
        out_specs=out_specs + side_specs,
        out_shape=out_shape + side_shapes,
        scratch_shapes=[pltpu.VMEM((F_ROWS, LANES), F32)],
        compiler_params=pltpu.CompilerParams(
            dimension_semantics=("arbitrary", "arbitrary"), vmem_limit_bytes=VMEM_LIMIT),
        name="in_proj",
    )(x, gain, w_main, wf, bf, cos, sa, sb, *side_weights)


def _dilated_kernel(qkv_ref, o_ref, lse_ref, vt_ref, *, d, nb, lookahead):
    lane = lax.broadcasted_iota(jnp.int32, (BLOCK, LANES), 1)
    lo_half = lane < HEAD_DIM
    kw = 2 * BLOCK
    n_pairs = GROUP_WIDTH // LANES

    kj = lax.broadcasted_iota(jnp.int32, (kw, kw), 0)
    qi = lax.broadcasted_iota(jnp.int32, (kw, kw), 1) % BLOCK
    kj1 = lax.broadcasted_iota(jnp.int32, (BLOCK, kw), 0)
    qi1 = lax.broadcasted_iota(jnp.int32, (BLOCK, kw), 1) % BLOCK
    bias_lead = jnp.where(kj1 <= qi1, 0.0, NEG_INF)
    bias_band = jnp.where((kj >= qi) & (kj <= qi + BLOCK), 0.0, NEG_INF)

    for r in range(d):
        for hp in range(n_pairs):
            cols = slice(hp * LANES, (hp + 1) * LANES)
            for n in range(nb):
                rows = slice(n * BLOCK, (n + 1) * BLOCK)
                vt = qkv_ref[0, r, rows, 2 * GROUP_WIDTH + hp * LANES:
                             2 * GROUP_WIDTH + (hp + 1) * LANES].astype(F32).T.astype(BF16)
                vt_ref[r, hp, 0:HEAD_DIM, rows] = vt[:HEAD_DIM]
                vt_ref[r, hp, VT_ROWS:VT_ROWS + HEAD_DIM, rows] = vt[HEAD_DIM:]
            ones = jnp.ones((VT_ROWS - HEAD_DIM, nb * BLOCK), BF16)
            vt_ref[r, hp, HEAD_DIM:VT_ROWS, :] = ones
            vt_ref[r, hp, VT_ROWS + HEAD_DIM:, :] = ones

    def scores(r, n, hp):
        cols = slice(hp * LANES, (hp + 1) * LANES)
        q = qkv_ref[0, r, n * BLOCK:(n + 1) * BLOCK, cols]
        zero = jnp.zeros_like(q)
        qm = jnp.concatenate([jnp.where(lo_half, q, zero), jnp.where(lo_half, zero, q)], axis=0)
        keys = slice(0, BLOCK) if n == 0 else slice((n - 1) * BLOCK, (n + 1) * BLOCK)
        k_cols = slice(GROUP_WIDTH + hp * LANES, GROUP_WIDTH + (hp + 1) * LANES)
        st = lax.dot_general(qkv_ref[0, r, keys, k_cols], qm, _NT,
                             preferred_element_type=F32)
        return st + (bias_lead if n == 0 else bias_band)

    def finish(r, n, hp, st):
        cols = slice(hp * LANES, (hp + 1) * LANES)
        rows = slice(n * BLOCK, (n + 1) * BLOCK)
        keys = slice(0, BLOCK) if n == 0 else slice((n - 1) * BLOCK, (n + 1) * BLOCK)
        vt = vt_ref[r, hp, :, keys]
        m = jnp.max(st, axis=0, keepdims=True)
        pe = jnp.exp2(st - m).astype(BF16)
        acc = jnp.dot(vt, pe, preferred_element_type=F32)
        out_t, lse_t = [], []
        for hh in range(2):
            a = acc[hh * VT_ROWS:(hh + 1) * VT_ROWS, hh * BLOCK:(hh + 1) * BLOCK]
            den = a[HEAD_DIM:HEAD_DIM + 1]
            out_t.append(a[:HEAD_DIM] * (1.0 / den))
            lse = m[:, hh * BLOCK:(hh + 1) * BLOCK] + jnp.log2(den)
            lse_t.append(jnp.broadcast_to(lse, (HEAD_DIM, BLOCK)))
        o_ref[0, r, rows, cols] = jnp.concatenate(out_t, axis=0).T.astype(BF16)
        lse_ref[0, r, rows, cols] = jnp.concatenate(lse_t, axis=0).T

    units = [(r, n, hp) for r in range(d) for n in range(nb) for hp in range(n_pairs)]
    ready = {}
    for idx in range(len(units) + lookahead):
        if idx < len(units):
            ready[idx] = scores(*units[idx])
        if idx >= lookahead:
            finish(*units[idx - lookahead], ready.pop(idx - lookahead))


def _dilated_group(qkv, lookahead=4):
    B, d, L, _ = qkv.shape
    W = GROUP_WIDTH
    spec = pl.BlockSpec((1, d, L, W), lambda b: (b, 0, 0, 0))
    return pl.pallas_call(
        functools.partial(_dilated_kernel, d=d, nb=L // BLOCK, lookahead=lookahead),
        grid=(B,),
        in_specs=[pl.BlockSpec((1, d, L, 3 * W), lambda b: (b, 0, 0, 0))],
        out_specs=[spec, spec],
        out_shape=[jax.ShapeDtypeStruct((B, d, L, W), BF16),
                   jax.ShapeDtypeStruct((B, d, L, W), F32)],
        scratch_shapes=[pltpu.VMEM((d, W // LANES, 2 * VT_ROWS, L), BF16)],
        compiler_params=pltpu.CompilerParams(
            dimension_semantics=("arbitrary",), vmem_limit_bytes=VMEM_LIMIT),
        name=f"dilated_d{d}",
    )(qkv)


def _fox_kernel(q_ref, k_ref, v_ref, ft_ref, o_ref, vt_ref, nfcol_ref, *, tq, nblk, lookahead):
    p = pl.program_id(1)
    seq = nblk * tq
    lane = lax.broadcasted_iota(jnp.int32, (tq, LANES), 1)
    lo_half = lane < HEAD_DIM
    key = lax.broadcasted_iota(jnp.int32, (tq, tq), 0)
    qry = lax.broadcasted_iota(jnp.int32, (tq, tq), 1)
    causal = key <= qry

    for c in range(seq // tq):
        cs = slice(c * tq, (c + 1) * tq)
        vt = v_ref[0, cs, :].astype(F32).T.astype(BF16)
        for hh in range(2):
            vt_ref[hh, :HEAD_DIM, cs] = vt[hh * HEAD_DIM:(hh + 1) * HEAD_DIM]
    for hh in range(2):
        vt_ref[hh, HEAD_DIM:, :] = jnp.ones((VT_ROWS - HEAD_DIM, seq), BF16)
    for hh in range(2):
        nfrow = ft_ref[0, pl.ds(2 * p + hh, 1), :] * (-LOG2E)
        for c in range(seq // LANES):
            cs = slice(c * LANES, (c + 1) * LANES)
            nfcol_ref[hh, cs, :] = jnp.broadcast_to(nfrow[:, cs], (LANES, LANES)).T

    def scores(t, i, hh):
        j = i - t
        ks = slice(j * tq, (j + 1) * tq)
        q = q_ref[0, i * tq:(i + 1) * tq, :]
        qm = jnp.where(lo_half if hh == 0 else ~lo_half, q, jnp.zeros_like(q))
        st = lax.dot_general(k_ref[0, ks, :], qm, _NT, preferred_element_type=F32)
        st = st + jnp.concatenate([nfcol_ref[hh, ks, :]] * (tq // LANES), axis=1)
        return jnp.where(causal, st, NEG_INF) if t == 0 else st

    state = {}

    def update(t, i, hh, st):
        j = i - t
        vt = vt_ref[hh, :, j * tq:(j + 1) * tq]
        if t == 0:
            m = jnp.max(st, axis=0, keepdims=True)
            pe = jnp.exp2(st - m)
            acc = jnp.dot(vt, pe.astype(BF16), preferred_element_type=F32)
        else:
            m_old, acc_old = state[i, hh]
            m = jnp.maximum(m_old, jnp.max(st, axis=0, keepdims=True))
            pe = jnp.exp2(st - m)
            acc = jnp.exp2(m_old - m) * acc_old + jnp.dot(vt, pe.astype(BF16),
                                                          preferred_element_type=F32)
        state[i, hh] = (m, acc)
        if j == 0 and hh == 1:
            out_t = jnp.concatenate(
                [state[i, h][1][:HEAD_DIM] * (1.0 / state[i, h][1][HEAD_DIM:HEAD_DIM + 1])
                 for h in range(2)], axis=0)
            o_ref[0, i * tq:(i + 1) * tq, :] = out_t.T.astype(BF16)

    units = [(t, i, hh) for t in range(nblk) for i in range(t, nblk) for hh in range(2)]
    ready = {}
    for n in range(len(units) + lookahead):
        if n < len(units):
            ready[n] = scores(*units[n])
        if n >= lookahead:
            update(*units[n - lookahead], ready.pop(n - lookahead))


def _fox(qkv, ft, tq=256, lookahead=4):
    B, S, _ = qkv.shape
    W = FOX_WIDTH
    n_pairs = W // LANES
    spec = pl.BlockSpec((1, S, LANES), lambda b, p: (b, 0, p))
    part = lambda s: pl.BlockSpec((1, S, LANES), lambda b, p: (b, 0, s * n_pairs + p))
    return pl.pallas_call(
        functools.partial(_fox_kernel, tq=tq, nblk=S // tq, lookahead=lookahead),
        grid=(B, n_pairs),
        in_specs=[part(0), part(1), part(2),
                  pl.BlockSpec((1, F_ROWS, S), lambda b, p: (b, 0, 0))],
        out_specs=spec,
        out_shape=jax.ShapeDtypeStruct((B, S, W), BF16),
        scratch_shapes=[pltpu.VMEM((2, VT_ROWS, S), BF16), pltpu.VMEM((2, S, LANES), F32)],
        compiler_params=pltpu.CompilerParams(
            dimension_semantics=("arbitrary", "arbitrary"), vmem_limit_bytes=VMEM_LIMIT),
        name="fox",
    )(qkv, qkv, qkv, ft)


def _mix_kernel(x_ref, o0, o1, o2, l0, l1, l2, ob_ref, wg_ref, wa_ref, wb_ref, wo_ref,
                gpre_ref, gpost_ref, y_ref, *stage_refs, tm):
    stage = iter(stage_refs)
    x = x_ref[0]
    h = _rmsnorm(x, gpre_ref[...]).astype(BF16)

    def tokens(ref):
        d = ref.shape[1]
        if d == 1:
            return ref[0, 0].astype(F32)
        st = next(stage)
        for r in range(d):
            for c in range(GROUP_WIDTH // LANES):
                st[_stage_residue_index(c, r, d, tm)] = ref[
                    0, r, :, c * LANES:(c + 1) * LANES].astype(F32)
        return jnp.concatenate(
            [_stage_read_tokens(st, c, d) for c in range(GROUP_WIDTH // LANES)], axis=1)

    la, lb, lc = tokens(l0), tokens(l1), tokens(l2)
    mx = jnp.maximum(jnp.maximum(la, lb), lc)
    e0, e1, e2 = jnp.exp2(la - mx), jnp.exp2(lb - mx), jnp.exp2(lc - mx)
    inv = 1.0 / (e0 + e1 + e2)
    oa = ((e0 * tokens(o0) + e1 * tokens(o1) + e2 * tokens(o2)) * inv).astype(BF16)
    ob = ob_ref[0]
    merged = []
    for c in range(D_MODEL // MXU_COLS):
        cols = slice(c * MXU_COLS, (c + 1) * MXU_COLS)
        ga = jnp.dot(h, wg_ref[:, cols], preferred_element_type=F32)
        gb = jnp.dot(h, wg_ref[:, D_MODEL + c * MXU_COLS:D_MODEL + (c + 1) * MXU_COLS],
                     preferred_element_type=F32)
        a = jnp.dot(oa, wa_ref[:, cols], preferred_element_type=F32)
        b = jnp.dot(ob, wb_ref[:, cols], preferred_element_type=F32)
        merged.append((jax.nn.sigmoid(ga) * a + jax.nn.sigmoid(gb) * b).astype(BF16))
    mix = jnp.dot(jnp.concatenate(merged, axis=1), wo_ref[...], preferred_element_type=F32)
    y_ref[0] = x + _rmsnorm(mix, gpost_ref[...])


def _mix(x, outs, lses, ob, wg, wa, wb, wo, gain_pre, gain_post, tm=1024):
    B, S, D = x.shape
    row = lambda w: pl.BlockSpec((1, tm, w), lambda b, i: (b, i, 0))
    const = lambda shape: pl.BlockSpec(shape, lambda b, i: (0, 0), pipeline_mode=pl.Buffered(1))
    res = lambda t: pl.BlockSpec((1, t.shape[1], tm // t.shape[1], GROUP_WIDTH),
                                 lambda b, i: (b, 0, i, 0))
    stages = [pltpu.VMEM(_stage_shape(t.shape[1], tm), F32)
              for group in (lses, outs) for t in group if t.shape[1] > 1]
    return pl.pallas_call(
        functools.partial(_mix_kernel, tm=tm),
        grid=(B, S // tm),
        in_specs=([row(D)] + [res(t) for t in outs] + [res(t) for t in lses]
                  + [row(FOX_WIDTH), const((D, 2 * D)),
                     const((GROUP_WIDTH, D)), const((FOX_WIDTH, D)), const((D, D)),
                     const((1, D)), const((1, D))]),
        out_specs=row(D),
        out_shape=jax.ShapeDtypeStruct((B, S, D), F32),
        scratch_shapes=stages,
        compiler_params=pltpu.CompilerParams(
            dimension_semantics=("arbitrary", "arbitrary"), vmem_limit_bytes=VMEM_LIMIT),
        name="mix",
    )(x, *outs, *lses, ob, wg, wa, wb, wo, gain_pre, gain_post)


def _ffn_kernel(x_ref, g1_ref, g2_ref, wg_ref, wu_ref, wd_ref, o_ref):
    x = x_ref[...]
    h = _rmsnorm(x, g1_ref[...]).astype(BF16)
    acc = jnp.zeros(x.shape, F32)
    for c in range(D_FF // MXU_COLS):
        cols = slice(c * MXU_COLS, (c + 1) * MXU_COLS)
        g = jnp.dot(h, wg_ref[:, cols], preferred_element_type=F32)
        u = jnp.dot(h, wu_ref[:, cols], preferred_element_type=F32)
        a = (g * jax.nn.sigmoid(g) * u).astype(BF16)
        acc = acc + jnp.dot(a, wd_ref[cols, :], preferred_element_type=F32)
    o_ref[...] = x + _rmsnorm(acc, g2_ref[...])


def _ffn(x2, g1, g2, wg, wu, wd, tm=1024):
    T, D = x2.shape
    row = pl.BlockSpec((tm, D), lambda i: (i, 0))
    const = lambda shape: pl.BlockSpec(shape, lambda i: (0, 0), pipeline_mode=pl.Buffered(1))
    return pl.pallas_call(
        _ffn_kernel,
        grid=(T // tm,),
        in_specs=[row, const((1, D)), const((1, D)), const((D, D_FF)), const((D, D_FF)),
                  const((D_FF, D))],
        out_specs=row,
        out_shape=jax.ShapeDtypeStruct((T, D), F32),
        compiler_params=pltpu.CompilerParams(
            dimension_semantics=("arbitrary",), vmem_limit_bytes=VMEM_LIMIT),
        name="ffn",
    )(x2, g1, g2, wg, wu, wd)


def kernel(x, w_in, w_proj_a, w_proj_b, w_out, b_forget, w_ffn_gate, w_ffn_up, w_ffn_down,
           norm_mix_pre, norm_mix_post, norm_ffn_pre, norm_ffn_post):
    B, S, D = x.shape
    q_scale = float(LOG2E / np.sqrt(HEAD_DIM))
    for layer in range(w_in.shape[0]):
        w_qkv, w_gates, wf = _prep_w_in(w_in[layer].T, q_scale)
        bf = jnp.zeros((F_ROWS, 1), F32).at[:N_FOX_HEADS, 0].set(b_forget[layer])

        (qkv0, qkv1, qkv2, qkvb, ft, wa, wb, wo, wg_ffn, wu_ffn, wd_ffn) = _in_proj(
            x, norm_mix_pre[layer][None, :], w_qkv, wf, bf,
            [w_proj_a[layer], w_proj_b[layer], w_out[layer],
             w_ffn_gate[layer], w_ffn_up[layer], w_ffn_down[layer]])

        outs, lses = [], []
        for qkv in (qkv0, qkv1, qkv2):
            o_g, l_g = _dilated_group(qkv)
            outs.append(o_g)
            lses.append(l_g)
        ob = _fox(qkvb, ft)

        x = _mix(x, outs, lses, ob, w_gates, wa, wb, wo, norm_mix_pre[layer][None, :],
                 norm_mix_post[layer][None, :])
        x = _ffn(x.reshape(B * S, D), norm_ffn_pre[layer][None, :], norm_ffn_post[layer][None, :],
                 wg_ffn, wu_ffn, wd_ffn).reshape(B, S, D)
    return x
```

```python
import functools

import numpy as np
import jax
import jax.numpy as jnp
from jax import lax
from jax.experimental import pallas as pl
from jax.experimental.pallas import tpu as pltpu

D_MODEL = 1024
HEAD_DIM = 64
DIL_CONFIGS = ((128, 1), (512, 4), (2048, 16))
N_DIL_GROUPS = 3
GROUP_WIDTH = 256
N_FOX_HEADS = 8
FOX_WIDTH = 512
BLOCK = 128
ROPE_THETA = 500000.0
ROPE_DIM = 16
D_FF = 2816
EPS = 1e-6
NEG_INF = -1e30

DIL_WIDTH = N_DIL_GROUPS * GROUP_WIDTH
QKV_COLS = 3 * DIL_WIDTH + 3 * FOX_WIDTH
GATE_COL0 = QKV_COLS + N_FOX_HEADS
F_ROWS = 16
LANES = 128
SUBLANES = 8
MXU_COLS = 256
VMEM_LIMIT = 56 * 1024 * 1024
VT_ROWS = HEAD_DIM + 16
LOG2E = 1.4426950408889634

F32 = jnp.float32
BF16 = jnp.bfloat16
_NT = (((1,), (1,)), ((), ()))


def _rmsnorm(x, gain):
    return x * lax.rsqrt(jnp.mean(x * x, axis=-1, keepdims=True) + EPS) * gain


def _log_sigmoid(z):
    return jnp.minimum(z, 0.0) - jnp.log1p(jnp.exp(-jnp.abs(z)))


def _stage_groups(d):
    return max(d // SUBLANES, 1)


def _stage_shape(d, tm):
    g = _stage_groups(d)
    return (GROUP_WIDTH // LANES, g, tm // g, LANES)


def _stage_read_tokens(st, c, d):
    g = _stage_groups(d)
    rows = st.shape[2]
    parts = [st[c, j].reshape(rows // SUBLANES, SUBLANES, LANES) for j in range(g)]
    return jnp.stack(parts, axis=1).reshape(rows * g, LANES)


def _stage_residue_index(c, r, d, tm):
    stride = d // _stage_groups(d)
    return (c, r // stride, pl.ds(r % stride, tm // d, stride=stride), slice(None))


def _prep_w_in_kernel(wt_ref, qkv_ref, gates_ref, wf_ref, *, q_scale):
    def block(row0):
        return wt_ref[row0:row0 + LANES, :].T

    for c in range(QKV_COLS // LANES):
        is_q = c * LANES < DIL_WIDTH or 3 * DIL_WIDTH <= c * LANES < 3 * DIL_WIDTH + FOX_WIDTH
        blk = block(c * LANES)
        qkv_ref[:, c * LANES:(c + 1) * LANES] = (blk * q_scale if is_q else blk).astype(BF16)
    for c in range(2 * D_MODEL // LANES):
        gates_ref[:, c * LANES:(c + 1) * LANES] = block(GATE_COL0 + c * LANES).astype(BF16)
    lane = lax.broadcasted_iota(jnp.int32, (wt_ref.shape[1], LANES), 1)
    wf_ref[...] = jnp.where(lane < N_FOX_HEADS, block(QKV_COLS), 0.0).astype(BF16)


def _prep_w_in(w_in_t, q_scale, kb=256):
    C, D = w_in_t.shape
    return pl.pallas_call(
        functools.partial(_prep_w_in_kernel, q_scale=q_scale),
        grid=(D // kb,),
        in_specs=[pl.BlockSpec((C, kb), lambda i: (0, i))],
        out_specs=[pl.BlockSpec((kb, QKV_COLS), lambda i: (i, 0)),
                   pl.BlockSpec((kb, 2 * D), lambda i: (i, 0)),
                   pl.BlockSpec((kb, LANES), lambda i: (i, 0))],
        out_shape=[jax.ShapeDtypeStruct((D, QKV_COLS), BF16),
                   jax.ShapeDtypeStruct((D, 2 * D), BF16),
                   jax.ShapeDtypeStruct((D, LANES), BF16)],
        compiler_params=pltpu.CompilerParams(dimension_semantics=("arbitrary",)),
        name="prep_w_in",
    )(w_in_t)


def _in_proj_kernel(*refs, tm, n_side):
    (x_ref, gain_ref, w_ref, wf_ref, bf_ref, cos_ref, sa_ref, sb_ref), refs = refs[:8], refs[8:]
    side_in, refs = refs[:n_side], refs[n_side:]
    (qkv0, qkv1, qkv2, qkvb_ref, ft_ref), refs = refs[:5], refs[5:]
    side_out, refs = refs[:n_side], refs[n_side:]
    (carry_ref,) = refs
    i = pl.program_id(1)
    for src, dst in zip(side_in, side_out):
        dst[...] = src[...].astype(BF16)
    h = _rmsnorm(x_ref[0], gain_ref[...]).astype(BF16)

    z = jnp.dot(h, wf_ref[...], preferred_element_type=F32).T[:F_ROWS] + bf_ref[...]
    logf = _log_sigmoid(z)
    lane = lax.broadcasted_iota(jnp.int32, (F_ROWS, LANES), 1)
    sums = []
    for j in range(tm // LANES):
        c = logf[:, j * LANES:(j + 1) * LANES]
        k = 1
        while k < LANES:
            c = c + jnp.where(lane >= k, pltpu.roll(c, k, 1), 0.0)
            k *= 2
        sums.append(c)
    carry = jnp.where(i == 0, 0.0, carry_ref[...])
    for j, c in enumerate(sums):
        c = c + carry
        ft_ref[0, :, j * LANES:(j + 1) * LANES] = c
        carry = jnp.broadcast_to(c[:, LANES - 1:LANES], (F_ROWS, LANES))
    carry_ref[...] = carry

    def proj(c0, width=MXU_COLS):
        return jnp.dot(h, w_ref[:, c0:c0 + width], preferred_element_type=F32)

    cos, sa, sb = cos_ref[...], sa_ref[...], sb_ref[...]

    def rope(y):
        halves = []
        for c in range(MXU_COLS // LANES):
            yc = y[:, c * LANES:(c + 1) * LANES]
            halves.append(yc * cos + pltpu.roll(yc, LANES - ROPE_DIM // 2, 1) * sa
                          + pltpu.roll(yc, ROPE_DIM // 2, 1) * sb)
        return jnp.concatenate(halves, axis=1)

    def put(o_ref, col0, y, d):
        if d == 1:
            o_ref[0, 0, :, col0:col0 + GROUP_WIDTH] = y.astype(BF16)
            return
        yt = jnp.swapaxes(y.reshape(tm // d, d, GROUP_WIDTH), 0, 1)
        for r in range(d):
            o_ref[0, r, :, col0:col0 + GROUP_WIDTH] = yt[r].astype(BF16)

    for g, o_ref in enumerate((qkv0, qkv1, qkv2)):
        d = DIL_CONFIGS[g][1]
        put(o_ref, 0, rope(proj(g * GROUP_WIDTH)), d)
        put(o_ref, GROUP_WIDTH, rope(proj(DIL_WIDTH + g * GROUP_WIDTH)), d)
        put(o_ref, 2 * GROUP_WIDTH, proj(2 * DIL_WIDTH + g * GROUP_WIDTH), d)
    base = 3 * DIL_WIDTH
    for c in range(3 * FOX_WIDTH // MXU_COLS):
        qkvb_ref[0, :, c * MXU_COLS:(c + 1) * MXU_COLS] = proj(base + c * MXU_COLS).astype(BF16)


def _rope_tables(seq):
    half = ROPE_DIM // 2
    inv_freq = np.power(ROPE_THETA, -np.arange(0, ROPE_DIM, 2, dtype=np.float64) / ROPE_DIM)
    ang = np.arange(seq, dtype=np.float64)[:, None] * inv_freq[None, :]
    cos = np.ones((seq, HEAD_DIM)); sa = np.zeros((seq, HEAD_DIM)); sb = np.zeros((seq, HEAD_DIM))
    cos[:, :half] = np.cos(ang); cos[:, half:ROPE_DIM] = np.cos(ang)
    sa[:, :half] = -np.sin(ang)
    sb[:, half:ROPE_DIM] = np.sin(ang)
    rep = LANES // HEAD_DIM
    return tuple(jnp.asarray(np.tile(t, (1, rep)), dtype=F32) for t in (cos, sa, sb))


def _in_proj(x, gain, w_main, wf, bf, side_weights, tm=1024):
    B, S, D = x.shape
    n_i = S // tm
    n_steps = B * n_i
    side_specs = [pl.BlockSpec((w.shape[0] // n_steps, w.shape[1]), lambda b, i: (b * n_i + i, 0))
                  for w in side_weights]
    side_shapes = [jax.ShapeDtypeStruct(w.shape, BF16) for w in side_weights]
    cos, sa, sb = _rope_tables(S)
    const = lambda shape: pl.BlockSpec(shape, lambda b, i: (0,) * len(shape),
                                       pipeline_mode=pl.Buffered(1))
    row = lambda w: pl.BlockSpec((1, tm, w), lambda b, i: (b, i, 0))
    tab = pl.BlockSpec((tm, LANES), lambda b, i: (i, 0))
    dils = [d for _, d in DIL_CONFIGS]
    res_shape = [jax.ShapeDtypeStruct((B, d, S // d, 3 * GROUP_WIDTH), BF16) for d in dils]
    res_spec = [pl.BlockSpec((1, d, tm // d, 3 * GROUP_WIDTH), lambda b, i: (b, 0, i, 0))
                for d in dils]
    out_shape = (res_shape
                 + [jax.ShapeDtypeStruct((B, S, 3 * FOX_WIDTH), BF16)]
                 + [jax.ShapeDtypeStruct((B, F_ROWS, S), F32)])
    out_specs = (res_spec + [row(3 * FOX_WIDTH)]
                 + [pl.BlockSpec((1, F_ROWS, tm), lambda b, i: (b, 0, i))])
    return pl.pallas_call(
        functools.partial(_in_proj_kernel, tm=tm, n_side=len(side_weights)),
        grid=(B, n_i),
        in_specs=[row(D), const((1, D)), const((D, QKV_COLS)), const((D, LANES)),
                  const((F_ROWS, 1)), tab, tab, tab] + side_specs,
        out_specs=out_specs + side_specs,
        out_shape=out_shape + side_shapes,
        scratch_shapes=[pltpu.VMEM((F_ROWS, LANES), F32)],
        compiler_params=pltpu.CompilerParams(
            dimension_semantics=("arbitrary", "arbitrary"), vmem_limit_bytes=VMEM_LIMIT),
        name="in_proj",
    )(x, gain, w_main, wf, bf, cos, sa, sb, *side_weights)


def _dilated_kernel(qkv_ref, o_ref, lse_ref, vt_ref, *, d, nb, lookahead):
    lane = lax.broadcasted_iota(jnp.int32, (BLOCK, LANES), 1)
    lo_half = lane < HEAD_DIM
    kw = 2 * BLOCK
    n_pairs = GROUP_WIDTH // LANES

    kj = lax.broadcasted_iota(jnp.int32, (kw, kw), 0)
    qi = lax.broadcasted_iota(jnp.int32, (kw, kw), 1) % BLOCK
    kj1 = lax.broadcasted_iota(jnp.int32, (BLOCK, kw), 0)
    qi1 = lax.broadcasted_iota(jnp.int32, (BLOCK, kw), 1) % BLOCK
    bias_lead = jnp.where(kj1 <= qi1, 0.0, NEG_INF)
    bias_band = jnp.where((kj >= qi) & (kj <= qi + BLOCK), 0.0, NEG_INF)

    for r in range(d):
        for hp in range(n_pairs):
            cols = slice(hp * LANES, (hp + 1) * LANES)
            for n in range(nb):
                rows = slice(n * BLOCK, (n + 1) * BLOCK)
                vt = qkv_ref[0, r, rows, 2 * GROUP_WIDTH + hp * LANES:
                             2 * GROUP_WIDTH + (hp + 1) * LANES].astype(F32).T.astype(BF16)
                vt_ref[r, hp, 0:HEAD_DIM, rows] = vt[:HEAD_DIM]
                vt_ref[r, hp, VT_ROWS:VT_ROWS + HEAD_DIM, rows] = vt[HEAD_DIM:]
            ones = jnp.ones((VT_ROWS - HEAD_DIM, nb * BLOCK), BF16)
            vt_ref[r, hp, HEAD_DIM:VT_ROWS, :] = ones
            vt_ref[r, hp, VT_ROWS + HEAD_DIM:, :] = ones

    def scores(r, n, hp):
        cols = slice(hp * LANES, (hp + 1) * LANES)
        q = qkv_ref[0, r, n * BLOCK:(n + 1) * BLOCK, cols]
        zero = jnp.zeros_like(q)
        qm = jnp.concatenate([jnp.where(lo_half, q, zero), jnp.where(lo_half, zero, q)], axis=0)
        keys = slice(0, BLOCK) if n == 0 else slice((n - 1) * BLOCK, (n + 1) * BLOCK)
        k_cols = slice(GROUP_WIDTH + hp * LANES, GROUP_WIDTH + (hp + 1) * LANES)
        st = lax.dot_general(qkv_ref[0, r, keys, k_cols], qm, _NT,
                             preferred_element_type=F32)
        return st + (bias_lead if n == 0 else bias_band)

    def finish(r, n, hp, st):
        cols = slice(hp * LANES, (hp + 1) * LANES)
        rows = slice(n * BLOCK, (n + 1) * BLOCK)
        keys = slice(0, BLOCK) if n == 0 else slice((n - 1) * BLOCK, (n + 1) * BLOCK)
        vt = vt_ref[r, hp, :, keys]
        m = jnp.max(st, axis=0, keepdims=True)
        pe = jnp.exp2(st - m).astype(BF16)
        acc = jnp.dot(vt, pe, preferred_element_type=F32)
        out_t, lse_t = [], []
        for hh in range(2):
            a = acc[hh * VT_ROWS:(hh + 1) * VT_ROWS, hh * BLOCK:(hh + 1) * BLOCK]
            den = a[HEAD_DIM:HEAD_DIM + 1]
            out_t.append(a[:HEAD_DIM] * (1.0 / den))
            lse = m[:, hh * BLOCK:(hh + 1) * BLOCK] + jnp.log2(den)
            lse_t.append(jnp.broadcast_to(lse, (HEAD_DIM, BLOCK)))
        o_ref[0, r, rows, cols] = jnp.concatenate(out_t, axis=0).T.astype(BF16)
        lse_ref[0, r, rows, cols] = jnp.concatenate(lse_t, axis=0).T

    units = [(r, n, hp) for r in range(d) for n in range(nb) for hp in range(n_pairs)]
    ready = {}
    for idx in range(len(units) + lookahead):
        if idx < len(units):
            ready[idx] = scores(*units[idx])
        if idx >= lookahead:
            finish(*units[idx - lookahead], ready.pop(idx - lookahead))


def _dilated_group(qkv, lookahead=4):
    B, d, L, _ = qkv.shape
    W = GROUP_WIDTH
    spec = pl.BlockSpec((1, d, L, W), lambda b: (b, 0, 0, 0))
    return pl.pallas_call(
        functools.partial(_dilated_kernel, d=d, nb=L // BLOCK, lookahead=lookahead),
        grid=(B,),
        in_specs=[pl.BlockSpec((1, d, L, 3 * W), lambda b: (b, 0, 0, 0))],
        out_specs=[spec, spec],
        out_shape=[jax.ShapeDtypeStruct((B, d, L, W), BF16),
                   jax.ShapeDtypeStruct((B, d, L, W), F32)],
        scratch_shapes=[pltpu.VMEM((d, W // LANES, 2 * VT_ROWS, L), BF16)],
        compiler_params=pltpu.CompilerParams(
            dimension_semantics=("arbitrary",), vmem_limit_bytes=VMEM_LIMIT),
        name=f"dilated_d{d}",
    )(qkv)


def _fox_kernel(q_ref, k_ref, v_ref, ft_ref, o_ref, vt_ref, nfcol_ref, *, tq, nblk, lookahead,
                pairs):
    lane = lax.broadcasted_iota(jnp.int32, (tq, LANES), 1)
    lo_half = lane < HEAD_DIM
    key = lax.broadcasted_iota(jnp.int32, (tq, tq), 0)
    qry = lax.broadcasted_iota(jnp.int32, (tq, tq), 1)
    causal = key <= qry

    for ph in range(pairs):
        _fox_pair(q_ref, k_ref, v_ref, ft_ref, o_ref, vt_ref, nfcol_ref, ph,
                  pl.program_id(1) * pairs + ph, lo_half, causal, tq, nblk, lookahead)


def _fox_pair(q_ref, k_ref, v_ref, ft_ref, o_ref, vt_ref, nfcol_ref, ph, pair, lo_half, causal,
              tq, nblk, lookahead):
    seq = nblk * tq
    ls = slice(ph * LANES, (ph + 1) * LANES)
    for c in range(seq // tq):
        cs = slice(c * tq, (c + 1) * tq)
        vt = v_ref[0, cs, ls].astype(F32).T.astype(BF16)
        for hh in range(2):
            vt_ref[ph, hh, :HEAD_DIM, cs] = vt[hh * HEAD_DIM:(hh + 1) * HEAD_DIM]
    for hh in range(2):
        vt_ref[ph, hh, HEAD_DIM:, :] = jnp.ones((VT_ROWS - HEAD_DIM, seq), BF16)
    for hh in range(2):
        nfrow = ft_ref[0, pl.ds(2 * pair + hh, 1), :] * (-LOG2E)
        for c in range(seq // LANES):
            cs = slice(c * LANES, (c + 1) * LANES)
            nfcol_ref[ph, hh, cs, :] = jnp.broadcast_to(nfrow[:, cs], (LANES, LANES)).T

    def scores(t, i, hh):
        j = i - t
        ks = slice(j * tq, (j + 1) * tq)
        q = q_ref[0, i * tq:(i + 1) * tq, ls]
        qm = jnp.where(lo_half if hh == 0 else ~lo_half, q, jnp.zeros_like(q))
        st = lax.dot_general(k_ref[0, ks, ls], qm, _NT, preferred_element_type=F32)
        st = st + jnp.concatenate([nfcol_ref[ph, hh, ks, :]] * (tq // LANES), axis=1)
        return jnp.where(causal, st, NEG_INF) if t == 0 else st

    state = {}

    def update(t, i, hh, st):
        j = i - t
        vt = vt_ref[ph, hh, :, j * tq:(j + 1) * tq]
        if t == 0:
            m = jnp.max(st, axis=0, keepdims=True)
            pe = jnp.exp2(st - m)
            acc = jnp.dot(vt, pe.astype(BF16), preferred_element_type=F32)
        else:
            m_old, acc_old = state[i, hh]
            m = jnp.maximum(m_old, jnp.max(st, axis=0, keepdims=True))
            pe = jnp.exp2(st - m)
            acc = jnp.exp2(m_old - m) * acc_old + jnp.dot(vt, pe.astype(BF16),
                                                          preferred_element_type=F32)
        state[i, hh] = (m, acc)
        if j == 0 and hh == 1:
            out_t = jnp.concatenate(
                [state[i, h][1][:HEAD_DIM] * (1.0 / state[i, h][1][HEAD_DIM:HEAD_DIM + 1])
                 for h in range(2)], axis=0)
            o_ref[0, i * tq:(i + 1) * tq, ls] = out_t.T.astype(BF16)

    units = [(t, i, hh) for t in range(nblk) for i in range(t, nblk) for hh in range(2)]
    ready = {}
    for n in range(len(units) + lookahead):
        if n < len(units):
            ready[n] = scores(*units[n])
        if n >= lookahead:
            update(*units[n - lookahead], ready.pop(n - lookahead))


def _fox(qkv, ft, tq=256, lookahead=4, pairs=2):
    B, S, _ = qkv.shape
    W = FOX_WIDTH
    wb = pairs * LANES
    n_steps = W // wb
    spec = pl.BlockSpec((1, S, wb), lambda b, p: (b, 0, p))
    part = lambda s: pl.BlockSpec((1, S, wb), lambda b, p: (b, 0, s * n_steps + p))
    return pl.pallas_call(
        functools.partial(_fox_kernel, tq=tq, nblk=S // tq, lookahead=lookahead, pairs=pairs),
        grid=(B, n_steps),
        in_specs=[part(0), part(1), part(2),
                  pl.BlockSpec((1, F_ROWS, S), lambda b, p: (b, 0, 0))],
        out_specs=spec,
        out_shape=jax.ShapeDtypeStruct((B, S, W), BF16),
        scratch_shapes=[pltpu.VMEM((pairs, 2, VT_ROWS, S), BF16),
                        pltpu.VMEM((pairs, 2, S, LANES), F32)],
        compiler_params=pltpu.CompilerParams(
            dimension_semantics=("arbitrary", "arbitrary"), vmem_limit_bytes=VMEM_LIMIT),
        name="fox",
    )(qkv, qkv, qkv, ft)


def _mix_kernel(x_ref, o0, o1, o2, l0, l1, l2, ob_ref, wg_ref, wa_ref, wb_ref, wo_ref,
                gpre_ref, gpost_ref, y_ref, *stage_refs, tm):
    stage = iter(stage_refs)
    x = x_ref[0]
    h = _rmsnorm(x, gpre_ref[...]).astype(BF16)

    def tokens(ref):
        d = ref.shape[1]
        if d == 1:
            return ref[0, 0].astype(F32)
        st = next(stage)
        for r in range(d):
            for c in range(GROUP_WIDTH // LANES):
                st[_stage_residue_index(c, r, d, tm)] = ref[
                    0, r, :, c * LANES:(c + 1) * LANES].astype(F32)
        return jnp.concatenate(
            [_stage_read_tokens(st, c, d) for c in range(GROUP_WIDTH // LANES)], axis=1)

    la, lb, lc = tokens(l0), tokens(l1), tokens(l2)
    mx = jnp.maximum(jnp.maximum(la, lb), lc)
    e0, e1, e2 = jnp.exp2(la - mx), jnp.exp2(lb - mx), jnp.exp2(lc - mx)
    inv = 1.0 / (e0 + e1 + e2)
    oa = ((e0 * tokens(o0) + e1 * tokens(o1) + e2 * tokens(o2)) * inv).astype(BF16)
    ob = ob_ref[0]
    merged = []
    for c in range(D_MODEL // MXU_COLS):
        cols = slice(c * MXU_COLS, (c + 1) * MXU_COLS)
        ga = jnp.dot(h, wg_ref[:, cols], preferred_element_type=F32)
        gb = jnp.dot(h, wg_ref[:, D_MODEL + c * MXU_COLS:D_MODEL + (c + 1) * MXU_COLS],
                     preferred_element_type=F32)
        a = jnp.dot(oa, wa_ref[:, cols], preferred_element_type=F32)
        b = jnp.dot(ob, wb_ref[:, cols], preferred_element_type=F32)
        merged.append((jax.nn.sigmoid(ga) * a + jax.nn.sigmoid(gb) * b).astype(BF16))
    mix = jnp.dot(jnp.concatenate(merged, axis=1), wo_ref[...], preferred_element_type=F32)
    y_ref[0] = x + _rmsnorm(mix, gpost_ref[...])


def _mix(x, outs, lses, ob, wg, wa, wb, wo, gain_pre, gain_post, tm=1024):
    B, S, D = x.shape
    row = lambda w: pl.BlockSpec((1, tm, w), lambda b, i: (b, i, 0))
    const = lambda shape: pl.BlockSpec(shape, lambda b, i: (0, 0), pipeline_mode=pl.Buffered(1))
    res = lambda t: pl.BlockSpec((1, t.shape[1], tm // t.shape[1], GROUP_WIDTH),
                                 lambda b, i: (b, 0, i, 0))
    stages = [pltpu.VMEM(_stage_shape(t.shape[1], tm), F32)
              for group in (lses, outs) for t in group if t.shape[1] > 1]
    return pl.pallas_call(
        functools.partial(_mix_kernel, tm=tm),
        grid=(B, S // tm),
        in_specs=([row(D)] + [res(t) for t in outs] + [res(t) for t in lses]
                  + [row(FOX_WIDTH), const((D, 2 * D)),
                     const((GROUP_WIDTH, D)), const((FOX_WIDTH, D)), const((D, D)),
                     const((1, D)), const((1, D))]),
        out_specs=row(D),
        out_shape=jax.ShapeDtypeStruct((B, S, D), F32),
        scratch_shapes=stages,
        compiler_params=pltpu.CompilerParams(
            dimension_semantics=("arbitrary", "arbitrary"), vmem_limit_bytes=VMEM_LIMIT),
        name="mix",
    )(x, *outs, *lses, ob, wg, wa, wb, wo, gain_pre, gain_post)


def _ffn_kernel(x_ref, g1_ref, g2_ref, wg_ref, wu_ref, wd_ref, o_ref):
    x = x_ref[...]
    h = _rmsnorm(x, g1_ref[...]).astype(BF16)
    acc = jnp.zeros(x.shape, F32)
    for c in range(D_FF // MXU_COLS):
        cols = slice(c * MXU_COLS, (c + 1) * MXU_COLS)
        g = jnp.dot(h, wg_ref[:, cols], preferred_element_type=F32)
        u = jnp.dot(h, wu_ref[:, cols], preferred_element_type=F32)
        a = (g * jax.nn.sigmoid(g) * u).astype(BF16)
        acc = acc + jnp.dot(a, wd_ref[cols, :], preferred_element_type=F32)
    o_ref[...] = x + _rmsnorm(acc, g2_ref[...])


def _ffn(x2, g1, g2, wg, wu, wd, tm=1024):
    T, D = x2.shape
    row = pl.BlockSpec((tm, D), lambda i: (i, 0))
    const = lambda shape: pl.BlockSpec(shape, lambda i: (0, 0), pipeline_mode=pl.Buffered(1))
    return pl.pallas_call(
        _ffn_kernel,
        grid=(T // tm,),
        in_specs=[row, const((1, D)), const((1, D)), const((D, D_FF)), const((D, D_FF)),
                  const((D_FF, D))],
        out_specs=row,
        out_shape=jax.ShapeDtypeStruct((T, D), F32),
        compiler_params=pltpu.CompilerParams(
            dimension_semantics=("arbitrary",), vmem_limit_bytes=VMEM_LIMIT),
        name="ffn",
    )(x2, g1, g2, wg, wu, wd)


def kernel(x, w_in, w_proj_a, w_proj_b, w_out, b_forget, w_ffn_gate, w_ffn_up, w_ffn_down,
           norm_mix_pre, norm_mix_post, norm_ffn_pre, norm_ffn_post):
    B, S, D = x.shape
    q_scale = float(LOG2E / np.sqrt(HEAD_DIM))
    for layer in range(w_in.shape[0]):
        w_qkv, w_gates, wf = _prep_w_in(w_in[layer].T, q_scale)
        bf = jnp.zeros((F_ROWS, 1), F32).at[:N_FOX_HEADS, 0].set(b_forget[layer])

        (qkv0, qkv1, qkv2, qkvb, ft, wa, wb, wo, wg_ffn, wu_ffn, wd_ffn) = _in_proj(
            x, norm_mix_pre[layer][None, :], w_qkv, wf, bf,
            [w_proj_a[layer], w_proj_b[layer], w_out[layer],
             w_ffn_gate[layer], w_ffn_up[layer], w_ffn_down[layer]])

        outs, lses = [], []
        for qkv in (qkv0, qkv1, qkv2):
            o_g, l_g = _dilated_group(qkv)
            outs.append(o_g)
            lses.append(l_g)
        ob = _fox(qkvb, ft)

        x = _mix(x, outs, lses, ob, w_gates, wa, wb, wo, norm_mix_pre[layer][None, :],
                 norm_mix_post[layer][None, :])
        x = _ffn(x.reshape(B * S, D), norm_ffn_pre[layer][None, :], norm_ffn_post[layer][None, :],
                 wg_ffn, wu_ffn, wd_ffn).reshape(B, S, D)
    return x
```

```python
import functools

import numpy as np
import jax
import jax.numpy as jnp
from jax import lax
from jax.experimental import pallas as pl
from jax.experimental.pallas import tpu as pltpu

D_MODEL = 1024
HEAD_DIM = 64
DIL_CONFIGS = ((128, 1), (512, 4), (2048, 16))
N_DIL_GROUPS = 3
GROUP_WIDTH = 256
N_FOX_HEADS = 8
FOX_WIDTH = 512
BLOCK = 128
ROPE_THETA = 500000.0
ROPE_DIM = 16
D_FF = 2816
EPS = 1e-6
NEG_INF = -1e30

DIL_WIDTH = N_DIL_GROUPS * GROUP_WIDTH
QKV_COLS = 3 * DIL_WIDTH + 3 * FOX_WIDTH
GATE_COL0 = QKV_COLS + N_FOX_HEADS
F_ROWS = 16
LANES = 128
SUBLANES = 8
MXU_COLS = 256
VMEM_LIMIT = 56 * 1024 * 1024
VT_ROWS = HEAD_DIM + 16
LOG2E = 1.4426950408889634

F32 = jnp.float32
BF16 = jnp.bfloat16
_NT = (((1,), (1,)), ((), ()))


def _rmsnorm(x, gain):
    return x * lax.rsqrt(jnp.mean(x * x, axis=-1, keepdims=True) + EPS) * gain


def _log_sigmoid(z):
    return jnp.minimum(z, 0.0) - jnp.log1p(jnp.exp(-jnp.abs(z)))


def _stage_groups(d):
    return max(d // SUBLANES, 1)


def _stage_shape(d, tm):
    g = _stage_groups(d)
    return (GROUP_WIDTH // LANES, g, tm // g, LANES)


def _stage_read_tokens(st, c, d):
    g = _stage_groups(d)
    rows = st.shape[2]
    parts = [st[c, j].reshape(rows // SUBLANES, SUBLANES, LANES) for j in range(g)]
    return jnp.stack(parts, axis=1).reshape(rows * g, LANES)


def _stage_residue_index(c, r, d, tm):
    stride = d // _stage_groups(d)
    return (c, r // stride, pl.ds(r % stride, tm // d, stride=stride), slice(None))


def _prep_w_in_kernel(wt_ref, qkv_ref, gates_ref, wf_ref, *, q_scale):
    def block(row0):
        return wt_ref[row0:row0 + LANES, :].T

    for c in range(QKV_COLS // LANES):
        is_q = c * LANES < DIL_WIDTH or 3 * DIL_WIDTH <= c * LANES < 3 * DIL_WIDTH + FOX_WIDTH
        blk = block(c * LANES)
        qkv_ref[:, c * LANES:(c + 1) * LANES] = (blk * q_scale if is_q else blk).astype(BF16)
    for c in range(2 * D_MODEL // LANES):
        gates_ref[:, c * LANES:(c + 1) * LANES] = block(GATE_COL0 + c * LANES).astype(BF16)
    lane = lax.broadcasted_iota(jnp.int32, (wt_ref.shape[1], LANES), 1)
    wf_ref[...] = jnp.where(lane < N_FOX_HEADS, block(QKV_COLS), 0.0).astype(BF16)


def _prep_w_in(w_in_t, q_scale, kb=256):
    C, D = w_in_t.shape
    return pl.pallas_call(
        functools.partial(_prep_w_in_kernel, q_scale=q_scale),
        grid=(D // kb,),
        in_specs=[pl.BlockSpec((C, kb), lambda i: (0, i))],
        out_specs=[pl.BlockSpec((kb, QKV_COLS), lambda i: (i, 0)),
                   pl.BlockSpec((kb, 2 * D), lambda i: (i, 0)),
                   pl.BlockSpec((kb, LANES), lambda i: (i, 0))],
        out_shape=[jax.ShapeDtypeStruct((D, QKV_COLS), BF16),
                   jax.ShapeDtypeStruct((D, 2 * D), BF16),
                   jax.ShapeDtypeStruct((D, LANES), BF16)],
        compiler_params=pltpu.CompilerParams(dimension_semantics=("arbitrary",)),
        name="prep_w_in",
    )(w_in_t)


def _in_proj_kernel(*refs, tm, n_side):
    (x_ref, gain_ref, w_ref, wf_ref, bf_ref, cos_ref, sa_ref, sb_ref), refs = refs[:8], refs[8:]
    side_in, refs = refs[:n_side], refs[n_side:]
    (qkv0, qkv1, qkv2, qkvb_ref, ft_ref), refs = refs[:5], refs[5:]
    side_out, refs = refs[:n_side], refs[n_side:]
    (carry_ref,) = refs
    i = pl.program_id(1)
    for src, dst in zip(side_in, side_out):
        dst[...] = src[...].astype(BF16)
    h = _rmsnorm(x_ref[0], gain_ref[...]).astype(BF16)

    z = jnp.dot(h, wf_ref[...], preferred_element_type=F32).T[:F_ROWS] + bf_ref[...]
    logf = _log_sigmoid(z)
    lane = lax.broadcasted_iota(jnp.int32, (F_ROWS, LANES), 1)
    sums = []
    for j in range(tm // LANES):
        c = logf[:, j * LANES:(j + 1) * LANES]
        k = 1
        while k < LANES:
            c = c + jnp.where(lane >= k, pltpu.roll(c, k, 1), 0.0)
            k *= 2
        sums.append(c)
    carry = jnp.where(i == 0, 0.0, carry_ref[...])
    for j, c in enumerate(sums):
        c = c + carry
        ft_ref[0, :, j * LANES:(j + 1) * LANES] = c
        carry = jnp.broadcast_to(c[:, LANES - 1:LANES], (F_ROWS, LANES))
    carry_ref[...] = carry

    def proj(c0, width=MXU_COLS):
        return jnp.dot(h, w_ref[:, c0:c0 + width], preferred_element_type=F32)

    cos, sa, sb = cos_ref[...], sa_ref[...], sb_ref[...]

    def rope(y):
        halves = []
        for c in range(MXU_COLS // LANES):
            yc = y[:, c * LANES:(c + 1) * LANES]
            halves.append(yc * cos + pltpu.roll(yc, LANES - ROPE_DIM // 2, 1) * sa
                          + pltpu.roll(yc, ROPE_DIM // 2, 1) * sb)
        return jnp.concatenate(halves, axis=1)

    def put(o_ref, col0, y, d):
        if d == 1:
            o_ref[0, 0, :, col0:col0 + GROUP_WIDTH] = y.astype(BF16)
            return
        yt = jnp.swapaxes(y.reshape(tm // d, d, GROUP_WIDTH), 0, 1)
        for r in range(d):
            o_ref[0, r, :, col0:col0 + GROUP_WIDTH] = yt[r].astype(BF16)

    for g, o_ref in enumerate((qkv0, qkv1, qkv2)):
        d = DIL_CONFIGS[g][1]
        put(o_ref, 0, rope(proj(g * GROUP_WIDTH)), d)
        put(o_ref, GROUP_WIDTH, rope(proj(DIL_WIDTH + g * GROUP_WIDTH)), d)
        put(o_ref, 2 * GROUP_WIDTH, proj(2 * DIL_WIDTH + g * GROUP_WIDTH), d)
    base = 3 * DIL_WIDTH
    for c in range(3 * FOX_WIDTH // MXU_COLS):
        qkvb_ref[0, :, c * MXU_COLS:(c + 1) * MXU_COLS] = proj(base + c * MXU_COLS).astype(BF16)


def _rope_tables(seq):
    half = ROPE_DIM // 2
    inv_freq = np.power(ROPE_THETA, -np.arange(0, ROPE_DIM, 2, dtype=np.float64) / ROPE_DIM)
    ang = np.arange(seq, dtype=np.float64)[:, None] * inv_freq[None, :]
    cos = np.ones((seq, HEAD_DIM)); sa = np.zeros((seq, HEAD_DIM)); sb = np.zeros((seq, HEAD_DIM))
    cos[:, :half] = np.cos(ang); cos[:, half:ROPE_DIM] = np.cos(ang)
    sa[:, :half] = -np.sin(ang)
    sb[:, half:ROPE_DIM] = np.sin(ang)
    rep = LANES // HEAD_DIM
    return tuple(jnp.asarray(np.tile(t, (1, rep)), dtype=F32) for t in (cos, sa, sb))


def _in_proj(x, gain, w_main, wf, bf, side_weights, tm=1024):
    B, S, D = x.shape
    n_i = S // tm
    n_steps = B * n_i
    side_specs = [pl.BlockSpec((w.shape[0] // n_steps, w.shape[1]), lambda b, i: (b * n_i + i, 0))
                  for w in side_weights]
    side_shapes = [jax.ShapeDtypeStruct(w.shape, BF16) for w in side_weights]
    cos, sa, sb = _rope_tables(S)
    const = lambda shape: pl.BlockSpec(shape, lambda b, i: (0,) * len(shape),
                                       pipeline_mode=pl.Buffered(1))
    row = lambda w: pl.BlockSpec((1, tm, w), lambda b, i: (b, i, 0))
    tab = pl.BlockSpec((tm, LANES), lambda b, i: (i, 0))
    dils = [d for _, d in DIL_CONFIGS]
    res_shape = [jax.ShapeDtypeStruct((B, d, S // d, 3 * GROUP_WIDTH), BF16) for d in dils]
    res_spec = [pl.BlockSpec((1, d, tm // d, 3 * GROUP_WIDTH), lambda b, i: (b, 0, i, 0))
                for d in dils]
    out_shape = (res_shape
                 + [jax.ShapeDtypeStruct((B, S, 3 * FOX_WIDTH), BF16)]
                 + [jax.ShapeDtypeStruct((B, F_ROWS, S), F32)])
    out_specs = (res_spec + [row(3 * FOX_WIDTH)]
                 + [pl.BlockSpec((1, F_ROWS, tm), lambda b, i: (b, 0, i))])
    return pl.pallas_call(
        functools.partial(_in_proj_kernel, tm=tm, n_side=len(side_weights)),
        grid=(B, n_i),
        in_specs=[row(D), const((1, D)), const((D, QKV_COLS)), const((D, LANES)),
                  const((F_ROWS, 1)), tab, tab, tab] + side_specs,
        out_specs=out_specs + side_specs,
        out_shape=out_shape + side_shapes,
        scratch_shapes=[pltpu.VMEM((F_ROWS, LANES), F32)],
        compiler_params=pltpu.CompilerParams(
            dimension_semantics=("arbitrary", "arbitrary"), vmem_limit_bytes=VMEM_LIMIT),
        name="in_proj",
    )(x, gain, w_main, wf, bf, cos, sa, sb, *side_weights)


def _dilated_kernel(qkv_ref, o_ref, lse_ref, vt_ref, *, d, nb, lookahead):
    lane = lax.broadcasted_iota(jnp.int32, (BLOCK, LANES), 1)
    lo_half = lane < HEAD_DIM
    kw = 2 * BLOCK
    n_pairs = GROUP_WIDTH // LANES

    kj = lax.broadcasted_iota(jnp.int32, (kw, kw), 0)
    qi = lax.broadcasted_iota(jnp.int32, (kw, kw), 1) % BLOCK
    kj1 = lax.broadcasted_iota(jnp.int32, (BLOCK, kw), 0)
    qi1 = lax.broadcasted_iota(jnp.int32, (BLOCK, kw), 1) % BLOCK
    bias_lead = jnp.where(kj1 <= qi1, 0.0, NEG_INF)
    bias_band = jnp.where((kj >= qi) & (kj <= qi + BLOCK), 0.0, NEG_INF)

    for r in range(d):
        for hp in range(n_pairs):
            cols = slice(hp * LANES, (hp + 1) * LANES)
            for n in range(nb):
                rows = slice(n * BLOCK, (n + 1) * BLOCK)
                vt = qkv_ref[0, r, rows, 2 * GROUP_WIDTH + hp * LANES:
                             2 * GROUP_WIDTH + (hp + 1) * LANES].astype(F32).T.astype(BF16)
                vt_ref[r, hp, 0:HEAD_DIM, rows] = vt[:HEAD_DIM]
                vt_ref[r, hp, VT_ROWS:VT_ROWS + HEAD_DIM, rows] = vt[HEAD_DIM:]
            ones = jnp.ones((VT_ROWS - HEAD_DIM, nb * BLOCK), BF16)
            vt_ref[r, hp, HEAD_DIM:VT_ROWS, :] = ones
            vt_ref[r, hp, VT_ROWS + HEAD_DIM:, :] = ones

    def scores(r, n, hp):
        cols = slice(hp * LANES, (hp + 1) * LANES)
        q = qkv_ref[0, r, n * BLOCK:(n + 1) * BLOCK, cols]
        zero = jnp.zeros_like(q)
        qm = jnp.concatenate([jnp.where(lo_half, q, zero), jnp.where(lo_half, zero, q)], axis=0)
        keys = slice(0, BLOCK) if n == 0 else slice((n - 1) * BLOCK, (n + 1) * BLOCK)
        k_cols = slice(GROUP_WIDTH + hp * LANES, GROUP_WIDTH + (hp + 1) * LANES)
        st = lax.dot_general(qkv_ref[0, r, keys, k_cols], qm, _NT,
                             preferred_element_type=F32)
        return st + (bias_lead if n == 0 else bias_band)

    def finish(r, n, hp, st):
        cols = slice(hp * LANES, (hp + 1) * LANES)
        rows = slice(n * BLOCK, (n + 1) * BLOCK)
        keys = slice(0, BLOCK) if n == 0 else slice((n - 1) * BLOCK, (n + 1) * BLOCK)
        vt = vt_ref[r, hp, :, keys]
        m = jnp.max(st, axis=0, keepdims=True)
        pe = jnp.exp2(st - m).astype(BF16)
        acc = jnp.dot(vt, pe, preferred_element_type=F32)
        out_t, lse_t = [], []
        for hh in range(2):
            a = acc[hh * VT_ROWS:(hh + 1) * VT_ROWS, hh * BLOCK:(hh + 1) * BLOCK]
            den = a[HEAD_DIM:HEAD_DIM + 1]
            out_t.append(a[:HEAD_DIM] * (1.0 / den))
            lse = m[:, hh * BLOCK:(hh + 1) * BLOCK] + jnp.log2(den)
            lse_t.append(jnp.broadcast_to(lse, (HEAD_DIM, BLOCK)))
        o_ref[0, r, rows, cols] = jnp.concatenate(out_t, axis=0).T.astype(BF16)
        lse_ref[0, r, rows, cols] = jnp.concatenate(lse_t, axis=0).T

    units = [(r, n, hp) for r in range(d) for n in range(nb) for hp in range(n_pairs)]
    ready = {}
    for idx in range(len(units) + lookahead):
        if idx < len(units):
            ready[idx] = scores(*units[idx])
        if idx >= lookahead:
            finish(*units[idx - lookahead], ready.pop(idx - lookahead))


def _dilated_group(qkv, lookahead=4):
    B, d, L, _ = qkv.shape
    W = GROUP_WIDTH
    spec = pl.BlockSpec((1, d, L, W), lambda b: (b, 0, 0, 0))
    return pl.pallas_call(
        functools.partial(_dilated_kernel, d=d, nb=L // BLOCK, lookahead=lookahead),
        grid=(B,),
        in_specs=[pl.BlockSpec((1, d, L, 3 * W), lambda b: (b, 0, 0, 0))],
        out_specs=[spec, spec],
        out_shape=[jax.ShapeDtypeStruct((B, d, L, W), BF16),
                   jax.ShapeDtypeStruct((B, d, L, W), F32)],
        scratch_shapes=[pltpu.VMEM((d, W // LANES, 2 * VT_ROWS, L), BF16)],
        compiler_params=pltpu.CompilerParams(
            dimension_semantics=("arbitrary",), vmem_limit_bytes=VMEM_LIMIT),
        name=f"dilated_d{d}",
    )(qkv)


def _fox_kernel(q_ref, k_ref, v_ref, ft_ref, o_ref, vt_ref, nfcol_ref, *, tq, nblk, lookahead,
                pairs):
    lane = lax.broadcasted_iota(jnp.int32, (tq, LANES), 1)
    lo_half = lane < HEAD_DIM
    key = lax.broadcasted_iota(jnp.int32, (tq, tq), 0)
    qry = lax.broadcasted_iota(jnp.int32, (tq, tq), 1)
    causal = key <= qry

    for ph in range(pairs):
        _fox_pair(q_ref, k_ref, v_ref, ft_ref, o_ref, vt_ref, nfcol_ref, ph,
                  pl.program_id(1) * pairs + ph, lo_half, causal, tq, nblk, lookahead)


def _fox_pair(q_ref, k_ref, v_ref, ft_ref, o_ref, vt_ref, nfcol_ref, ph, pair, lo_half, causal,
              tq, nblk, lookahead):
    seq = nblk * tq
    ls = slice(ph * LANES, (ph + 1) * LANES)
    for c in range(seq // tq):
        cs = slice(c * tq, (c + 1) * tq)
        vt = v_ref[0, cs, ls].astype(F32).T.astype(BF16)
        for hh in range(2):
            vt_ref[ph, hh, :HEAD_DIM, cs] = vt[hh * HEAD_DIM:(hh + 1) * HEAD_DIM]
    for hh in range(2):
        vt_ref[ph, hh, HEAD_DIM:, :] = jnp.ones((VT_ROWS - HEAD_DIM, seq), BF16)
    for hh in range(2):
        nfrow = ft_ref[0, pl.ds(2 * pair + hh, 1), :] * (-LOG2E)
        for c in range(seq // LANES):
            cs = slice(c * LANES, (c + 1) * LANES)
            nfcol_ref[ph, hh, cs, :] = jnp.broadcast_to(nfrow[:, cs], (LANES, LANES)).T

    def scores(t, i, hh):
        j = i - t
        ks = slice(j * tq, (j + 1) * tq)
        q = q_ref[0, i * tq:(i + 1) * tq, ls]
        qm = jnp.where(lo_half if hh == 0 else ~lo_half, q, jnp.zeros_like(q))
        st = lax.dot_general(k_ref[0, ks, ls], qm, _NT, preferred_element_type=F32)
        st = st + jnp.concatenate([nfcol_ref[ph, hh, ks, :]] * (tq // LANES), axis=1)
        return jnp.where(causal, st, NEG_INF) if t == 0 else st

    state = {}

    def update(t, i, hh, st):
        j = i - t
        vt = vt_ref[ph, hh, :, j * tq:(j + 1) * tq]
        if t == 0:
            m = jnp.max(st, axis=0, keepdims=True)
            pe = jnp.exp2(st - m)
            acc = jnp.dot(vt, pe.astype(BF16), preferred_element_type=F32)
        else:
            m_old, acc_old = state[i, hh]
            m = jnp.maximum(m_old, jnp.max(st, axis=0, keepdims=True))
            pe = jnp.exp2(st - m)
            acc = jnp.exp2(m_old - m) * acc_old + jnp.dot(vt, pe.astype(BF16),
                                                          preferred_element_type=F32)
        state[i, hh] = (m, acc)
        if j == 0 and hh == 1:
            out_t = jnp.concatenate(
                [state[i, h][1][:HEAD_DIM] * (1.0 / state[i, h][1][HEAD_DIM:HEAD_DIM + 1])
                 for h in range(2)], axis=0)
            o_ref[0, i * tq:(i + 1) * tq, ls] = out_t.T.astype(BF16)

    units = [(t, i, hh) for t in range(nblk) for i in range(t, nblk) for hh in range(2)]
    ready = {}
    for n in range(len(units) + lookahead):
        if n < len(units):
            ready[n] = scores(*units[n])
        if n >= lookahead:
            update(*units[n - lookahead], ready.pop(n - lookahead))


def _fox(qkv, ft, tq=256, lookahead=4, pairs=1):
    B, S, _ = qkv.shape
    W = FOX_WIDTH
    wb = pairs * LANES
    n_steps = W // wb
    spec = pl.BlockSpec((1, S, wb), lambda b, p: (b, 0, p))
    part = lambda s: pl.BlockSpec((1, S, wb), lambda b, p: (b, 0, s * n_steps + p))
    return pl.pallas_call(
        functools.partial(_fox_kernel, tq=tq, nblk=S // tq, lookahead=lookahead, pairs=pairs),
        grid=(B, n_steps),
        in_specs=[part(0), part(1), part(2),
                  pl.BlockSpec((1, F_ROWS, S), lambda b, p: (b, 0, 0))],
        out_specs=spec,
        out_shape=jax.ShapeDtypeStruct((B, S, W), BF16),
        scratch_shapes=[pltpu.VMEM((pairs, 2, VT_ROWS, S), BF16),
                        pltpu.VMEM((pairs, 2, S, LANES), F32)],
        compiler_params=pltpu.CompilerParams(
            dimension_semantics=("arbitrary", "arbitrary"), vmem_limit_bytes=VMEM_LIMIT),
        name="fox",
    )(qkv, qkv, qkv, ft)


def _mix_kernel(x_ref, o0, o1, o2, l0, l1, l2, ob_ref, wg_ref, wa_ref, wb_ref, wo_ref,
                gpre_ref, gpost_ref, y_ref, *stage_refs, tm):
    stage = iter(stage_refs)
    x = x_ref[0]
    h = _rmsnorm(x, gpre_ref[...]).astype(BF16)

    def tokens(ref):
        d = ref.shape[1]
        if d == 1:
            return ref[0, 0].astype(F32)
        st = next(stage)
        for r in range(d):
            for c in range(GROUP_WIDTH // LANES):
                st[_stage_residue_index(c, r, d, tm)] = ref[
                    0, r, :, c * LANES:(c + 1) * LANES].astype(F32)
        return jnp.concatenate(
            [_stage_read_tokens(st, c, d) for c in range(GROUP_WIDTH // LANES)], axis=1)

    la, lb, lc = tokens(l0), tokens(l1), tokens(l2)
    mx = jnp.maximum(jnp.maximum(la, lb), lc)
    e0, e1, e2 = jnp.exp2(la - mx), jnp.exp2(lb - mx), jnp.exp2(lc - mx)
    inv = 1.0 / (e0 + e1 + e2)
    oa = ((e0 * tokens(o0) + e1 * tokens(o1) + e2 * tokens(o2)) * inv).astype(BF16)
    ob = ob_ref[0]
    merged = []
    for c in range(D_MODEL // MXU_COLS):
        cols = slice(c * MXU_COLS, (c + 1) * MXU_COLS)
        ga = jnp.dot(h, wg_ref[:, cols], preferred_element_type=F32)
        gb = jnp.dot(h, wg_ref[:, D_MODEL + c * MXU_COLS:D_MODEL + (c + 1) * MXU_COLS],
                     preferred_element_type=F32)
        a = jnp.dot(oa, wa_ref[:, cols], preferred_element_type=F32)
        b = jnp.dot(ob, wb_ref[:, cols], preferred_element_type=F32)
        merged.append((jax.nn.sigmoid(ga) * a + jax.nn.sigmoid(gb) * b).astype(BF16))
    mix = jnp.dot(jnp.concatenate(merged, axis=1), wo_ref[...], preferred_element_type=F32)
    y_ref[0] = x + _rmsnorm(mix, gpost_ref[...])


def _mix(x, outs, lses, ob, wg, wa, wb, wo, gain_pre, gain_post, tm=1024):
    B, S, D = x.shape
    row = lambda w: pl.BlockSpec((1, tm, w), lambda b, i: (b, i, 0))
    const = lambda shape: pl.BlockSpec(shape, lambda b, i: (0, 0), pipeline_mode=pl.Buffered(1))
    res = lambda t: pl.BlockSpec((1, t.shape[1], tm // t.shape[1], GROUP_WIDTH),
                                 lambda b, i: (b, 0, i, 0))
    stages = [pltpu.VMEM(_stage_shape(t.shape[1], tm), F32)
              for group in (lses, outs) for t in group if t.shape[1] > 1]
    return pl.pallas_call(
        functools.partial(_mix_kernel, tm=tm),
        grid=(B, S // tm),
        in_specs=([row(D)] + [res(t) for t in outs] + [res(t) for t in lses]
                  + [row(FOX_WIDTH), const((D, 2 * D)),
                     const((GROUP_WIDTH, D)), const((FOX_WIDTH, D)), const((D, D)),
                     const((1, D)), const((1, D))]),
        out_specs=row(D),
        out_shape=jax.ShapeDtypeStruct((B, S, D), F32),
        scratch_shapes=stages,
        compiler_params=pltpu.CompilerParams(
            dimension_semantics=("arbitrary", "arbitrary"), vmem_limit_bytes=VMEM_LIMIT),
        name="mix",
    )(x, *outs, *lses, ob, wg, wa, wb, wo, gain_pre, gain_post)


def _ffn_kernel(x_ref, g1_ref, g2_ref, wg_ref, wu_ref, wd_ref, o_ref):
    x = x_ref[...]
    h = _rmsnorm(x, g1_ref[...]).astype(BF16)
    acc = jnp.zeros(x.shape, F32)
    for c in range(D_FF // MXU_COLS):
        cols = slice(c * MXU_COLS, (c + 1) * MXU_COLS)
        g = jnp.dot(h, wg_ref[:, cols], preferred_element_type=F32)
        u = jnp.dot(h, wu_ref[:, cols], preferred_element_type=F32)
        a = (g * jax.nn.sigmoid(g) * u).astype(BF16)
        acc = acc + jnp.dot(a, wd_ref[cols, :], preferred_element_type=F32)
    o_ref[...] = x + _rmsnorm(acc, g2_ref[...])


def _ffn(x2, g1, g2, wg, wu, wd, tm=1024):
    T, D = x2.shape
    row = pl.BlockSpec((tm, D), lambda i: (i, 0))
    const = lambda shape: pl.BlockSpec(shape, lambda i: (0, 0), pipeline_mode=pl.Buffered(1))
    return pl.pallas_call(
        _ffn_kernel,
        grid=(T // tm,),
        in_specs=[row, const((1, D)), const((1, D)), const((D, D_FF)), const((D, D_FF)),
                  const((D_FF, D))],
        out_specs=row,
        out_shape=jax.ShapeDtypeStruct((T, D), F32),
        compiler_params=pltpu.CompilerParams(
            dimension_semantics=("arbitrary",), vmem_limit_bytes=VMEM_LIMIT),
        name="ffn",
    )(x2, g1, g2, wg, wu, wd)


def kernel(x, w_in, w_proj_a, w_proj_b, w_out, b_forget, w_ffn_gate, w_ffn_up, w_ffn_down,
           norm_mix_pre, norm_mix_post, norm_ffn_pre, norm_ffn_post):
    B, S, D = x.shape
    q_scale = float(LOG2E / np.sqrt(HEAD_DIM))
    for layer in range(w_in.shape[0]):
        w_qkv, w_gates, wf = _prep_w_in(w_in[layer].T, q_scale)
        bf = jnp.zeros((F_ROWS, 1), F32).at[:N_FOX_HEADS, 0].set(b_forget[layer])

        (qkv0, qkv1, qkv2, qkvb, ft, wa, wb, wo, wg_ffn, wu_ffn, wd_ffn) = _in_proj(
            x, norm_mix_pre[layer][None, :], w_qkv, wf, bf,
            [w_proj_a[layer], w_proj_b[layer], w_out[layer],
             w_ffn_gate[layer], w_ffn_up[layer], w_ffn_down[layer]])

        outs, lses = [], []
        for qkv in (qkv0, qkv1, qkv2):
            o_g, l_g = _dilated_group(qkv)
            outs.append(o_g)
            lses.append(l_g)
        ob = _fox(qkvb, ft)

        x = _mix(x, outs, lses, ob, w_gates, wa, wb, wo, norm_mix_pre[layer][None, :],
                 norm_mix_post[layer][None, :])
        x = _ffn(x.reshape(B * S, D), norm_ffn_pre[layer][None, :], norm_ffn_post[layer][None, :],
                 wg_ffn, wu_ffn, wd_ffn).reshape(B, S, D)
    return x
```

```python
import functools

import numpy as np
import jax
import jax.numpy as jnp
from jax import lax
from jax.experimental import pallas as pl
from jax.experimental.pallas import tpu as pltpu

D_MODEL = 1024
HEAD_DIM = 64
DIL_CONFIGS = ((128, 1), (512, 4), (2048, 16))
N_DIL_GROUPS = 3
GROUP_WIDTH = 256
N_FOX_HEADS = 8
FOX_WIDTH = 512
BLOCK = 128
ROPE_THETA = 500000.0
ROPE_DIM = 16
D_FF = 2816
EPS = 1e-6
NEG_INF = -1e30

DIL_WIDTH = N_DIL_GROUPS * GROUP_WIDTH
QKV_COLS = 3 * DIL_WIDTH + 3 * FOX_WIDTH
GATE_COL0 = QKV_COLS + N_FOX_HEADS
F_ROWS = 16
LANES = 128
SUBLANES = 8
MXU_COLS = 256
VMEM_LIMIT = 56 * 1024 * 1024
VT_ROWS = HEAD_DIM + 16
LOG2E = 1.4426950408889634

F32 = jnp.float32
BF16 = jnp.bfloat16
_NT = (((1,), (1,)), ((), ()))


def _rmsnorm(x, gain):
    return x * lax.rsqrt(jnp.mean(x * x, axis=-1, keepdims=True) + EPS) * gain


def _log_sigmoid(z):
    return jnp.minimum(z, 0.0) - jnp.log1p(jnp.exp(-jnp.abs(z)))


def _stage_groups(d):
    return max(d // SUBLANES, 1)


def _stage_shape(d, tm):
    g = _stage_groups(d)
    return (GROUP_WIDTH // LANES, g, tm // g, LANES)


def _stage_read_tokens(st, c, d):
    g = _stage_groups(d)
    rows = st.shape[2]
    parts = [st[c, j].reshape(rows // SUBLANES, SUBLANES, LANES) for j in range(g)]
    return jnp.stack(parts, axis=1).reshape(rows * g, LANES)


def _stage_residue_index(c, r, d, tm):
    stride = d // _stage_groups(d)
    return (c, r // stride, pl.ds(r % stride, tm // d, stride=stride), slice(None))


def _prep_w_in_kernel(wt_ref, qkv_ref, gates_ref, wf_ref, *, q_scale):
    def block(row0):
        return wt_ref[row0:row0 + LANES, :].T

    for c in range(QKV_COLS // LANES):
        is_q = c * LANES < DIL_WIDTH or 3 * DIL_WIDTH <= c * LANES < 3 * DIL_WIDTH + FOX_WIDTH
        blk = block(c * LANES)
        qkv_ref[:, c * LANES:(c + 1) * LANES] = (blk * q_scale if is_q else blk).astype(BF16)
    for c in range(2 * D_MODEL // LANES):
        gates_ref[:, c * LANES:(c + 1) * LANES] = block(GATE_COL0 + c * LANES).astype(BF16)
    lane = lax.broadcasted_iota(jnp.int32, (wt_ref.shape[1], LANES), 1)
    wf_ref[...] = jnp.where(lane < N_FOX_HEADS, block(QKV_COLS), 0.0).astype(BF16)


def _prep_w_in(w_in_t, q_scale, kb=256):
    C, D = w_in_t.shape
    return pl.pallas_call(
        functools.partial(_prep_w_in_kernel, q_scale=q_scale),
        grid=(D // kb,),
        in_specs=[pl.BlockSpec((C, kb), lambda i: (0, i))],
        out_specs=[pl.BlockSpec((kb, QKV_COLS), lambda i: (i, 0)),
                   pl.BlockSpec((kb, 2 * D), lambda i: (i, 0)),
                   pl.BlockSpec((kb, LANES), lambda i: (i, 0))],
        out_shape=[jax.ShapeDtypeStruct((D, QKV_COLS), BF16),
                   jax.ShapeDtypeStruct((D, 2 * D), BF16),
                   jax.ShapeDtypeStruct((D, LANES), BF16)],
        compiler_params=pltpu.CompilerParams(dimension_semantics=("arbitrary",)),
        name="prep_w_in",
    )(w_in_t)


def _in_proj_kernel(*refs, tm, n_side):
    (x_ref, gain_ref, w_ref, wf_ref, bf_ref, cos_ref, sa_ref, sb_ref), refs = refs[:8], refs[8:]
    side_in, refs = refs[:n_side], refs[n_side:]
    (qkv0, qkv1, qkv2, qkvb_ref, ft_ref), refs = refs[:5], refs[5:]
    side_out, refs = refs[:n_side], refs[n_side:]
    (carry_ref,) = refs
    i = pl.program_id(1)
    h = _rmsnorm(x_ref[0], gain_ref[...]).astype(BF16)

    z = jnp.dot(h, wf_ref[...], preferred_element_type=F32).T[:F_ROWS] + bf_ref[...]
    logf = _log_sigmoid(z)
    lane = lax.broadcasted_iota(jnp.int32, (F_ROWS, LANES), 1)
    sums = []
    for j in range(tm // LANES):
        c = logf[:, j * LANES:(j + 1) * LANES]
        k = 1
        while k < LANES:
            c = c + jnp.where(lane >= k, pltpu.roll(c, k, 1), 0.0)
            k *= 2
        sums.append(c)
    carry = jnp.where(i == 0, 0.0, carry_ref[...])
    for j, c in enumerate(sums):
        c = c + carry
        ft_ref[0, :, j * LANES:(j + 1) * LANES] = c
        carry = jnp.broadcast_to(c[:, LANES - 1:LANES], (F_ROWS, LANES))
    carry_ref[...] = carry

    def proj(c0, width=MXU_COLS):
        return jnp.dot(h, w_ref[:, c0:c0 + width], preferred_element_type=F32)

    cos, sa, sb = cos_ref[...], sa_ref[...], sb_ref[...]

    def rope(y):
        halves = []
        for c in range(MXU_COLS // LANES):
            yc = y[:, c * LANES:(c + 1) * LANES]
            halves.append(yc * cos + pltpu.roll(yc, LANES - ROPE_DIM // 2, 1) * sa
                          + pltpu.roll(yc, ROPE_DIM // 2, 1) * sb)
        return jnp.concatenate(halves, axis=1)

    def put(o_ref, col0, y, d):
        if d == 1:
            o_ref[0, 0, :, col0:col0 + GROUP_WIDTH] = y.astype(BF16)
            return
        yt = jnp.swapaxes(y.reshape(tm // d, d, GROUP_WIDTH), 0, 1)
        for r in range(d):
            o_ref[0, r, :, col0:col0 + GROUP_WIDTH] = yt[r].astype(BF16)

    for g, o_ref in enumerate((qkv0, qkv1, qkv2)):
        d = DIL_CONFIGS[g][1]
        put(o_ref, 0, rope(proj(g * GROUP_WIDTH)), d)
        put(o_ref, GROUP_WIDTH, rope(proj(DIL_WIDTH + g * GROUP_WIDTH)), d)
        put(o_ref, 2 * GROUP_WIDTH, proj(2 * DIL_WIDTH + g * GROUP_WIDTH), d)
    base = 3 * DIL_WIDTH
    for c in range(3 * FOX_WIDTH // MXU_COLS):
        qkvb_ref[0, :, c * MXU_COLS:(c + 1) * MXU_COLS] = proj(base + c * MXU_COLS).astype(BF16)
    for src, dst in zip(side_in, side_out):
        dst[...] = src[...].astype(BF16)


def _rope_tables(seq):
    half = ROPE_DIM // 2
    inv_freq = np.power(ROPE_THETA, -np.arange(0, ROPE_DIM, 2, dtype=np.float64) / ROPE_DIM)
    ang = np.arange(seq, dtype=np.float64)[:, None] * inv_freq[None, :]
    cos = np.ones((seq, HEAD_DIM)); sa = np.zeros((seq, HEAD_DIM)); sb = np.zeros((seq, HEAD_DIM))
    cos[:, :half] = np.cos(ang); cos[:, half:ROPE_DIM] = np.cos(ang)
    sa[:, :half] = -np.sin(ang)
    sb[:, half:ROPE_DIM] = np.sin(ang)
    rep = LANES // HEAD_DIM
    return tuple(jnp.asarray(np.tile(t, (1, rep)), dtype=F32) for t in (cos, sa, sb))


def _in_proj(x, gain, w_main, wf, bf, side_weights, tm=1024):
    B, S, D = x.shape
    n_i = S // tm
    n_steps = B * n_i
    side_specs = [pl.BlockSpec((w.shape[0] // n_steps, w.shape[1]), lambda b, i: (b * n_i + i, 0))
                  for w in side_weights]
    side_shapes = [jax.ShapeDtypeStruct(w.shape, BF16) for w in side_weights]
    cos, sa, sb = _rope_tables(S)
    const = lambda shape: pl.BlockSpec(shape, lambda b, i: (0,) * len(shape),
                                       pipeline_mode=pl.Buffered(1))
    row = lambda w: pl.BlockSpec((1, tm, w), lambda b, i: (b, i, 0))
    tab = pl.BlockSpec((tm, LANES), lambda b, i: (i, 0))
    dils = [d for _, d in DIL_CONFIGS]
    res_shape = [jax.ShapeDtypeStruct((B, d, S // d, 3 * GROUP_WIDTH), BF16) for d in dils]
    res_spec = [pl.BlockSpec((1, d, tm // d, 3 * GROUP_WIDTH), lambda b, i: (b, 0, i, 0))
                for d in dils]
    out_shape = (res_shape
                 + [jax.ShapeDtypeStruct((B, S, 3 * FOX_WIDTH), BF16)]
                 + [jax.ShapeDtypeStruct((B, F_ROWS, S), F32)])
    out_specs = (res_spec + [row(3 * FOX_WIDTH)]
                 + [pl.BlockSpec((1, F_ROWS, tm), lambda b, i: (b, 0, i))])
    return pl.pallas_call(
        functools.partial(_in_proj_kernel, tm=tm, n_side=len(side_weights)),
        grid=(B, n_i),
        in_specs=[row(D), const((1, D)), const((D, QKV_COLS)), const((D, LANES)),
                  const((F_ROWS, 1)), tab, tab, tab] + side_specs,
        out_specs=out_specs + side_specs,
        out_shape=out_shape + side_shapes,
        scratch_shapes=[pltpu.VMEM((F_ROWS, LANES), F32)],
        compiler_params=pltpu.CompilerParams(
            dimension_semantics=("arbitrary", "arbitrary"), vmem_limit_bytes=VMEM_LIMIT),
        name="in_proj",
    )(x, gain, w_main, wf, bf, cos, sa, sb, *side_weights)


def _dilated_kernel(qkv_ref, o_ref, lse_ref, vt_ref, *, d, nb, lookahead):
    lane = lax.broadcasted_iota(jnp.int32, (BLOCK, LANES), 1)
    lo_half = lane < HEAD_DIM
    kw = 2 * BLOCK
    n_pairs = GROUP_WIDTH // LANES

    kj = lax.broadcasted_iota(jnp.int32, (kw, kw), 0)
    qi = lax.broadcasted_iota(jnp.int32, (kw, kw), 1) % BLOCK
    kj1 = lax.broadcasted_iota(jnp.int32, (BLOCK, kw), 0)
    qi1 = lax.broadcasted_iota(jnp.int32, (BLOCK, kw), 1) % BLOCK
    bias_lead = jnp.where(kj1 <= qi1, 0.0, NEG_INF)
    bias_band = jnp.where((kj >= qi) & (kj <= qi + BLOCK), 0.0, NEG_INF)

    for r in range(d):
        for hp in range(n_pairs):
            cols = slice(hp * LANES, (hp + 1) * LANES)
            for n in range(nb):
                rows = slice(n * BLOCK, (n + 1) * BLOCK)
                vt = qkv_ref[0, r, rows, 2 * GROUP_WIDTH + hp * LANES:
                             2 * GROUP_WIDTH + (hp + 1) * LANES].astype(F32).T.astype(BF16)
                vt_ref[r, hp, 0:HEAD_DIM, rows] = vt[:HEAD_DIM]
                vt_ref[r, hp, VT_ROWS:VT_ROWS + HEAD_DIM, rows] = vt[HEAD_DIM:]
            ones = jnp.ones((VT_ROWS - HEAD_DIM, nb * BLOCK), BF16)
            vt_ref[r, hp, HEAD_DIM:VT_ROWS, :] = ones
            vt_ref[r, hp, VT_ROWS + HEAD_DIM:, :] = ones

    def scores(r, n, hp):
        cols = slice(hp * LANES, (hp + 1) * LANES)
        q = qkv_ref[0, r, n * BLOCK:(n + 1) * BLOCK, cols]
        zero = jnp.zeros_like(q)
        qm = jnp.concatenate([jnp.where(lo_half, q, zero), jnp.where(lo_half, zero, q)], axis=0)
        keys = slice(0, BLOCK) if n == 0 else slice((n - 1) * BLOCK, (n + 1) * BLOCK)
        k_cols = slice(GROUP_WIDTH + hp * LANES, GROUP_WIDTH + (hp + 1) * LANES)
        st = lax.dot_general(qkv_ref[0, r, keys, k_cols], qm, _NT,
                             preferred_element_type=F32)
        return st + (bias_lead if n == 0 else bias_band)

    def finish(r, n, hp, st):
        cols = slice(hp * LANES, (hp + 1) * LANES)
        rows = slice(n * BLOCK, (n + 1) * BLOCK)
        keys = slice(0, BLOCK) if n == 0 else slice((n - 1) * BLOCK, (n + 1) * BLOCK)
        vt = vt_ref[r, hp, :, keys]
        m = jnp.max(st, axis=0, keepdims=True)
        pe = jnp.exp2(st - m).astype(BF16)
        acc = jnp.dot(vt, pe, preferred_element_type=F32)
        out_t, lse_t = [], []
        for hh in range(2):
            a = acc[hh * VT_ROWS:(hh + 1) * VT_ROWS, hh * BLOCK:(hh + 1) * BLOCK]
            den = a[HEAD_DIM:HEAD_DIM + 1]
            out_t.append(a[:HEAD_DIM] * (1.0 / den))
            lse = m[:, hh * BLOCK:(hh + 1) * BLOCK] + jnp.log2(den)
            lse_t.append(jnp.broadcast_to(lse, (HEAD_DIM, BLOCK)))
        o_ref[0, r, rows, cols] = jnp.concatenate(out_t, axis=0).T.astype(BF16)
        lse_ref[0, r, rows, cols] = jnp.concatenate(lse_t, axis=0).T

    units = [(r, n, hp) for r in range(d) for n in range(nb) for hp in range(n_pairs)]
    ready = {}
    for idx in range(len(units) + lookahead):
        if idx < len(units):
            ready[idx] = scores(*units[idx])
        if idx >= lookahead:
            finish(*units[idx - lookahead], ready.pop(idx - lookahead))


def _dilated_group(qkv, lookahead=4):
    B, d, L, _ = qkv.shape
    W = GROUP_WIDTH
    spec = pl.BlockSpec((1, d, L, W), lambda b: (b, 0, 0, 0))
    return pl.pallas_call(
        functools.partial(_dilated_kernel, d=d, nb=L // BLOCK, lookahead=lookahead),
        grid=(B,),
        in_specs=[pl.BlockSpec((1, d, L, 3 * W), lambda b: (b, 0, 0, 0))],
        out_specs=[spec, spec],
        out_shape=[jax.ShapeDtypeStruct((B, d, L, W), BF16),
                   jax.ShapeDtypeStruct((B, d, L, W), F32)],
        scratch_shapes=[pltpu.VMEM((d, W // LANES, 2 * VT_ROWS, L), BF16)],
        compiler_params=pltpu.CompilerParams(
            dimension_semantics=("arbitrary",), vmem_limit_bytes=VMEM_LIMIT),
        name=f"dilated_d{d}",
    )(qkv)


def _fox_kernel(q_ref, k_ref, v_ref, ft_ref, o_ref, vt_ref, nfcol_ref, *, tq, nblk, lookahead,
                pairs):
    lane = lax.broadcasted_iota(jnp.int32, (tq, LANES), 1)
    lo_half = lane < HEAD_DIM
    key = lax.broadcasted_iota(jnp.int32, (tq, tq), 0)
    qry = lax.broadcasted_iota(jnp.int32, (tq, tq), 1)
    causal = key <= qry

    for ph in range(pairs):
        _fox_pair(q_ref, k_ref, v_ref, ft_ref, o_ref, vt_ref, nfcol_ref, ph,
                  pl.program_id(1) * pairs + ph, lo_half, causal, tq, nblk, lookahead)


def _fox_pair(q_ref, k_ref, v_ref, ft_ref, o_ref, vt_ref, nfcol_ref, ph, pair, lo_half, causal,
              tq, nblk, lookahead):
    seq = nblk * tq
    ls = slice(ph * LANES, (ph + 1) * LANES)
    for c in range(seq // tq):
        cs = slice(c * tq, (c + 1) * tq)
        vt = v_ref[0, cs, ls].astype(F32).T.astype(BF16)
        for hh in range(2):
            vt_ref[ph, hh, :HEAD_DIM, cs] = vt[hh * HEAD_DIM:(hh + 1) * HEAD_DIM]
    for hh in range(2):
        vt_ref[ph, hh, HEAD_DIM:, :] = jnp.ones((VT_ROWS - HEAD_DIM, seq), BF16)
    for hh in range(2):
        nfrow = ft_ref[0, pl.ds(2 * pair + hh, 1), :] * (-LOG2E)
        for c in range(seq // LANES):
            cs = slice(c * LANES, (c + 1) * LANES)
            nfcol_ref[ph, hh, cs, :] = jnp.broadcast_to(nfrow[:, cs], (LANES, LANES)).T

    def scores(t, i, hh):
        j = i - t
        ks = slice(j * tq, (j + 1) * tq)
        q = q_ref[0, i * tq:(i + 1) * tq, ls]
        qm = jnp.where(lo_half if hh == 0 else ~lo_half, q, jnp.zeros_like(q))
        st = lax.dot_general(k_ref[0, ks, ls], qm, _NT, preferred_element_type=F32)
        st = st + jnp.concatenate([nfcol_ref[ph, hh, ks, :]] * (tq // LANES), axis=1)
        return jnp.where(causal, st, NEG_INF) if t == 0 else st

    state = {}

    def update(t, i, hh, st):
        j = i - t
        vt = vt_ref[ph, hh, :, j * tq:(j + 1) * tq]
        if t == 0:
            m = jnp.max(st, axis=0, keepdims=True)
            pe = jnp.exp2(st - m)
            acc = jnp.dot(vt, pe.astype(BF16), preferred_element_type=F32)
        else:
            m_old, acc_old = state[i, hh]
            m = jnp.maximum(m_old, jnp.max(st, axis=0, keepdims=True))
            pe = jnp.exp2(st - m)
            acc = jnp.exp2(m_old - m) * acc_old + jnp.dot(vt, pe.astype(BF16),
                                                          preferred_element_type=F32)
        state[i, hh] = (m, acc)
        if j == 0 and hh == 1:
            out_t = jnp.concatenate(
                [state[i, h][1][:HEAD_DIM] * (1.0 / state[i, h][1][HEAD_DIM:HEAD_DIM + 1])
                 for h in range(2)], axis=0)
            o_ref[0, i * tq:(i + 1) * tq, ls] = out_t.T.astype(BF16)

    units = [(t, i, hh) for t in range(nblk) for i in range(t, nblk) for hh in range(2)]
    ready = {}
    for n in range(len(units) + lookahead):
        if n < len(units):
            ready[n] = scores(*units[n])
        if n >= lookahead:
            update(*units[n - lookahead], ready.pop(n - lookahead))


def _fox(qkv, ft, tq=256, lookahead=5, pairs=1):
    B, S, _ = qkv.shape
    W = FOX_WIDTH
    wb = pairs * LANES
    n_steps = W // wb
    spec = pl.BlockSpec((1, S, wb), lambda b, p: (b, 0, p))
    part = lambda s: pl.BlockSpec((1, S, wb), lambda b, p: (b, 0, s * n_steps + p))
    return pl.pallas_call(
        functools.partial(_fox_kernel, tq=tq, nblk=S // tq, lookahead=lookahead, pairs=pairs),
        grid=(B, n_steps),
        in_specs=[part(0), part(1), part(2),
                  pl.BlockSpec((1, F_ROWS, S), lambda b, p: (b, 0, 0))],
        out_specs=spec,
        out_shape=jax.ShapeDtypeStruct((B, S, W), BF16),
        scratch_shapes=[pltpu.VMEM((pairs, 2, VT_ROWS, S), BF16),
                        pltpu.VMEM((pairs, 2, S, LANES), F32)],
        compiler_params=pltpu.CompilerParams(
            dimension_semantics=("arbitrary", "arbitrary"), vmem_limit_bytes=VMEM_LIMIT),
        name="fox",
    )(qkv, qkv, qkv, ft)


def _mix_kernel(x_ref, o0, o1, o2, l0, l1, l2, ob_ref, wg_ref, wa_ref, wb_ref, wo_ref,
                gpre_ref, gpost_ref, y_ref, *stage_refs, tm):
    stage = iter(stage_refs)
    x = x_ref[0]
    h = _rmsnorm(x, gpre_ref[...]).astype(BF16)

    def tokens(ref):
        d = ref.shape[1]
        if d == 1:
            return ref[0, 0].astype(F32)
        st = next(stage)
        for r in range(d):
            for c in range(GROUP_WIDTH // LANES):
                st[_stage_residue_index(c, r, d, tm)] = ref[
                    0, r, :, c * LANES:(c + 1) * LANES].astype(F32)
        return jnp.concatenate(
            [_stage_read_tokens(st, c, d) for c in range(GROUP_WIDTH // LANES)], axis=1)

    la, lb, lc = tokens(l0), tokens(l1), tokens(l2)
    mx = jnp.maximum(jnp.maximum(la, lb), lc)
    e0, e1, e2 = jnp.exp2(la - mx), jnp.exp2(lb - mx), jnp.exp2(lc - mx)
    inv = 1.0 / (e0 + e1 + e2)
    oa = ((e0 * tokens(o0) + e1 * tokens(o1) + e2 * tokens(o2)) * inv).astype(BF16)
    ob = ob_ref[0]
    merged = []
    for c in range(D_MODEL // MXU_COLS):
        cols = slice(c * MXU_COLS, (c + 1) * MXU_COLS)
        ga = jnp.dot(h, wg_ref[:, cols], preferred_element_type=F32)
        gb = jnp.dot(h, wg_ref[:, D_MODEL + c * MXU_COLS:D_MODEL + (c + 1) * MXU_COLS],
                     preferred_element_type=F32)
        a = jnp.dot(oa, wa_ref[:, cols], preferred_element_type=F32)
        b = jnp.dot(ob, wb_ref[:, cols], preferred_element_type=F32)
        merged.append((jax.nn.sigmoid(ga) * a + jax.nn.sigmoid(gb) * b).astype(BF16))
    mix = jnp.dot(jnp.concatenate(merged, axis=1), wo_ref[...], preferred_element_type=F32)
    y_ref[0] = x + _rmsnorm(mix, gpost_ref[...])


def _mix(x, outs, lses, ob, wg, wa, wb, wo, gain_pre, gain_post, tm=1024):
    B, S, D = x.shape
    row = lambda w: pl.BlockSpec((1, tm, w), lambda b, i: (b, i, 0))
    const = lambda shape: pl.BlockSpec(shape, lambda b, i: (0, 0), pipeline_mode=pl.Buffered(1))
    res = lambda t: pl.BlockSpec((1, t.shape[1], tm // t.shape[1], GROUP_WIDTH),
                                 lambda b, i: (b, 0, i, 0))
    stages = [pltpu.VMEM(_stage_shape(t.shape[1], tm), F32)
              for group in (lses, outs) for t in group if t.shape[1] > 1]
    return pl.pallas_call(
        functools.partial(_mix_kernel, tm=tm),
        grid=(B, S // tm),
        in_specs=([row(D)] + [res(t) for t in outs] + [res(t) for t in lses]
                  + [row(FOX_WIDTH), const((D, 2 * D)),
                     const((GROUP_WIDTH, D)), const((FOX_WIDTH, D)), const((D, D)),
                     const((1, D)), const((1, D))]),
        out_specs=row(D),
        out_shape=jax.ShapeDtypeStruct((B, S, D), F32),
        scratch_shapes=stages,
        compiler_params=pltpu.CompilerParams(
            dimension_semantics=("arbitrary", "arbitrary"), vmem_limit_bytes=VMEM_LIMIT),
        name="mix",
    )(x, *outs, *lses, ob, wg, wa, wb, wo, gain_pre, gain_post)


def _ffn_kernel(x_ref, g1_ref, g2_ref, wg_ref, wu_ref, wd_ref, o_ref):
    x = x_ref[...]
    h = _rmsnorm(x, g1_ref[...]).astype(BF16)
    acc = jnp.zeros(x.shape, F32)
    for c in range(D_FF // MXU_COLS):
        cols = slice(c * MXU_COLS, (c + 1) * MXU_COLS)
        g = jnp.dot(h, wg_ref[:, cols], preferred_element_type=F32)
        u = jnp.dot(h, wu_ref[:, cols], preferred_element_type=F32)
        a = (g * jax.nn.sigmoid(g) * u).astype(BF16)
        acc = acc + jnp.dot(a, wd_ref[cols, :], preferred_element_type=F32)
    o_ref[...] = x + _rmsnorm(acc, g2_ref[...])


def _ffn(x2, g1, g2, wg, wu, wd, tm=1024):
    T, D = x2.shape
    row = pl.BlockSpec((tm, D), lambda i: (i, 0))
    const = lambda shape: pl.BlockSpec(shape, lambda i: (0, 0), pipeline_mode=pl.Buffered(1))
    return pl.pallas_call(
        _ffn_kernel,
        grid=(T // tm,),
        in_specs=[row, const((1, D)), const((1, D)), const((D, D_FF)), const((D, D_FF)),
                  const((D_FF, D))],
        out_specs=row,
        out_shape=jax.ShapeDtypeStruct((T, D), F32),
        compiler_params=pltpu.CompilerParams(
            dimension_semantics=("arbitrary",), vmem_limit_bytes=VMEM_LIMIT),
        name="ffn",
    )(x2, g1, g2, wg, wu, wd)


def kernel(x, w_in, w_proj_a, w_proj_b, w_out, b_forget, w_ffn_gate, w_ffn_up, w_ffn_down,
           norm_mix_pre, norm_mix_post, norm_ffn_pre, norm_ffn_post):
    B, S, D = x.shape
    q_scale = float(LOG2E / np.sqrt(HEAD_DIM))
    for layer in range(w_in.shape[0]):
        w_qkv, w_gates, wf = _prep_w_in(w_in[layer].T, q_scale)
        bf = jnp.zeros((F_ROWS, 1), F32).at[:N_FOX_HEADS, 0].set(b_forget[layer])

        (qkv0, qkv1, qkv2, qkvb, ft, wa, wb, wo, wg_ffn, wu_ffn, wd_ffn) = _in_proj(
            x, norm_mix_pre[layer][None, :], w_qkv, wf, bf,
            [w_proj_a[layer], w_proj_b[layer], w_out[layer],
             w_ffn_gate[layer], w_ffn_up[layer], w_ffn_down[layer]])

        outs, lses = [], []
        for qkv in (qkv0, qkv1, qkv2):
            o_g, l_g = _dilated_group(qkv)
            outs.append(o_g)
            lses.append(l_g)
        ob = _fox(qkvb, ft)

        x = _mix(x, outs, lses, ob, w_gates, wa, wb, wo, norm_mix_pre[layer][None, :],
                 norm_mix_post[layer][None, :])
        x = _ffn(x.reshape(B * S, D), norm_ffn_pre[layer][None, :], norm_ffn_post[layer][None, :],
                 wg_ffn, wu_ffn, wd_ffn).reshape(B, S, D)
    return x
```

```python
import functools

import numpy as np
import jax
import jax.numpy as jnp
from jax import lax
from jax.experimental import pallas as pl
from jax.experimental.pallas import tpu as pltpu

D_MODEL = 1024
HEAD_DIM = 64
DIL_CONFIGS = ((128, 1), (512, 4), (2048, 16))
N_DIL_GROUPS = 3
GROUP_WIDTH = 256
N_FOX_HEADS = 8
FOX_WIDTH = 512
BLOCK = 128
ROPE_THETA = 500000.0
ROPE_DIM = 16
D_FF = 2816
EPS = 1e-6
NEG_INF = -1e30

DIL_WIDTH = N_DIL_GROUPS * GROUP_WIDTH
QKV_COLS = 3 * DIL_WIDTH + 3 * FOX_WIDTH
GATE_COL0 = QKV_COLS + N_FOX_HEADS
F_ROWS = 16
LANES = 128
SUBLANES = 8
MXU_COLS = 256
VMEM_LIMIT = 56 * 1024 * 1024
VT_ROWS = HEAD_DIM + 16
LOG2E = 1.4426950408889634

F32 = jnp.float32
BF16 = jnp.bfloat16
_NT = (((1,), (1,)), ((), ()))


def _rmsnorm(x, gain):
    return x * lax.rsqrt(jnp.mean(x * x, axis=-1, keepdims=True) + EPS) * gain


def _log_sigmoid(z):
    return jnp.minimum(z, 0.0) - jnp.log1p(jnp.exp(-jnp.abs(z)))


def _stage_groups(d):
    return max(d // SUBLANES, 1)


def _stage_shape(d, tm):
    g = _stage_groups(d)
    return (GROUP_WIDTH // LANES, g, tm // g, LANES)


def _stage_read_tokens(st, c, d):
    g = _stage_groups(d)
    rows = st.shape[2]
    parts = [st[c, j].reshape(rows // SUBLANES, SUBLANES, LANES) for j in range(g)]
    return jnp.stack(parts, axis=1).reshape(rows * g, LANES)


def _stage_residue_index(c, r, d, tm):
    stride = d // _stage_groups(d)
    return (c, r // stride, pl.ds(r % stride, tm // d, stride=stride), slice(None))


def _prep_w_in_kernel(wt_ref, qkv_ref, gates_ref, wf_ref, *, q_scale):
    def block(row0):
        return wt_ref[row0:row0 + LANES, :].T

    for c in range(QKV_COLS // LANES):
        is_q = c * LANES < DIL_WIDTH or 3 * DIL_WIDTH <= c * LANES < 3 * DIL_WIDTH + FOX_WIDTH
        blk = block(c * LANES)
        qkv_ref[:, c * LANES:(c + 1) * LANES] = (blk * q_scale if is_q else blk).astype(BF16)
    for c in range(2 * D_MODEL // LANES):
        gates_ref[:, c * LANES:(c + 1) * LANES] = block(GATE_COL0 + c * LANES).astype(BF16)
    lane = lax.broadcasted_iota(jnp.int32, (wt_ref.shape[1], LANES), 1)
    wf_ref[...] = jnp.where(lane < N_FOX_HEADS, block(QKV_COLS), 0.0).astype(BF16)


def _prep_w_in(w_in_t, q_scale, kb=256):
    C, D = w_in_t.shape
    return pl.pallas_call(
        functools.partial(_prep_w_in_kernel, q_scale=q_scale),
        grid=(D // kb,),
        in_specs=[pl.BlockSpec((C, kb), lambda i: (0, i))],
        out_specs=[pl.BlockSpec((kb, QKV_COLS), lambda i: (i, 0)),
                   pl.BlockSpec((kb, 2 * D), lambda i: (i, 0)),
                   pl.BlockSpec((kb, LANES), lambda i: (i, 0))],
        out_shape=[jax.ShapeDtypeStruct((D, QKV_COLS), BF16),
                   jax.ShapeDtypeStruct((D, 2 * D), BF16),
                   jax.ShapeDtypeStruct((D, LANES), BF16)],
        compiler_params=pltpu.CompilerParams(dimension_semantics=("arbitrary",)),
        name="prep_w_in",
    )(w_in_t)


def _in_proj_kernel(*refs, tm, n_side):
    (x_ref, gain_ref, w_ref, wf_ref, bf_ref, cos_ref, sa_ref, sb_ref), refs = refs[:8], refs[8:]
    side_in, refs = refs[:n_side], refs[n_side:]
    (qkv0, qkv1, qkv2, qkvb_ref, ft_ref), refs = refs[:5], refs[5:]
    side_out, refs = refs[:n_side], refs[n_side:]
    (carry_ref,) = refs
    i = pl.program_id(1)
    h = _rmsnorm(x_ref[0], gain_ref[...]).astype(BF16)

    z = jnp.dot(h, wf_ref[...], preferred_element_type=F32).T[:F_ROWS] + bf_ref[...]
    logf = _log_sigmoid(z)
    lane = lax.broadcasted_iota(jnp.int32, (F_ROWS, LANES), 1)
    sums = []
    for j in range(tm // LANES):
        c = logf[:, j * LANES:(j + 1) * LANES]
        k = 1
        while k < LANES:
            c = c + jnp.where(lane >= k, pltpu.roll(c, k, 1), 0.0)
            k *= 2
        sums.append(c)
    carry = jnp.where(i == 0, 0.0, carry_ref[...])
    for j, c in enumerate(sums):
        c = c + carry
        ft_ref[0, :, j * LANES:(j + 1) * LANES] = c
        carry = jnp.broadcast_to(c[:, LANES - 1:LANES], (F_ROWS, LANES))
    carry_ref[...] = carry

    def proj(c0, width=MXU_COLS):
        return jnp.dot(h, w_ref[:, c0:c0 + width], preferred_element_type=F32)

    cos, sa, sb = cos_ref[...], sa_ref[...], sb_ref[...]

    def rope(y):
        halves = []
        for c in range(MXU_COLS // LANES):
            yc = y[:, c * LANES:(c + 1) * LANES]
            halves.append(yc * cos + pltpu.roll(yc, LANES - ROPE_DIM // 2, 1) * sa
                          + pltpu.roll(yc, ROPE_DIM // 2, 1) * sb)
        return jnp.concatenate(halves, axis=1)

    def put(o_ref, col0, y, d):
        if d == 1:
            o_ref[0, 0, :, col0:col0 + GROUP_WIDTH] = y.astype(BF16)
            return
        yt = jnp.swapaxes(y.reshape(tm // d, d, GROUP_WIDTH), 0, 1)
        for r in range(d):
            o_ref[0, r, :, col0:col0 + GROUP_WIDTH] = yt[r].astype(BF16)

    for g, o_ref in enumerate((qkv0, qkv1, qkv2)):
        d = DIL_CONFIGS[g][1]
        put(o_ref, 0, rope(proj(g * GROUP_WIDTH)), d)
        put(o_ref, GROUP_WIDTH, rope(proj(DIL_WIDTH + g * GROUP_WIDTH)), d)
        put(o_ref, 2 * GROUP_WIDTH, proj(2 * DIL_WIDTH + g * GROUP_WIDTH), d)
    base = 3 * DIL_WIDTH
    for c in range(3 * FOX_WIDTH // MXU_COLS):
        qkvb_ref[0, :, c * MXU_COLS:(c + 1) * MXU_COLS] = proj(base + c * MXU_COLS).astype(BF16)
    for src, dst in zip(side_in, side_out):
        dst[...] = src[...].astype(BF16)


def _rope_tables(seq):
    half = ROPE_DIM // 2
    inv_freq = np.power(ROPE_THETA, -np.arange(0, ROPE_DIM, 2, dtype=np.float64) / ROPE_DIM)
    ang = np.arange(seq, dtype=np.float64)[:, None] * inv_freq[None, :]
    cos = np.ones((seq, HEAD_DIM)); sa = np.zeros((seq, HEAD_DIM)); sb = np.zeros((seq, HEAD_DIM))
    cos[:, :half] = np.cos(ang); cos[:, half:ROPE_DIM] = np.cos(ang)
    sa[:, :half] = -np.sin(ang)
    sb[:, half:ROPE_DIM] = np.sin(ang)
    rep = LANES // HEAD_DIM
    return tuple(jnp.asarray(np.tile(t, (1, rep)), dtype=F32) for t in (cos, sa, sb))


def _in_proj(x, gain, w_main, wf, bf, side_weights, tm=1024):
    B, S, D = x.shape
    n_i = S // tm
    n_steps = B * n_i
    side_specs = [pl.BlockSpec((w.shape[0] // n_steps, w.shape[1]), lambda b, i: (b * n_i + i, 0))
                  for w in side_weights]
    side_shapes = [jax.ShapeDtypeStruct(w.shape, BF16) for w in side_weights]
    cos, sa, sb = _rope_tables(S)
    const = lambda shape: pl.BlockSpec(shape, lambda b, i: (0,) * len(shape),
                                       pipeline_mode=pl.Buffered(1))
    row = lambda w: pl.BlockSpec((1, tm, w), lambda b, i: (b, i, 0))
    tab = pl.BlockSpec((tm, LANES), lambda b, i: (i, 0))
    dils = [d for _, d in DIL_CONFIGS]
    res_shape = [jax.ShapeDtypeStruct((B, d, S // d, 3 * GROUP_WIDTH), BF16) for d in dils]
    res_spec = [pl.BlockSpec((1, d, tm // d, 3 * GROUP_WIDTH), lambda b, i: (b, 0, i, 0))
                for d in dils]
    out_shape = (res_shape
                 + [jax.ShapeDtypeStruct((B, S, 3 * FOX_WIDTH), BF16)]
                 + [jax.ShapeDtypeStruct((B, F_ROWS, S), F32)])
    out_specs = (res_spec + [row(3 * FOX_WIDTH)]
                 + [pl.BlockSpec((1, F_ROWS, tm), lambda b, i: (b, 0, i))])
    return pl.pallas_call(
        functools.partial(_in_proj_kernel, tm=tm, n_side=len(side_weights)),
        grid=(B, n_i),
        in_specs=[row(D), const((1, D)), const((D, QKV_COLS)), const((D, LANES)),
                  const((F_ROWS, 1)), tab, tab, tab] + side_specs,
        out_specs=out_specs + side_specs,
        out_shape=out_shape + side_shapes,
        scratch_shapes=[pltpu.VMEM((F_ROWS, LANES), F32)],
        compiler_params=pltpu.CompilerParams(
            dimension_semantics=("arbitrary", "arbitrary"), vmem_limit_bytes=VMEM_LIMIT),
        name="in_proj",
    )(x, gain, w_main, wf, bf, cos, sa, sb, *side_weights)


def _dilated_kernel(qkv_ref, o_ref, lse_ref, vt_ref, *, d, nb, lookahead):
    lane = lax.broadcasted_iota(jnp.int32, (BLOCK, LANES), 1)
    lo_half = lane < HEAD_DIM
    kw = 2 * BLOCK
    n_pairs = GROUP_WIDTH // LANES

    kj = lax.broadcasted_iota(jnp.int32, (kw, kw), 0)
    qi = lax.broadcasted_iota(jnp.int32, (kw, kw), 1) % BLOCK
    kj1 = lax.broadcasted_iota(jnp.int32, (BLOCK, kw), 0)
    qi1 = lax.broadcasted_iota(jnp.int32, (BLOCK, kw), 1) % BLOCK
    bias_lead = jnp.where(kj1 <= qi1, 0.0, NEG_INF)
    bias_band = jnp.where((kj >= qi) & (kj <= qi + BLOCK), 0.0, NEG_INF)

    for r in range(d):
        for hp in range(n_pairs):
            cols = slice(hp * LANES, (hp + 1) * LANES)
            for n in range(nb):
                rows = slice(n * BLOCK, (n + 1) * BLOCK)
                vt = qkv_ref[0, r, rows, 2 * GROUP_WIDTH + hp * LANES:
                             2 * GROUP_WIDTH + (hp + 1) * LANES].astype(F32).T.astype(BF16)
                vt_ref[r, hp, 0:HEAD_DIM, rows] = vt[:HEAD_DIM]
                vt_ref[r, hp, VT_ROWS:VT_ROWS + HEAD_DIM, rows] = vt[HEAD_DIM:]
            ones = jnp.ones((VT_ROWS - HEAD_DIM, nb * BLOCK), BF16)
            vt_ref[r, hp, HEAD_DIM:VT_ROWS, :] = ones
            vt_ref[r, hp, VT_ROWS + HEAD_DIM:, :] = ones

    def scores(r, n, hp):
        cols = slice(hp * LANES, (hp + 1) * LANES)
        q = qkv_ref[0, r, n * BLOCK:(n + 1) * BLOCK, cols]
        zero = jnp.zeros_like(q)
        qm = jnp.concatenate([jnp.where(lo_half, q, zero), jnp.where(lo_half, zero, q)], axis=0)
        keys = slice(0, BLOCK) if n == 0 else slice((n - 1) * BLOCK, (n + 1) * BLOCK)
        k_cols = slice(GROUP_WIDTH + hp * LANES, GROUP_WIDTH + (hp + 1) * LANES)
        st = lax.dot_general(qkv_ref[0, r, keys, k_cols], qm, _NT,
                             preferred_element_type=F32)
        return st + (bias_lead if n == 0 else bias_band)

    def finish(r, n, hp, st):
        cols = slice(hp * LANES, (hp + 1) * LANES)
        rows = slice(n * BLOCK, (n + 1) * BLOCK)
        keys = slice(0, BLOCK) if n == 0 else slice((n - 1) * BLOCK, (n + 1) * BLOCK)
        vt = vt_ref[r, hp, :, keys]
        m = jnp.max(st, axis=0, keepdims=True)
        pe = jnp.exp2(st - m).astype(BF16)
        acc = jnp.dot(vt, pe, preferred_element_type=F32)
        out_t, lse_t = [], []
        for hh in range(2):
            a = acc[hh * VT_ROWS:(hh + 1) * VT_ROWS, hh * BLOCK:(hh + 1) * BLOCK]
            den = a[HEAD_DIM:HEAD_DIM + 1]
            out_t.append(a[:HEAD_DIM] * (1.0 / den))
            lse = m[:, hh * BLOCK:(hh + 1) * BLOCK] + jnp.log2(den)
            lse_t.append(jnp.broadcast_to(lse, (HEAD_DIM, BLOCK)))
        o_ref[0, r, rows, cols] = jnp.concatenate(out_t, axis=0).T.astype(BF16)
        lse_ref[0, r, rows, cols] = jnp.concatenate(lse_t, axis=0).T

    units = [(r, n, hp) for r in range(d) for n in range(nb) for hp in range(n_pairs)]
    ready = {}
    for idx in range(len(units) + lookahead):
        if idx < len(units):
            ready[idx] = scores(*units[idx])
        if idx >= lookahead:
            finish(*units[idx - lookahead], ready.pop(idx - lookahead))


def _dilated_group(qkv, lookahead=4):
    B, d, L, _ = qkv.shape
    W = GROUP_WIDTH
    spec = pl.BlockSpec((1, d, L, W), lambda b: (b, 0, 0, 0))
    return pl.pallas_call(
        functools.partial(_dilated_kernel, d=d, nb=L // BLOCK, lookahead=lookahead),
        grid=(B,),
        in_specs=[pl.BlockSpec((1, d, L, 3 * W), lambda b: (b, 0, 0, 0))],
        out_specs=[spec, spec],
        out_shape=[jax.ShapeDtypeStruct((B, d, L, W), BF16),
                   jax.ShapeDtypeStruct((B, d, L, W), F32)],
        scratch_shapes=[pltpu.VMEM((d, W // LANES, 2 * VT_ROWS, L), BF16)],
        compiler_params=pltpu.CompilerParams(
            dimension_semantics=("arbitrary",), vmem_limit_bytes=VMEM_LIMIT),
        name=f"dilated_d{d}",
    )(qkv)


def _fox_kernel(q_ref, k_ref, v_ref, ft_ref, o_ref, vt_ref, nfcol_ref, *, tq, nblk, lookahead,
                pairs):
    lane = lax.broadcasted_iota(jnp.int32, (tq, LANES), 1)
    lo_half = lane < HEAD_DIM
    key = lax.broadcasted_iota(jnp.int32, (tq, tq), 0)
    qry = lax.broadcasted_iota(jnp.int32, (tq, tq), 1)
    causal = key <= qry

    for ph in range(pairs):
        _fox_pair(q_ref, k_ref, v_ref, ft_ref, o_ref, vt_ref, nfcol_ref, ph,
                  pl.program_id(1) * pairs + ph, lo_half, causal, tq, nblk, lookahead)


def _fox_pair(q_ref, k_ref, v_ref, ft_ref, o_ref, vt_ref, nfcol_ref, ph, pair, lo_half, causal,
              tq, nblk, lookahead):
    seq = nblk * tq
    ls = slice(ph * LANES, (ph + 1) * LANES)
    for c in range(seq // tq):
        cs = slice(c * tq, (c + 1) * tq)
        vt = v_ref[0, cs, ls].astype(F32).T.astype(BF16)
        for hh in range(2):
            vt_ref[ph, hh, :HEAD_DIM, cs] = vt[hh * HEAD_DIM:(hh + 1) * HEAD_DIM]
    for hh in range(2):
        vt_ref[ph, hh, HEAD_DIM:, :] = jnp.ones((VT_ROWS - HEAD_DIM, seq), BF16)
    for hh in range(2):
        nfrow = ft_ref[0, pl.ds(2 * pair + hh, 1), :] * (-LOG2E)
        for c in range(seq // LANES):
            cs = slice(c * LANES, (c + 1) * LANES)
            nfcol_ref[ph, hh, cs, :] = jnp.broadcast_to(nfrow[:, cs], (LANES, LANES)).T

    def scores(t, i, hh):
        j = i - t
        ks = slice(j * tq, (j + 1) * tq)
        q = q_ref[0, i * tq:(i + 1) * tq, ls]
        qm = jnp.where(lo_half if hh == 0 else ~lo_half, q, jnp.zeros_like(q))
        st = lax.dot_general(k_ref[0, ks, ls], qm, _NT, preferred_element_type=F32)
        st = st + jnp.concatenate([nfcol_ref[ph, hh, ks, :]] * (tq // LANES), axis=1)
        return jnp.where(causal, st, NEG_INF) if t == 0 else st

    state = {}

    def update(t, i, hh, st):
        j = i - t
        vt = vt_ref[ph, hh, :, j * tq:(j + 1) * tq]
        if t == 0:
            m = jnp.max(st, axis=0, keepdims=True)
            pe = jnp.exp2(st - m)
            acc = jnp.dot(vt, pe.astype(BF16), preferred_element_type=F32)
        else:
            m_old, acc_old = state[i, hh]
            m = jnp.maximum(m_old, jnp.max(st, axis=0, keepdims=True))
            pe = jnp.exp2(st - m)
            acc = jnp.exp2(m_old - m) * acc_old + jnp.dot(vt, pe.astype(BF16),
                                                          preferred_element_type=F32)
        state[i, hh] = (m, acc)
        if j == 0 and hh == 1:
            out_t = jnp.concatenate(
                [state[i, h][1][:HEAD_DIM] * (1.0 / state[i, h][1][HEAD_DIM:HEAD_DIM + 1])
                 for h in range(2)], axis=0)
            o_ref[0, i * tq:(i + 1) * tq, ls] = out_t.T.astype(BF16)

    units = [(t, i, hh) for t in range(nblk) for i in range(t, nblk) for hh in range(2)]
    ready = {}
    for n in range(len(units) + lookahead):
        if n < len(units):
            ready[n] = scores(*units[n])
        if n >= lookahead:
            update(*units[n - lookahead], ready.pop(n - lookahead))


def _fox(qkv, ft, tq=256, lookahead=5, pairs=2):
    B, S, _ = qkv.shape
    W = FOX_WIDTH
    wb = pairs * LANES
    n_steps = W // wb
    spec = pl.BlockSpec((1, S, wb), lambda b, p: (b, 0, p))
    part = lambda s: pl.BlockSpec((1, S, wb), lambda b, p: (b, 0, s * n_steps + p))
    return pl.pallas_call(
        functools.partial(_fox_kernel, tq=tq, nblk=S // tq, lookahead=lookahead, pairs=pairs),
        grid=(B, n_steps),
        in_specs=[part(0), part(1), part(2),
                  pl.BlockSpec((1, F_ROWS, S), lambda b, p: (b, 0, 0))],
        out_specs=spec,
        out_shape=jax.ShapeDtypeStruct((B, S, W), BF16),
        scratch_shapes=[pltpu.VMEM((pairs, 2, VT_ROWS, S), BF16),
                        pltpu.VMEM((pairs, 2, S, LANES), F32)],
        compiler_params=pltpu.CompilerParams(
            dimension_semantics=("arbitrary", "arbitrary"), vmem_limit_bytes=VMEM_LIMIT),
        name="fox",
    )(qkv, qkv, qkv, ft)


def _mix_kernel(x_ref, o0, o1, o2, l0, l1, l2, ob_ref, wg_ref, wa_ref, wb_ref, wo_ref,
                gpre_ref, gpost_ref, y_ref, *stage_refs, tm):
    stage = iter(stage_refs)
    x = x_ref[0]
    h = _rmsnorm(x, gpre_ref[...]).astype(BF16)

    def tokens(ref):
        d = ref.shape[1]
        if d == 1:
            return ref[0, 0].astype(F32)
        st = next(stage)
        for r in range(d):
            for c in range(GROUP_WIDTH // LANES):
                st[_stage_residue_index(c, r, d, tm)] = ref[
                    0, r, :, c * LANES:(c + 1) * LANES].astype(F32)
        return jnp.concatenate(
            [_stage_read_tokens(st, c, d) for c in range(GROUP_WIDTH // LANES)], axis=1)

    la, lb, lc = tokens(l0), tokens(l1), tokens(l2)
    mx = jnp.maximum(jnp.maximum(la, lb), lc)
    e0, e1, e2 = jnp.exp2(la - mx), jnp.exp2(lb - mx), jnp.exp2(lc - mx)
    inv = 1.0 / (e0 + e1 + e2)
    oa = ((e0 * tokens(o0) + e1 * tokens(o1) + e2 * tokens(o2)) * inv).astype(BF16)
    ob = ob_ref[0]
    merged = []
    for c in range(D_MODEL // MXU_COLS):
        cols = slice(c * MXU_COLS, (c + 1) * MXU_COLS)
        ga = jnp.dot(h, wg_ref[:, cols], preferred_element_type=F32)
        gb = jnp.dot(h, wg_ref[:, D_MODEL + c * MXU_COLS:D_MODEL + (c + 1) * MXU_COLS],
                     preferred_element_type=F32)
        a = jnp.dot(oa, wa_ref[:, cols], preferred_element_type=F32)
        b = jnp.dot(ob, wb_ref[:, cols], preferred_element_type=F32)
        merged.append((jax.nn.sigmoid(ga) * a + jax.nn.sigmoid(gb) * b).astype(BF16))
    mix = jnp.dot(jnp.concatenate(merged, axis=1), wo_ref[...], preferred_element_type=F32)
    y_ref[0] = x + _rmsnorm(mix, gpost_ref[...])


def _mix(x, outs, lses, ob, wg, wa, wb, wo, gain_pre, gain_post, tm=1024):
    B, S, D = x.shape
    row = lambda w: pl.BlockSpec((1, tm, w), lambda b, i: (b, i, 0))
    const = lambda shape: pl.BlockSpec(shape, lambda b, i: (0, 0), pipeline_mode=pl.Buffered(1))
    res = lambda t: pl.BlockSpec((1, t.shape[1], tm // t.shape[1], GROUP_WIDTH),
                                 lambda b, i: (b, 0, i, 0))
    stages = [pltpu.VMEM(_stage_shape(t.shape[1], tm), F32)
              for group in (lses, outs) for t in group if t.shape[1] > 1]
    return pl.pallas_call(
        functools.partial(_mix_kernel, tm=tm),
        grid=(B, S // tm),
        in_specs=([row(D)] + [res(t) for t in outs] + [res(t) for t in lses]
                  + [row(FOX_WIDTH), const((D, 2 * D)),
                     const((GROUP_WIDTH, D)), const((FOX_WIDTH, D)), const((D, D)),
                     const((1, D)), const((1, D))]),
        out_specs=row(D),
        out_shape=jax.ShapeDtypeStruct((B, S, D), F32),
        scratch_shapes=stages,
        compiler_params=pltpu.CompilerParams(
            dimension_semantics=("arbitrary", "arbitrary"), vmem_limit_bytes=VMEM_LIMIT),
        name="mix",
    )(x, *outs, *lses, ob, wg, wa, wb, wo, gain_pre, gain_post)


def _ffn_kernel(x_ref, g1_ref, g2_ref, wg_ref, wu_ref, wd_ref, o_ref):
    x = x_ref[...]
    h = _rmsnorm(x, g1_ref[...]).astype(BF16)
    acc = jnp.zeros(x.shape, F32)
    for c in range(D_FF // MXU_COLS):
        cols = slice(c * MXU_COLS, (c + 1) * MXU_COLS)
        g = jnp.dot(h, wg_ref[:, cols], preferred_element_type=F32)
        u = jnp.dot(h, wu_ref[:, cols], preferred_element_type=F32)
        a = (g * jax.nn.sigmoid(g) * u).astype(BF16)
        acc = acc + jnp.dot(a, wd_ref[cols, :], preferred_element_type=F32)
    o_ref[...] = x + _rmsnorm(acc, g2_ref[...])


def _ffn(x2, g1, g2, wg, wu, wd, tm=1024):
    T, D = x2.shape
    row = pl.BlockSpec((tm, D), lambda i: (i, 0))
    const = lambda shape: pl.BlockSpec(shape, lambda i: (0, 0), pipeline_mode=pl.Buffered(1))
    return pl.pallas_call(
        _ffn_kernel,
        grid=(T // tm,),
        in_specs=[row, const((1, D)), const((1, D)), const((D, D_FF)), const((D, D_FF)),
                  const((D_FF, D))],
        out_specs=row,
        out_shape=jax.ShapeDtypeStruct((T, D), F32),
        compiler_params=pltpu.CompilerParams(
            dimension_semantics=("arbitrary",), vmem_limit_bytes=VMEM_LIMIT),
        name="ffn",
    )(x2, g1, g2, wg, wu, wd)


def kernel(x, w_in, w_proj_a, w_proj_b, w_out, b_forget, w_ffn_gate, w_ffn_up, w_ffn_down,
           norm_mix_pre, norm_mix_post, norm_ffn_pre, norm_ffn_post):
    B, S, D = x.shape
    q_scale = float(LOG2E / np.sqrt(HEAD_DIM))
    for layer in range(w_in.shape[0]):
        w_qkv, w_gates, wf = _prep_w_in(w_in[layer].T, q_scale)
        bf = jnp.zeros((F_ROWS, 1), F32).at[:N_FOX_HEADS, 0].set(b_forget[layer])

        (qkv0, qkv1, qkv2, qkvb, ft, wa, wb, wo, wg_ffn, wu_ffn, wd_ffn) = _in_proj(
            x, norm_mix_pre[layer][None, :], w_qkv, wf, bf,
            [w_proj_a[layer], w_proj_b[layer], w_out[layer],
             w_ffn_gate[layer], w_ffn_up[layer], w_ffn_down[layer]])

        ob = _fox(qkvb, ft)
        outs, lses = [], []
        for qkv in (qkv0, qkv1, qkv2):
            o_g, l_g = _dilated_group(qkv)
            outs.append(o_g)
            lses.append(l_g)

        x = _mix(x, outs, lses, ob, w_gates, wa, wb, wo, norm_mix_pre[layer][None, :],
                 norm_mix_post[layer][None, :])
        x = _ffn(x.reshape(B * S, D), norm_ffn_pre[layer][None, :], norm_ffn_post[layer][None, :],
                 wg_ffn, wu_ffn, wd_ffn).reshape(B, S, D)
    return x
```

```python
import functools

import numpy as np
import jax
import jax.numpy as jnp
from jax import lax
from jax.experimental import pallas as pl
from jax.experimental.pallas import tpu as pltpu

D_MODEL = 1024
HEAD_DIM = 64
DIL_CONFIGS = ((128, 1), (512, 4), (2048, 16))
N_DIL_GROUPS = 3
GROUP_WIDTH = 256
N_FOX_HEADS = 8
FOX_WIDTH = 512
BLOCK = 128
ROPE_THETA = 500000.0
ROPE_DIM = 16
D_FF = 2816
EPS = 1e-6
NEG_INF = -1e30

DIL_WIDTH = N_DIL_GROUPS * GROUP_WIDTH
QKV_COLS = 3 * DIL_WIDTH + 3 * FOX_WIDTH
GATE_COL0 = QKV_COLS + N_FOX_HEADS
F_ROWS = 16
LANES = 128
SUBLANES = 8
MXU_COLS = 256
VMEM_LIMIT = 56 * 1024 * 1024
VT_ROWS = HEAD_DIM + 16
LOG2E = 1.4426950408889634

F32 = jnp.float32
BF16 = jnp.bfloat16
_NT = (((1,), (1,)), ((), ()))


def _rmsnorm(x, gain):
    return x * lax.rsqrt(jnp.mean(x * x, axis=-1, keepdims=True) + EPS) * gain


def _log_sigmoid(z):
    return jnp.minimum(z, 0.0) - jnp.log1p(jnp.exp(-jnp.abs(z)))


def _stage_groups(d):
    return max(d // SUBLANES, 1)


def _stage_shape(d, tm):
    g = _stage_groups(d)
    return (GROUP_WIDTH // LANES, g, tm // g, LANES)


def _stage_read_tokens(st, c, d):
    g = _stage_groups(d)
    rows = st.shape[2]
    parts = [st[c, j].reshape(rows // SUBLANES, SUBLANES, LANES) for j in range(g)]
    return jnp.stack(parts, axis=1).reshape(rows * g, LANES)


def _stage_residue_index(c, r, d, tm):
    stride = d // _stage_groups(d)
    return (c, r // stride, pl.ds(r % stride, tm // d, stride=stride), slice(None))


def _prep_w_in_kernel(wt_ref, qkv_ref, gates_ref, wf_ref, *, q_scale):
    def block(row0):
        return wt_ref[row0:row0 + LANES, :].T

    for c in range(QKV_COLS // LANES):
        is_q = c * LANES < DIL_WIDTH or 3 * DIL_WIDTH <= c * LANES < 3 * DIL_WIDTH + FOX_WIDTH
        blk = block(c * LANES)
        qkv_ref[:, c * LANES:(c + 1) * LANES] = (blk * q_scale if is_q else blk).astype(BF16)
    for c in range(2 * D_MODEL // LANES):
        gates_ref[:, c * LANES:(c + 1) * LANES] = block(GATE_COL0 + c * LANES).astype(BF16)
    lane = lax.broadcasted_iota(jnp.int32, (wt_ref.shape[1], LANES), 1)
    wf_ref[...] = jnp.where(lane < N_FOX_HEADS, block(QKV_COLS), 0.0).astype(BF16)


def _prep_w_in(w_in_t, q_scale, kb=256):
    C, D = w_in_t.shape
    return pl.pallas_call(
        functools.partial(_prep_w_in_kernel, q_scale=q_scale),
        grid=(D // kb,),
        in_specs=[pl.BlockSpec((C, kb), lambda i: (0, i))],
        out_specs=[pl.BlockSpec((kb, QKV_COLS), lambda i: (i, 0)),
                   pl.BlockSpec((kb, 2 * D), lambda i: (i, 0)),
                   pl.BlockSpec((kb, LANES), lambda i: (i, 0))],
        out_shape=[jax.ShapeDtypeStruct((D, QKV_COLS), BF16),
                   jax.ShapeDtypeStruct((D, 2 * D), BF16),
                   jax.ShapeDtypeStruct((D, LANES), BF16)],
        compiler_params=pltpu.CompilerParams(dimension_semantics=("arbitrary",)),
        name="prep_w_in",
    )(w_in_t)


def _in_proj_kernel(*refs, tm, n_side):
    (x_ref, gain_ref, w_ref, wf_ref, bf_ref, cos_ref, sa_ref, sb_ref), refs = refs[:8], refs[8:]
    side_in, refs = refs[:n_side], refs[n_side:]
    (qkv0, qkv1, qkv2, qkvb_ref, ft_ref), refs = refs[:5], refs[5:]
    side_out, refs = refs[:n_side], refs[n_side:]
    (carry_ref,) = refs
    i = pl.program_id(1)
    h = _rmsnorm(x_ref[0], gain_ref[...]).astype(BF16)

    z = jnp.dot(h, wf_ref[...], preferred_element_type=F32).T[:F_ROWS] + bf_ref[...]
    logf = _log_sigmoid(z)
    lane = lax.broadcasted_iota(jnp.int32, (F_ROWS, LANES), 1)
    sums = []
    for j in range(tm // LANES):
        c = logf[:, j * LANES:(j + 1) * LANES]
        k = 1
        while k < LANES:
            c = c + jnp.where(lane >= k, pltpu.roll(c, k, 1), 0.0)
            k *= 2
        sums.append(c)
    carry = jnp.where(i == 0, 0.0, carry_ref[...])
    for j, c in enumerate(sums):
        c = c + carry
        ft_ref[0, :, j * LANES:(j + 1) * LANES] = c
        carry = jnp.broadcast_to(c[:, LANES - 1:LANES], (F_ROWS, LANES))
    carry_ref[...] = carry

    def proj(c0, width=MXU_COLS):
        return jnp.dot(h, w_ref[:, c0:c0 + width], preferred_element_type=F32)

    cos, sa, sb = cos_ref[...], sa_ref[...], sb_ref[...]

    def rope(y):
        halves = []
        for c in range(MXU_COLS // LANES):
            yc = y[:, c * LANES:(c + 1) * LANES]
            halves.append(yc * cos + pltpu.roll(yc, LANES - ROPE_DIM // 2, 1) * sa
                          + pltpu.roll(yc, ROPE_DIM // 2, 1) * sb)
        return jnp.concatenate(halves, axis=1)

    def put(o_ref, col0, y, d):
        if d == 1:
            o_ref[0, 0, :, col0:col0 + GROUP_WIDTH] = y.astype(BF16)
            return
        yt = jnp.swapaxes(y.reshape(tm // d, d, GROUP_WIDTH), 0, 1)
        for r in range(d):
            o_ref[0, r, :, col0:col0 + GROUP_WIDTH] = yt[r].astype(BF16)

    for g, o_ref in enumerate((qkv0, qkv1, qkv2)):
        d = DIL_CONFIGS[g][1]
        put(o_ref, 0, rope(proj(g * GROUP_WIDTH)), d)
        put(o_ref, GROUP_WIDTH, rope(proj(DIL_WIDTH + g * GROUP_WIDTH)), d)
        put(o_ref, 2 * GROUP_WIDTH, proj(2 * DIL_WIDTH + g * GROUP_WIDTH), d)
    base = 3 * DIL_WIDTH
    for c in range(3 * FOX_WIDTH // MXU_COLS):
        qkvb_ref[0, :, c * MXU_COLS:(c + 1) * MXU_COLS] = proj(base + c * MXU_COLS).astype(BF16)
    for src, dst in zip(side_in, side_out):
        dst[...] = src[...].astype(BF16)


def _rope_tables(seq):
    half = ROPE_DIM // 2
    inv_freq = np.power(ROPE_THETA, -np.arange(0, ROPE_DIM, 2, dtype=np.float64) / ROPE_DIM)
    ang = np.arange(seq, dtype=np.float64)[:, None] * inv_freq[None, :]
    cos = np.ones((seq, HEAD_DIM)); sa = np.zeros((seq, HEAD_DIM)); sb = np.zeros((seq, HEAD_DIM))
    cos[:, :half] = np.cos(ang); cos[:, half:ROPE_DIM] = np.cos(ang)
    sa[:, :half] = -np.sin(ang)
    sb[:, half:ROPE_DIM] = np.sin(ang)
    rep = LANES // HEAD_DIM
    return tuple(jnp.asarray(np.tile(t, (1, rep)), dtype=F32) for t in (cos, sa, sb))


def _in_proj(x, gain, w_main, wf, bf, side_weights, tm=1024):
    B, S, D = x.shape
    n_i = S // tm
    n_steps = B * n_i
    side_specs = [pl.BlockSpec((w.shape[0] // n_steps, w.shape[1]), lambda b, i: (b * n_i + i, 0))
                  for w in side_weights]
    side_shapes = [jax.ShapeDtypeStruct(w.shape, BF16) for w in side_weights]
    cos, sa, sb = _rope_tables(S)
    const = lambda shape: pl.BlockSpec(shape, lambda b, i: (0,) * len(shape),
                                       pipeline_mode=pl.Buffered(1))
    row = lambda w: pl.BlockSpec((1, tm, w), lambda b, i: (b, i, 0))
    tab = pl.BlockSpec((tm, LANES), lambda b, i: (i, 0))
    dils = [d for _, d in DIL_CONFIGS]
    res_shape = [jax.ShapeDtypeStruct((B, d, S // d, 3 * GROUP_WIDTH), BF16) for d in dils]
    res_spec = [pl.BlockSpec((1, d, tm // d, 3 * GROUP_WIDTH), lambda b, i: (b, 0, i, 0))
                for d in dils]
    out_shape = (res_shape
                 + [jax.ShapeDtypeStruct((B, S, 3 * FOX_WIDTH), BF16)]
                 + [jax.ShapeDtypeStruct((B, F_ROWS, S), F32)])
    out_specs = (res_spec + [row(3 * FOX_WIDTH)]
                 + [pl.BlockSpec((1, F_ROWS, tm), lambda b, i: (b, 0, i))])
    return pl.pallas_call(
        functools.partial(_in_proj_kernel, tm=tm, n_side=len(side_weights)),
        grid=(B, n_i),
        in_specs=[row(D), const((1, D)), const((D, QKV_COLS)), const((D, LANES)),
                  const((F_ROWS, 1)), tab, tab, tab] + side_specs,
        out_specs=out_specs + side_specs,
        out_shape=out_shape + side_shapes,
        scratch_shapes=[pltpu.VMEM((F_ROWS, LANES), F32)],
        compiler_params=pltpu.CompilerParams(
            dimension_semantics=("arbitrary", "arbitrary"), vmem_limit_bytes=VMEM_LIMIT),
        name="in_proj",
    )(x, gain, w_main, wf, bf, cos, sa, sb, *side_weights)


def _dilated_kernel(qkv_ref, o_ref, lse_ref, vt_ref, *, d, nb, lookahead):
    lane = lax.broadcasted_iota(jnp.int32, (BLOCK, LANES), 1)
    lo_half = lane < HEAD_DIM
    kw = 2 * BLOCK
    n_pairs = GROUP_WIDTH // LANES

    kj = lax.broadcasted_iota(jnp.int32, (kw, kw), 0)
    qi = lax.broadcasted_iota(jnp.int32, (kw, kw), 1) % BLOCK
    kj1 = lax.broadcasted_iota(jnp.int32, (BLOCK, kw), 0)
    qi1 = lax.broadcasted_iota(jnp.int32, (BLOCK, kw), 1) % BLOCK
    bias_lead = jnp.where(kj1 <= qi1, 0.0, NEG_INF)
    bias_band = jnp.where((kj >= qi) & (kj <= qi + BLOCK), 0.0, NEG_INF)

    for r in range(d):
        for hp in range(n_pairs):
            cols = slice(hp * LANES, (hp + 1) * LANES)
            for n in range(nb):
                rows = slice(n * BLOCK, (n + 1) * BLOCK)
                vt = qkv_ref[0, r, rows, 2 * GROUP_WIDTH + hp * LANES:
                             2 * GROUP_WIDTH + (hp + 1) * LANES].astype(F32).T.astype(BF16)
                vt_ref[r, hp, 0:HEAD_DIM, rows] = vt[:HEAD_DIM]
                vt_ref[r, hp, VT_ROWS:VT_ROWS + HEAD_DIM, rows] = vt[HEAD_DIM:]
            ones = jnp.ones((VT_ROWS - HEAD_DIM, nb * BLOCK), BF16)
            vt_ref[r, hp, HEAD_DIM:VT_ROWS, :] = ones
            vt_ref[r, hp, VT_ROWS + HEAD_DIM:, :] = ones

    def scores(r, n, hp):
        cols = slice(hp * LANES, (hp + 1) * LANES)
        q = qkv_ref[0, r, n * BLOCK:(n + 1) * BLOCK, cols]
        zero = jnp.zeros_like(q)
        qm = jnp.concatenate([jnp.where(lo_half, q, zero), jnp.where(lo_half, zero, q)], axis=0)
        keys = slice(0, BLOCK) if n == 0 else slice((n - 1) * BLOCK, (n + 1) * BLOCK)
        k_cols = slice(GROUP_WIDTH + hp * LANES, GROUP_WIDTH + (hp + 1) * LANES)
        st = lax.dot_general(qkv_ref[0, r, keys, k_cols], qm, _NT,
                             preferred_element_type=F32)
        return st + (bias_lead if n == 0 else bias_band)

    def finish(r, n, hp, st):
        cols = slice(hp * LANES, (hp + 1) * LANES)
        rows = slice(n * BLOCK, (n + 1) * BLOCK)
        keys = slice(0, BLOCK) if n == 0 else slice((n - 1) * BLOCK, (n + 1) * BLOCK)
        vt = vt_ref[r, hp, :, keys]
        m = jnp.max(st, axis=0, keepdims=True)
        pe = jnp.exp2(st - m).astype(BF16)
        acc = jnp.dot(vt, pe, preferred_element_type=F32)
        out_t, lse_t = [], []
        for hh in range(2):
            a = acc[hh * VT_ROWS:(hh + 1) * VT_ROWS, hh * BLOCK:(hh + 1) * BLOCK]
            den = a[HEAD_DIM:HEAD_DIM + 1]
            out_t.append(a[:HEAD_DIM] * (1.0 / den))
            lse = m[:, hh * BLOCK:(hh + 1) * BLOCK] + jnp.log2(den)
            lse_t.append(jnp.broadcast_to(lse, (HEAD_DIM, BLOCK)))
        o_ref[0, r, rows, cols] = jnp.concatenate(out_t, axis=0).T.astype(BF16)
        lse_ref[0, r, rows, cols] = jnp.concatenate(lse_t, axis=0).T

    units = [(r, n, hp) for r in range(d) for n in range(nb) for hp in range(n_pairs)]
    ready = {}
    for idx in range(len(units) + lookahead):
        if idx < len(units):
            ready[idx] = scores(*units[idx])
        if idx >= lookahead:
            finish(*units[idx - lookahead], ready.pop(idx - lookahead))


def _dilated_group(qkv, lookahead=4):
    B, d, L, _ = qkv.shape
    W = GROUP_WIDTH
    spec = pl.BlockSpec((1, d, L, W), lambda b: (b, 0, 0, 0))
    return pl.pallas_call(
        functools.partial(_dilated_kernel, d=d, nb=L // BLOCK, lookahead=lookahead),
        grid=(B,),
        in_specs=[pl.BlockSpec((1, d, L, 3 * W), lambda b: (b, 0, 0, 0))],
        out_specs=[spec, spec],
        out_shape=[jax.ShapeDtypeStruct((B, d, L, W), BF16),
                   jax.ShapeDtypeStruct((B, d, L, W), F32)],
        scratch_shapes=[pltpu.VMEM((d, W // LANES, 2 * VT_ROWS, L), BF16)],
        compiler_params=pltpu.CompilerParams(
            dimension_semantics=("arbitrary",), vmem_limit_bytes=VMEM_LIMIT),
        name=f"dilated_d{d}",
    )(qkv)


def _fox_kernel(q_ref, k_ref, v_ref, ft_ref, _after_ref, o_ref, vt_ref, nfcol_ref, *, tq, nblk,
                lookahead, pairs):
    lane = lax.broadcasted_iota(jnp.int32, (tq, LANES), 1)
    lo_half = lane < HEAD_DIM
    key = lax.broadcasted_iota(jnp.int32, (tq, tq), 0)
    qry = lax.broadcasted_iota(jnp.int32, (tq, tq), 1)
    causal = key <= qry

    for ph in range(pairs):
        _fox_pair(q_ref, k_ref, v_ref, ft_ref, o_ref, vt_ref, nfcol_ref, ph,
                  pl.program_id(1) * pairs + ph, lo_half, causal, tq, nblk, lookahead)


def _fox_pair(q_ref, k_ref, v_ref, ft_ref, o_ref, vt_ref, nfcol_ref, ph, pair, lo_half, causal,
              tq, nblk, lookahead):
    seq = nblk * tq
    ls = slice(ph * LANES, (ph + 1) * LANES)
    for c in range(seq // tq):
        cs = slice(c * tq, (c + 1) * tq)
        vt = v_ref[0, cs, ls].astype(F32).T.astype(BF16)
        for hh in range(2):
            vt_ref[ph, hh, :HEAD_DIM, cs] = vt[hh * HEAD_DIM:(hh + 1) * HEAD_DIM]
    for hh in range(2):
        vt_ref[ph, hh, HEAD_DIM:, :] = jnp.ones((VT_ROWS - HEAD_DIM, seq), BF16)
    for hh in range(2):
        nfrow = ft_ref[0, pl.ds(2 * pair + hh, 1), :] * (-LOG2E)
        for c in range(seq // LANES):
            cs = slice(c * LANES, (c + 1) * LANES)
            nfcol_ref[ph, hh, cs, :] = jnp.broadcast_to(nfrow[:, cs], (LANES, LANES)).T

    def scores(t, i, hh):
        j = i - t
        ks = slice(j * tq, (j + 1) * tq)
        q = q_ref[0, i * tq:(i + 1) * tq, ls]
        qm = jnp.where(lo_half if hh == 0 else ~lo_half, q, jnp.zeros_like(q))
        st = lax.dot_general(k_ref[0, ks, ls], qm, _NT, preferred_element_type=F32)
        st = st + jnp.concatenate([nfcol_ref[ph, hh, ks, :]] * (tq // LANES), axis=1)
        return jnp.where(causal, st, NEG_INF) if t == 0 else st

    state = {}

    def update(t, i, hh, st):
        j = i - t
        vt = vt_ref[ph, hh, :, j * tq:(j + 1) * tq]
        if t == 0:
            m = jnp.max(st, axis=0, keepdims=True)
            pe = jnp.exp2(st - m)
            acc = jnp.dot(vt, pe.astype(BF16), preferred_element_type=F32)
        else:
            m_old, acc_old = state[i, hh]
            m = jnp.maximum(m_old, jnp.max(st, axis=0, keepdims=True))
            pe = jnp.exp2(st - m)
            acc = jnp.exp2(m_old - m) * acc_old + jnp.dot(vt, pe.astype(BF16),
                                                          preferred_element_type=F32)
        state[i, hh] = (m, acc)
        if j == 0 and hh == 1:
            out_t = jnp.concatenate(
                [state[i, h][1][:HEAD_DIM] * (1.0 / state[i, h][1][HEAD_DIM:HEAD_DIM + 1])
                 for h in range(2)], axis=0)
            o_ref[0, i * tq:(i + 1) * tq, ls] = out_t.T.astype(BF16)

    units = [(t, i, hh) for t in range(nblk) for i in range(t, nblk) for hh in range(2)]
    ready = {}
    for n in range(len(units) + lookahead):
        if n < len(units):
            ready[n] = scores(*units[n])
        if n >= lookahead:
            update(*units[n - lookahead], ready.pop(n - lookahead))


def _fox(qkv, ft, after, tq=256, lookahead=5, pairs=2):
    B, S, _ = qkv.shape
    W = FOX_WIDTH
    wb = pairs * LANES
    n_steps = W // wb
    spec = pl.BlockSpec((1, S, wb), lambda b, p: (b, 0, p))
    part = lambda s: pl.BlockSpec((1, S, wb), lambda b, p: (b, 0, s * n_steps + p))
    return pl.pallas_call(
        functools.partial(_fox_kernel, tq=tq, nblk=S // tq, lookahead=lookahead, pairs=pairs),
        grid=(B, n_steps),
        in_specs=[part(0), part(1), part(2),
                  pl.BlockSpec((1, F_ROWS, S), lambda b, p: (b, 0, 0)),
                  pl.BlockSpec(memory_space=pl.ANY)],
        out_specs=spec,
        out_shape=jax.ShapeDtypeStruct((B, S, W), BF16),
        scratch_shapes=[pltpu.VMEM((pairs, 2, VT_ROWS, S), BF16),
                        pltpu.VMEM((pairs, 2, S, LANES), F32)],
        compiler_params=pltpu.CompilerParams(
            dimension_semantics=("arbitrary", "arbitrary"), vmem_limit_bytes=VMEM_LIMIT),
        name="fox",
    )(qkv, qkv, qkv, ft, after)


def _mix_kernel(x_ref, o0, o1, o2, l0, l1, l2, ob_ref, wg_ref, wa_ref, wb_ref, wo_ref,
                gpre_ref, gpost_ref, y_ref, *stage_refs, tm):
    stage = iter(stage_refs)
    x = x_ref[0]
    h = _rmsnorm(x, gpre_ref[...]).astype(BF16)

    def tokens(ref):
        d = ref.shape[1]
        if d == 1:
            return ref[0, 0].astype(F32)
        st = next(stage)
        for r in range(d):
            for c in range(GROUP_WIDTH // LANES):
                st[_stage_residue_index(c, r, d, tm)] = ref[
                    0, r, :, c * LANES:(c + 1) * LANES].astype(F32)
        return jnp.concatenate(
            [_stage_read_tokens(st, c, d) for c in range(GROUP_WIDTH // LANES)], axis=1)

    la, lb, lc = tokens(l0), tokens(l1), tokens(l2)
    mx = jnp.maximum(jnp.maximum(la, lb), lc)
    e0, e1, e2 = jnp.exp2(la - mx), jnp.exp2(lb - mx), jnp.exp2(lc - mx)
    inv = 1.0 / (e0 + e1 + e2)
    oa = ((e0 * tokens(o0) + e1 * tokens(o1) + e2 * tokens(o2)) * inv).astype(BF16)
    ob = ob_ref[0]
    merged = []
    for c in range(D_MODEL // MXU_COLS):
        cols = slice(c * MXU_COLS, (c + 1) * MXU_COLS)
        ga = jnp.dot(h, wg_ref[:, cols], preferred_element_type=F32)
        gb = jnp.dot(h, wg_ref[:, D_MODEL + c * MXU_COLS:D_MODEL + (c + 1) * MXU_COLS],
                     preferred_element_type=F32)
        a = jnp.dot(oa, wa_ref[:, cols], preferred_element_type=F32)
        b = jnp.dot(ob, wb_ref[:, cols], preferred_element_type=F32)
        merged.append((jax.nn.sigmoid(ga) * a + jax.nn.sigmoid(gb) * b).astype(BF16))
    mix = jnp.dot(jnp.concatenate(merged, axis=1), wo_ref[...], preferred_element_type=F32)
    y_ref[0] = x + _rmsnorm(mix, gpost_ref[...])


def _mix(x, outs, lses, ob, wg, wa, wb, wo, gain_pre, gain_post, tm=1024):
    B, S, D = x.shape
    row = lambda w: pl.BlockSpec((1, tm, w), lambda b, i: (b, i, 0))
    const = lambda shape: pl.BlockSpec(shape, lambda b, i: (0, 0), pipeline_mode=pl.Buffered(1))
    res = lambda t: pl.BlockSpec((1, t.shape[1], tm // t.shape[1], GROUP_WIDTH),
                                 lambda b, i: (b, 0, i, 0))
    stages = [pltpu.VMEM(_stage_shape(t.shape[1], tm), F32)
              for group in (lses, outs) for t in group if t.shape[1] > 1]
    return pl.pallas_call(
        functools.partial(_mix_kernel, tm=tm),
        grid=(B, S // tm),
        in_specs=([row(D)] + [res(t) for t in outs] + [res(t) for t in lses]
                  + [row(FOX_WIDTH), const((D, 2 * D)),
                     const((GROUP_WIDTH, D)), const((FOX_WIDTH, D)), const((D, D)),
                     const((1, D)), const((1, D))]),
        out_specs=row(D),
        out_shape=jax.ShapeDtypeStruct((B, S, D), F32),
        scratch_shapes=stages,
        compiler_params=pltpu.CompilerParams(
            dimension_semantics=("arbitrary", "arbitrary"), vmem_limit_bytes=VMEM_LIMIT),
        name="mix",
    )(x, *outs, *lses, ob, wg, wa, wb, wo, gain_pre, gain_post)


def _ffn_kernel(x_ref, g1_ref, g2_ref, wg_ref, wu_ref, wd_ref, o_ref):
    x = x_ref[...]
    h = _rmsnorm(x, g1_ref[...]).astype(BF16)
    acc = jnp.zeros(x.shape, F32)
    for c in range(D_FF // MXU_COLS):
        cols = slice(c * MXU_COLS, (c + 1) * MXU_COLS)
        g = jnp.dot(h, wg_ref[:, cols], preferred_element_type=F32)
        u = jnp.dot(h, wu_ref[:, cols], preferred_element_type=F32)
        a = (g * jax.nn.sigmoid(g) * u).astype(BF16)
        acc = acc + jnp.dot(a, wd_ref[cols, :], preferred_element_type=F32)
    o_ref[...] = x + _rmsnorm(acc, g2_ref[...])


def _ffn(x2, g1, g2, wg, wu, wd, tm=1024):
    T, D = x2.shape
    row = pl.BlockSpec((tm, D), lambda i: (i, 0))
    const = lambda shape: pl.BlockSpec(shape, lambda i: (0, 0), pipeline_mode=pl.Buffered(1))
    return pl.pallas_call(
        _ffn_kernel,
        grid=(T // tm,),
        in_specs=[row, const((1, D)), const((1, D)), const((D, D_FF)), const((D, D_FF)),
                  const((D_FF, D))],
        out_specs=row,
        out_shape=jax.ShapeDtypeStruct((T, D), F32),
        compiler_params=pltpu.CompilerParams(
            dimension_semantics=("arbitrary",), vmem_limit_bytes=VMEM_LIMIT),
        name="ffn",
    )(x2, g1, g2, wg, wu, wd)


def kernel(x, w_in, w_proj_a, w_proj_b, w_out, b_forget, w_ffn_gate, w_ffn_up, w_ffn_down,
           norm_mix_pre, norm_mix_post, norm_ffn_pre, norm_ffn_post):
    B, S, D = x.shape
    q_scale = float(LOG2E / np.sqrt(HEAD_DIM))
    for layer in range(w_in.shape[0]):
        w_qkv, w_gates, wf = _prep_w_in(w_in[layer].T, q_scale)
        bf = jnp.zeros((F_ROWS, 1), F32).at[:N_FOX_HEADS, 0].set(b_forget[layer])

        (qkv0, qkv1, qkv2, qkvb, ft, wa, wb, wo, wg_ffn, wu_ffn, wd_ffn) = _in_proj(
            x, norm_mix_pre[layer][None, :], w_qkv, wf, bf,
            [w_proj_a[layer], w_proj_b[layer], w_out[layer],
             w_ffn_gate[layer], w_ffn_up[layer], w_ffn_down[layer]])

        outs, lses = [], []
        for qkv in (qkv0, qkv1, qkv2):
            o_g, l_g = _dilated_group(qkv)
            outs.append(o_g)
            lses.append(l_g)
        ob = _fox(qkvb, ft, after=outs[-1])

        x = _mix(x, outs, lses, ob, w_gates, wa, wb, wo, norm_mix_pre[layer][None, :],
                 norm_mix_post[layer][None, :])
        x = _ffn(x.reshape(B * S, D), norm_ffn_pre[layer][None, :], norm_ffn_post[layer][None, :],
                 wg_ffn, wu_ffn, wd_ffn).reshape(B, S, D)
    return x
```

```python
import functools

import numpy as np
import jax
import jax.numpy as jnp
from jax import lax
from jax.experimental import pallas as pl
from jax.experimental.pallas import tpu as pltpu

D_MODEL = 1024
HEAD_DIM = 64
DIL_CONFIGS = ((128, 1), (512, 4), (2048, 16))
N_DIL_GROUPS = 3
GROUP_WIDTH = 256
N_FOX_HEADS = 8
FOX_WIDTH = 512
BLOCK = 128
ROPE_THETA = 500000.0
ROPE_DIM = 16
D_FF = 2816
EPS = 1e-6
NEG_INF = -1e30

DIL_WIDTH = N_DIL_GROUPS * GROUP_WIDTH
QKV_COLS = 3 * DIL_WIDTH + 3 * FOX_WIDTH
GATE_COL0 = QKV_COLS + N_FOX_HEADS
F_ROWS = 16
LANES = 128
SUBLANES = 8
MXU_COLS = 256
VMEM_LIMIT = 56 * 1024 * 1024
VT_ROWS = HEAD_DIM + 16
LOG2E = 1.4426950408889634

F32 = jnp.float32
BF16 = jnp.bfloat16
_NT = (((1,), (1,)), ((), ()))


def _rmsnorm(x, gain):
    return x * lax.rsqrt(jnp.mean(x * x, axis=-1, keepdims=True) + EPS) * gain


def _log_sigmoid(z):
    return jnp.minimum(z, 0.0) - jnp.log1p(jnp.exp(-jnp.abs(z)))


def _stage_groups(d):
    return max(d // SUBLANES, 1)


def _stage_shape(d, tm):
    g = _stage_groups(d)
    return (GROUP_WIDTH // LANES, g, tm // g, LANES)


def _stage_read_tokens(st, c, d):
    g = _stage_groups(d)
    rows = st.shape[2]
    parts = [st[c, j].reshape(rows // SUBLANES, SUBLANES, LANES) for j in range(g)]
    return jnp.stack(parts, axis=1).reshape(rows * g, LANES)


def _stage_residue_index(c, r, d, tm):
    stride = d // _stage_groups(d)
    return (c, r // stride, pl.ds(r % stride, tm // d, stride=stride), slice(None))


def _prep_w_in_kernel(wt_ref, qkv_ref, gates_ref, wf_ref, *, q_scale):
    def block(row0):
        return wt_ref[row0:row0 + LANES, :].T

    for c in range(QKV_COLS // LANES):
        is_q = c * LANES < DIL_WIDTH or 3 * DIL_WIDTH <= c * LANES < 3 * DIL_WIDTH + FOX_WIDTH
        blk = block(c * LANES)
        qkv_ref[:, c * LANES:(c + 1) * LANES] = (blk * q_scale if is_q else blk).astype(BF16)
    for c in range(2 * D_MODEL // LANES):
        gates_ref[:, c * LANES:(c + 1) * LANES] = block(GATE_COL0 + c * LANES).astype(BF16)
    lane = lax.broadcasted_iota(jnp.int32, (wt_ref.shape[1], LANES), 1)
    wf_ref[...] = jnp.where(lane < N_FOX_HEADS, block(QKV_COLS), 0.0).astype(BF16)


def _prep_w_in(w_in_t, q_scale, kb=512):
    C, D = w_in_t.shape
    return pl.pallas_call(
        functools.partial(_prep_w_in_kernel, q_scale=q_scale),
        grid=(D // kb,),
        in_specs=[pl.BlockSpec((C, kb), lambda i: (0, i))],
        out_specs=[pl.BlockSpec((kb, QKV_COLS), lambda i: (i, 0)),
                   pl.BlockSpec((kb, 2 * D), lambda i: (i, 0)),
                   pl.BlockSpec((kb, LANES), lambda i: (i, 0))],
        out_shape=[jax.ShapeDtypeStruct((D, QKV_COLS), BF16),
                   jax.ShapeDtypeStruct((D, 2 * D), BF16),
                   jax.ShapeDtypeStruct((D, LANES), BF16)],
        compiler_params=pltpu.CompilerParams(
            dimension_semantics=("arbitrary",), vmem_limit_bytes=VMEM_LIMIT),
        name="prep_w_in",
    )(w_in_t)


def _in_proj_kernel(*refs, tm, n_side):
    (x_ref, gain_ref, w_ref, wf_ref, bf_ref, cos_ref, sa_ref, sb_ref), refs = refs[:8], refs[8:]
    side_in, refs = refs[:n_side], refs[n_side:]
    (qkv0, qkv1, qkv2, qkvb_ref, ft_ref), refs = refs[:5], refs[5:]
    side_out, refs = refs[:n_side], refs[n_side:]
    (carry_ref,) = refs
    i = pl.program_id(1)
    h = _rmsnorm(x_ref[0], gain_ref[...]).astype(BF16)

    z = jnp.dot(h, wf_ref[...], preferred_element_type=F32).T[:F_ROWS] + bf_ref[...]
    logf = _log_sigmoid(z)
    lane = lax.broadcasted_iota(jnp.int32, (F_ROWS, LANES), 1)
    sums = []
    for j in range(tm // LANES):
        c = logf[:, j * LANES:(j + 1) * LANES]
        k = 1
        while k < LANES:
            c = c + jnp.where(lane >= k, pltpu.roll(c, k, 1), 0.0)
            k *= 2
        sums.append(c)
    carry = jnp.where(i == 0, 0.0, carry_ref[...])
    for j, c in enumerate(sums):
        c = c + carry
        ft_ref[0, :, j * LANES:(j + 1) * LANES] = c
        carry = jnp.broadcast_to(c[:, LANES - 1:LANES], (F_ROWS, LANES))
    carry_ref[...] = carry

    def proj(c0, width=MXU_COLS):
        return jnp.dot(h, w_ref[:, c0:c0 + width], preferred_element_type=F32)

    cos, sa, sb = cos_ref[...], sa_ref[...], sb_ref[...]

    def rope(y):
        halves = []
        for c in range(MXU_COLS // LANES):
            yc = y[:, c * LANES:(c + 1) * LANES]
            halves.append(yc * cos + pltpu.roll(yc, LANES - ROPE_DIM // 2, 1) * sa
                          + pltpu.roll(yc, ROPE_DIM // 2, 1) * sb)
        return jnp.concatenate(halves, axis=1)

    def put(o_ref, col0, y, d):
        if d == 1:
            o_ref[0, 0, :, col0:col0 + GROUP_WIDTH] = y.astype(BF16)
            return
        yt = jnp.swapaxes(y.reshape(tm // d, d, GROUP_WIDTH), 0, 1)
        for r in range(d):
            o_ref[0, r, :, col0:col0 + GROUP_WIDTH] = yt[r].astype(BF16)

    for g, o_ref in enumerate((qkv0, qkv1, qkv2)):
        d = DIL_CONFIGS[g][1]
        put(o_ref, 0, rope(proj(g * GROUP_WIDTH)), d)
        put(o_ref, GROUP_WIDTH, rope(proj(DIL_WIDTH + g * GROUP_WIDTH)), d)
        put(o_ref, 2 * GROUP_WIDTH, proj(2 * DIL_WIDTH + g * GROUP_WIDTH), d)
    base = 3 * DIL_WIDTH
    for c in range(3 * FOX_WIDTH // MXU_COLS):
        qkvb_ref[0, :, c * MXU_COLS:(c + 1) * MXU_COLS] = proj(base + c * MXU_COLS).astype(BF16)
    for src, dst in zip(side_in, side_out):
        dst[...] = src[...].astype(BF16)


def _rope_tables(seq):
    half = ROPE_DIM // 2
    inv_freq = np.power(ROPE_THETA, -np.arange(0, ROPE_DIM, 2, dtype=np.float64) / ROPE_DIM)
    ang = np.arange(seq, dtype=np.float64)[:, None] * inv_freq[None, :]
    cos = np.ones((seq, HEAD_DIM)); sa = np.zeros((seq, HEAD_DIM)); sb = np.zeros((seq, HEAD_DIM))
    cos[:, :half] = np.cos(ang); cos[:, half:ROPE_DIM] = np.cos(ang)
    sa[:, :half] = -np.sin(ang)
    sb[:, half:ROPE_DIM] = np.sin(ang)
    rep = LANES // HEAD_DIM
    return tuple(jnp.asarray(np.tile(t, (1, rep)), dtype=F32) for t in (cos, sa, sb))


def _in_proj(x, gain, w_main, wf, bf, side_weights, tm=1024):
    B, S, D = x.shape
    n_i = S // tm
    n_steps = B * n_i
    side_specs = [pl.BlockSpec((w.shape[0] // n_steps, w.shape[1]), lambda b, i: (b * n_i + i, 0))
                  for w in side_weights]
    side_shapes = [jax.ShapeDtypeStruct(w.shape, BF16) for w in side_weights]
    cos, sa, sb = _rope_tables(S)
    const = lambda shape: pl.BlockSpec(shape, lambda b, i: (0,) * len(shape),
                                       pipeline_mode=pl.Buffered(1))
    row = lambda w: pl.BlockSpec((1, tm, w), lambda b, i: (b, i, 0))
    tab = pl.BlockSpec((tm, LANES), lambda b, i: (i, 0))
    dils = [d for _, d in DIL_CONFIGS]
    res_shape = [jax.ShapeDtypeStruct((B, d, S // d, 3 * GROUP_WIDTH), BF16) for d in dils]
    res_spec = [pl.BlockSpec((1, d, tm // d, 3 * GROUP_WIDTH), lambda b, i: (b, 0, i, 0))
                for d in dils]
    out_shape = (res_shape
                 + [jax.ShapeDtypeStruct((B, S, 3 * FOX_WIDTH), BF16)]
                 + [jax.ShapeDtypeStruct((B, F_ROWS, S), F32)])
    out_specs = (res_spec + [row(3 * FOX_WIDTH)]
                 + [pl.BlockSpec((1, F_ROWS, tm), lambda b, i: (b, 0, i))])
    return pl.pallas_call(
        functools.partial(_in_proj_kernel, tm=tm, n_side=len(side_weights)),
        grid=(B, n_i),
        in_specs=[row(D), const((1, D)), const((D, QKV_COLS)), const((D, LANES)),
                  const((F_ROWS, 1)), tab, tab, tab] + side_specs,
        out_specs=out_specs + side_specs,
        out_shape=out_shape + side_shapes,
        scratch_shapes=[pltpu.VMEM((F_ROWS, LANES), F32)],
        compiler_params=pltpu.CompilerParams(
            dimension_semantics=("arbitrary", "arbitrary"), vmem_limit_bytes=VMEM_LIMIT),
        name="in_proj",
    )(x, gain, w_main, wf, bf, cos, sa, sb, *side_weights)


def _dilated_kernel(qkv_ref, o_ref, lse_ref, vt_ref, *, d, nb, lookahead):
    lane = lax.broadcasted_iota(jnp.int32, (BLOCK, LANES), 1)
    lo_half = lane < HEAD_DIM
    kw = 2 * BLOCK
    n_pairs = GROUP_WIDTH // LANES

    kj = lax.broadcasted_iota(jnp.int32, (kw, kw), 0)
    qi = lax.broadcasted_iota(jnp.int32, (kw, kw), 1) % BLOCK
    kj1 = lax.broadcasted_iota(jnp.int32, (BLOCK, kw), 0)
    qi1 = lax.broadcasted_iota(jnp.int32, (BLOCK, kw), 1) % BLOCK
    bias_lead = jnp.where(kj1 <= qi1, 0.0, NEG_INF)
    bias_band = jnp.where((kj >= qi) & (kj <= qi + BLOCK), 0.0, NEG_INF)

    for r in range(d):
        for hp in range(n_pairs):
            cols = slice(hp * LANES, (hp + 1) * LANES)
            for n in range(nb):
                rows = slice(n * BLOCK, (n + 1) * BLOCK)
                vt = qkv_ref[0, r, rows, 2 * GROUP_WIDTH + hp * LANES:
                             2 * GROUP_WIDTH + (hp + 1) * LANES].astype(F32).T.astype(BF16)
                vt_ref[r, hp, 0:HEAD_DIM, rows] = vt[:HEAD_DIM]
                vt_ref[r, hp, VT_ROWS:VT_ROWS + HEAD_DIM, rows] = vt[HEAD_DIM:]
            ones = jnp.ones((VT_ROWS - HEAD_DIM, nb * BLOCK), BF16)
            vt_ref[r, hp, HEAD_DIM:VT_ROWS, :] = ones
            vt_ref[r, hp, VT_ROWS + HEAD_DIM:, :] = ones

    def scores(r, n, hp):
        cols = slice(hp * LANES, (hp + 1) * LANES)
        q = qkv_ref[0, r, n * BLOCK:(n + 1) * BLOCK, cols]
        zero = jnp.zeros_like(q)
        qm = jnp.concatenate([jnp.where(lo_half, q, zero), jnp.where(lo_half, zero, q)], axis=0)
        keys = slice(0, BLOCK) if n == 0 else slice((n - 1) * BLOCK, (n + 1) * BLOCK)
        k_cols = slice(GROUP_WIDTH + hp * LANES, GROUP_WIDTH + (hp + 1) * LANES)
        st = lax.dot_general(qkv_ref[0, r, keys, k_cols], qm, _NT,
                             preferred_element_type=F32)
        return st + (bias_lead if n == 0 else bias_band)

    def finish(r, n, hp, st):
        cols = slice(hp * LANES, (hp + 1) * LANES)
        rows = slice(n * BLOCK, (n + 1) * BLOCK)
        keys = slice(0, BLOCK) if n == 0 else slice((n - 1) * BLOCK, (n + 1) * BLOCK)
        vt = vt_ref[r, hp, :, keys]
        m = jnp.max(st, axis=0, keepdims=True)
        pe = jnp.exp2(st - m).astype(BF16)
        acc = jnp.dot(vt, pe, preferred_element_type=F32)
        out_t, lse_t = [], []
        for hh in range(2):
            a = acc[hh * VT_ROWS:(hh + 1) * VT_ROWS, hh * BLOCK:(hh + 1) * BLOCK]
            den = a[HEAD_DIM:HEAD_DIM + 1]
            out_t.append(a[:HEAD_DIM] * (1.0 / den))
            lse = m[:, hh * BLOCK:(hh + 1) * BLOCK] + jnp.log2(den)
            lse_t.append(jnp.broadcast_to(lse, (HEAD_DIM, BLOCK)))
        o_ref[0, r, rows, cols] = jnp.concatenate(out_t, axis=0).T.astype(BF16)
        lse_ref[0, r, rows, cols] = jnp.concatenate(lse_t, axis=0).T

    units = [(r, n, hp) for r in range(d) for n in range(nb) for hp in range(n_pairs)]
    ready = {}
    for idx in range(len(units) + lookahead):
        if idx < len(units):
            ready[idx] = scores(*units[idx])
        if idx >= lookahead:
            finish(*units[idx - lookahead], ready.pop(idx - lookahead))


def _dilated_group(qkv, lookahead=4):
    B, d, L, _ = qkv.shape
    W = GROUP_WIDTH
    spec = pl.BlockSpec((1, d, L, W), lambda b: (b, 0, 0, 0))
    return pl.pallas_call(
        functools.partial(_dilated_kernel, d=d, nb=L // BLOCK, lookahead=lookahead),
        grid=(B,),
        in_specs=[pl.BlockSpec((1, d, L, 3 * W), lambda b: (b, 0, 0, 0))],
        out_specs=[spec, spec],
        out_shape=[jax.ShapeDtypeStruct((B, d, L, W), BF16),
                   jax.ShapeDtypeStruct((B, d, L, W), F32)],
        scratch_shapes=[pltpu.VMEM((d, W // LANES, 2 * VT_ROWS, L), BF16)],
        compiler_params=pltpu.CompilerParams(
            dimension_semantics=("arbitrary",), vmem_limit_bytes=VMEM_LIMIT),
        name=f"dilated_d{d}",
    )(qkv)


def _fox_kernel(q_ref, k_ref, v_ref, ft_ref, _after_ref, o_ref, vt_ref, nfcol_ref, *, tq, nblk,
                lookahead, pairs):
    lane = lax.broadcasted_iota(jnp.int32, (tq, LANES), 1)
    lo_half = lane < HEAD_DIM
    key = lax.broadcasted_iota(jnp.int32, (tq, tq), 0)
    qry = lax.broadcasted_iota(jnp.int32, (tq, tq), 1)
    causal = key <= qry

    for ph in range(pairs):
        _fox_pair(q_ref, k_ref, v_ref, ft_ref, o_ref, vt_ref, nfcol_ref, ph,
                  pl.program_id(1) * pairs + ph, lo_half, causal, tq, nblk, lookahead)


def _fox_pair(q_ref, k_ref, v_ref, ft_ref, o_ref, vt_ref, nfcol_ref, ph, pair, lo_half, causal,
              tq, nblk, lookahead):
    seq = nblk * tq
    ls = slice(ph * LANES, (ph + 1) * LANES)
    for c in range(seq // tq):
        cs = slice(c * tq, (c + 1) * tq)
        vt = v_ref[0, cs, ls].astype(F32).T.astype(BF16)
        for hh in range(2):
            vt_ref[ph, hh, :HEAD_DIM, cs] = vt[hh * HEAD_DIM:(hh + 1) * HEAD_DIM]
    for hh in range(2):
        vt_ref[ph, hh, HEAD_DIM:, :] = jnp.ones((VT_ROWS - HEAD_DIM, seq), BF16)
    for hh in range(2):
        nfrow = ft_ref[0, pl.ds(2 * pair + hh, 1), :] * (-LOG2E)
        for c in range(seq // LANES):
            cs = slice(c * LANES, (c + 1) * LANES)
            nfcol_ref[ph, hh, cs, :] = jnp.broadcast_to(nfrow[:, cs], (LANES, LANES)).T

    def scores(t, i, hh):
        j = i - t
        ks = slice(j * tq, (j + 1) * tq)
        q = q_ref[0, i * tq:(i + 1) * tq, ls]
        qm = jnp.where(lo_half if hh == 0 else ~lo_half, q, jnp.zeros_like(q))
        st = lax.dot_general(k_ref[0, ks, ls], qm, _NT, preferred_element_type=F32)
        st = st + jnp.concatenate([nfcol_ref[ph, hh, ks, :]] * (tq // LANES), axis=1)
        return jnp.where(causal, st, NEG_INF) if t == 0 else st

    state = {}

    def update(t, i, hh, st):
        j = i - t
        vt = vt_ref[ph, hh, :, j * tq:(j + 1) * tq]
        if t == 0:
            m = jnp.max(st, axis=0, keepdims=True)
            pe = jnp.exp2(st - m)
            acc = jnp.dot(vt, pe.astype(BF16), preferred_element_type=F32)
        else:
            m_old, acc_old = state[i, hh]
            m = jnp.maximum(m_old, jnp.max(st, axis=0, keepdims=True))
            pe = jnp.exp2(st - m)
            acc = jnp.exp2(m_old - m) * acc_old + jnp.dot(vt, pe.astype(BF16),
                                                          preferred_element_type=F32)
        state[i, hh] = (m, acc)
        if j == 0 and hh == 1:
            out_t = jnp.concatenate(
                [state[i, h][1][:HEAD_DIM] * (1.0 / state[i, h][1][HEAD_DIM:HEAD_DIM + 1])
                 for h in range(2)], axis=0)
            o_ref[0, i * tq:(i + 1) * tq, ls] = out_t.T.astype(BF16)

    units = [(t, i, hh) for t in range(nblk) for i in range(t, nblk) for hh in range(2)]
    ready = {}
    for n in range(len(units) + lookahead):
        if n < len(units):
            ready[n] = scores(*units[n])
        if n >= lookahead:
            update(*units[n - lookahead], ready.pop(n - lookahead))


def _fox(qkv, ft, after, tq=256, lookahead=5, pairs=2):
    B, S, _ = qkv.shape
    W = FOX_WIDTH
    wb = pairs * LANES
    n_steps = W // wb
    spec = pl.BlockSpec((1, S, wb), lambda b, p: (b, 0, p))
    part = lambda s: pl.BlockSpec((1, S, wb), lambda b, p: (b, 0, s * n_steps + p))
    return pl.pallas_call(
        functools.partial(_fox_kernel, tq=tq, nblk=S // tq, lookahead=lookahead, pairs=pairs),
        grid=(B, n_steps),
        in_specs=[part(0), part(1), part(2),
                  pl.BlockSpec((1, F_ROWS, S), lambda b, p: (b, 0, 0)),
                  pl.BlockSpec(memory_space=pl.ANY)],
        out_specs=spec,
        out_shape=jax.ShapeDtypeStruct((B, S, W), BF16),
        scratch_shapes=[pltpu.VMEM((pairs, 2, VT_ROWS, S), BF16),
                        pltpu.VMEM((pairs, 2, S, LANES), F32)],
        compiler_params=pltpu.CompilerParams(
            dimension_semantics=("arbitrary", "arbitrary"), vmem_limit_bytes=VMEM_LIMIT),
        name="fox",
    )(qkv, qkv, qkv, ft, after)


def _mix_kernel(x_ref, o0, o1, o2, l0, l1, l2, ob_ref, wg_ref, wa_ref, wb_ref, wo_ref,
                gpre_ref, gpost_ref, y_ref, *stage_refs, tm):
    stage = iter(stage_refs)
    x = x_ref[0]
    h = _rmsnorm(x, gpre_ref[...]).astype(BF16)

    def tokens(ref):
        d = ref.shape[1]
        if d == 1:
            return ref[0, 0].astype(F32)
        st = next(stage)
        for r in range(d):
            for c in range(GROUP_WIDTH // LANES):
                st[_stage_residue_index(c, r, d, tm)] = ref[
                    0, r, :, c * LANES:(c + 1) * LANES].astype(F32)
        return jnp.concatenate(
            [_stage_read_tokens(st, c, d) for c in range(GROUP_WIDTH // LANES)], axis=1)

    la, lb, lc = tokens(l0), tokens(l1), tokens(l2)
    mx = jnp.maximum(jnp.maximum(la, lb), lc)
    e0, e1, e2 = jnp.exp2(la - mx), jnp.exp2(lb - mx), jnp.exp2(lc - mx)
    inv = 1.0 / (e0 + e1 + e2)
    oa = ((e0 * tokens(o0) + e1 * tokens(o1) + e2 * tokens(o2)) * inv).astype(BF16)
    ob = ob_ref[0]
    merged = []
    for c in range(D_MODEL // MXU_COLS):
        cols = slice(c * MXU_COLS, (c + 1) * MXU_COLS)
        ga = jnp.dot(h, wg_ref[:, cols], preferred_element_type=F32)
        gb = jnp.dot(h, wg_ref[:, D_MODEL + c * MXU_COLS:D_MODEL + (c + 1) * MXU_COLS],
                     preferred_element_type=F32)
        a = jnp.dot(oa, wa_ref[:, cols], preferred_element_type=F32)
        b = jnp.dot(ob, wb_ref[:, cols], preferred_element_type=F32)
        merged.append((jax.nn.sigmoid(ga) * a + jax.nn.sigmoid(gb) * b).astype(BF16))
    mix = jnp.dot(jnp.concatenate(merged, axis=1), wo_ref[...], preferred_element_type=F32)
    y_ref[0] = x + _rmsnorm(mix, gpost_ref[...])


def _mix(x, outs, lses, ob, wg, wa, wb, wo, gain_pre, gain_post, tm=1024):
    B, S, D = x.shape
    row = lambda w: pl.BlockSpec((1, tm, w), lambda b, i: (b, i, 0))
    const = lambda shape: pl.BlockSpec(shape, lambda b, i: (0, 0), pipeline_mode=pl.Buffered(1))
    res = lambda t: pl.BlockSpec((1, t.shape[1], tm // t.shape[1], GROUP_WIDTH),
                                 lambda b, i: (b, 0, i, 0))
    stages = [pltpu.VMEM(_stage_shape(t.shape[1], tm), F32)
              for group in (lses, outs) for t in group if t.shape[1] > 1]
    return pl.pallas_call(
        functools.partial(_mix_kernel, tm=tm),
        grid=(B, S // tm),
        in_specs=([row(D)] + [res(t) for t in outs] + [res(t) for t in lses]
                  + [row(FOX_WIDTH), const((D, 2 * D)),
                     const((GROUP_WIDTH, D)), const((FOX_WIDTH, D)), const((D, D)),
                     const((1, D)), const((1, D))]),
        out_specs=row(D),
        out_shape=jax.ShapeDtypeStruct((B, S, D), F32),
        scratch_shapes=stages,
        compiler_params=pltpu.CompilerParams(
            dimension_semantics=("arbitrary", "arbitrary"), vmem_limit_bytes=VMEM_LIMIT),
        name="mix",
    )(x, *outs, *lses, ob, wg, wa, wb, wo, gain_pre, gain_post)


def _ffn_kernel(x_ref, g1_ref, g2_ref, wg_ref, wu_ref, wd_ref, o_ref):
    x = x_ref[...]
    h = _rmsnorm(x, g1_ref[...]).astype(BF16)
    acc = jnp.zeros(x.shape, F32)
    for c in range(D_FF // MXU_COLS):
        cols = slice(c * MXU_COLS, (c + 1) * MXU_COLS)
        g = jnp.dot(h, wg_ref[:, cols], preferred_element_type=F32)
        u = jnp.dot(h, wu_ref[:, cols], preferred_element_type=F32)
        a = (g * jax.nn.sigmoid(g) * u).astype(BF16)
        acc = acc + jnp.dot(a, wd_ref[cols, :], preferred_element_type=F32)
    o_ref[...] = x + _rmsnorm(acc, g2_ref[...])


def _ffn(x2, g1, g2, wg, wu, wd, tm=1024):
    T, D = x2.shape
    row = pl.BlockSpec((tm, D), lambda i: (i, 0))
    const = lambda shape: pl.BlockSpec(shape, lambda i: (0, 0), pipeline_mode=pl.Buffered(1))
    return pl.pallas_call(
        _ffn_kernel,
        grid=(T // tm,),
        in_specs=[row, const((1, D)), const((1, D)), const((D, D_FF)), const((D, D_FF)),
                  const((D_FF, D))],
        out_specs=row,
        out_shape=jax.ShapeDtypeStruct((T, D), F32),
        compiler_params=pltpu.CompilerParams(
            dimension_semantics=("arbitrary",), vmem_limit_bytes=VMEM_LIMIT),
        name="ffn",
    )(x2, g1, g2, wg, wu, wd)


def kernel(x, w_in, w_proj_a, w_proj_b, w_out, b_forget, w_ffn_gate, w_ffn_up, w_ffn_down,
           norm_mix_pre, norm_mix_post, norm_ffn_pre, norm_ffn_post):
    B, S, D = x.shape
    q_scale = float(LOG2E / np.sqrt(HEAD_DIM))
    for layer in range(w_in.shape[0]):
        w_qkv, w_gates, wf = _prep_w_in(w_in[layer].T, q_scale)
        bf = jnp.zeros((F_ROWS, 1), F32).at[:N_FOX_HEADS, 0].set(b_forget[layer])

        (qkv0, qkv1, qkv2, qkvb, ft, wa, wb, wo, wg_ffn, wu_ffn, wd_ffn) = _in_proj(
            x, norm_mix_pre[layer][None, :], w_qkv, wf, bf,
            [w_proj_a[layer], w_proj_b[layer], w_out[layer],
             w_ffn_gate[layer], w_ffn_up[layer], w_ffn_down[layer]])

        outs, lses = [], []
        for qkv in (qkv0, qkv1, qkv2):
            o_g, l_g = _dilated_group(qkv)
            outs.append(o_g)
            lses.append(l_g)
        ob = _fox(qkvb, ft, after=outs[-1])

        x = _mix(x, outs, lses, ob, w_gates, wa, wb, wo, norm_mix_pre[layer][None, :],
                 norm_mix_post[layer][None, :])
        x = _ffn(x.reshape(B * S, D), norm_ffn_pre[layer][None, :], norm_ffn_post[layer][None, :],
                 wg_ffn, wu_ffn, wd_ffn).reshape(B, S, D)
    return x
```

```python
import functools

import numpy as np
import jax
import jax.numpy as jnp
from jax import lax
from jax.experimental import pallas as pl
from jax.experimental.pallas import tpu as pltpu

D_MODEL = 1024
HEAD_DIM = 64
DIL_CONFIGS = ((128, 1), (512, 4), (2048, 16))
N_DIL_GROUPS = 3
GROUP_WIDTH = 256
N_FOX_HEADS = 8
FOX_WIDTH = 512
BLOCK = 128
ROPE_THETA = 500000.0
ROPE_DIM = 16
D_FF = 2816
EPS = 1e-6
NEG_INF = -1e30

DIL_WIDTH = N_DIL_GROUPS * GROUP_WIDTH
QKV_COLS = 3 * DIL_WIDTH + 3 * FOX_WIDTH
GATE_COL0 = QKV_COLS + N_FOX_HEADS
F_ROWS = 16
LANES = 128
SUBLANES = 8
MXU_COLS = 256
VMEM_LIMIT = 56 * 1024 * 1024
VT_ROWS = HEAD_DIM + 16
LOG2E = 1.4426950408889634

F32 = jnp.float32
BF16 = jnp.bfloat16
_NT = (((1,), (1,)), ((), ()))


def _rmsnorm(x, gain):
    return x * lax.rsqrt(jnp.mean(x * x, axis=-1, keepdims=True) + EPS) * gain


def _log_sigmoid(z):
    return jnp.minimum(z, 0.0) - jnp.log1p(jnp.exp(-jnp.abs(z)))


def _stage_groups(d):
    return max(d // SUBLANES, 1)


def _stage_shape(d, tm):
    g = _stage_groups(d)
    return (GROUP_WIDTH // LANES, g, tm // g, LANES)


def _stage_read_tokens(st, c, d):
    g = _stage_groups(d)
    rows = st.shape[2]
    parts = [st[c, j].reshape(rows // SUBLANES, SUBLANES, LANES) for j in range(g)]
    return jnp.stack(parts, axis=1).reshape(rows * g, LANES)


def _stage_residue_index(c, r, d, tm):
    stride = d // _stage_groups(d)
    return (c, r // stride, pl.ds(r % stride, tm // d, stride=stride), slice(None))


def _prep_w_in_kernel(wt_ref, qkv_ref, gates_ref, wf_ref, *, q_scale):
    def block(row0):
        return wt_ref[row0:row0 + LANES, :].T

    for c in range(QKV_COLS // LANES):
        is_q = c * LANES < DIL_WIDTH or 3 * DIL_WIDTH <= c * LANES < 3 * DIL_WIDTH + FOX_WIDTH
        blk = block(c * LANES)
        qkv_ref[:, c * LANES:(c + 1) * LANES] = (blk * q_scale if is_q else blk).astype(BF16)
    for c in range(2 * D_MODEL // LANES):
        gates_ref[:, c * LANES:(c + 1) * LANES] = block(GATE_COL0 + c * LANES).astype(BF16)
    lane = lax.broadcasted_iota(jnp.int32, (wt_ref.shape[1], LANES), 1)
    wf_ref[...] = jnp.where(lane < N_FOX_HEADS, block(QKV_COLS), 0.0).astype(BF16)


def _prep_w_in(w_in_t, q_scale, kb=512):
    C, D = w_in_t.shape
    return pl.pallas_call(
        functools.partial(_prep_w_in_kernel, q_scale=q_scale),
        grid=(D // kb,),
        in_specs=[pl.BlockSpec((C, kb), lambda i: (0, i))],
        out_specs=[pl.BlockSpec((kb, QKV_COLS), lambda i: (i, 0)),
                   pl.BlockSpec((kb, 2 * D), lambda i: (i, 0)),
                   pl.BlockSpec((kb, LANES), lambda i: (i, 0))],
        out_shape=[jax.ShapeDtypeStruct((D, QKV_COLS), BF16),
                   jax.ShapeDtypeStruct((D, 2 * D), BF16),
                   jax.ShapeDtypeStruct((D, LANES), BF16)],
        compiler_params=pltpu.CompilerParams(
            dimension_semantics=("arbitrary",), vmem_limit_bytes=VMEM_LIMIT),
        name="prep_w_in",
    )(w_in_t)


def _in_proj_kernel(*refs, tm, n_side):
    (x_ref, gain_ref, w_ref, wf_ref, bf_ref, cos_ref, sa_ref, sb_ref), refs = refs[:8], refs[8:]
    side_in, refs = refs[:n_side], refs[n_side:]
    (qkv0, qkv1, qkv2, qkvb_ref, ft_ref), refs = refs[:5], refs[5:]
    side_out, refs = refs[:n_side], refs[n_side:]
    (carry_ref,) = refs
    i = pl.program_id(1)
    h = _rmsnorm(x_ref[0], gain_ref[...]).astype(BF16)

    z = jnp.dot(h, wf_ref[...], preferred_element_type=F32).T[:F_ROWS] + bf_ref[...]
    logf = _log_sigmoid(z)
    lane = lax.broadcasted_iota(jnp.int32, (F_ROWS, LANES), 1)
    sums = []
    for j in range(tm // LANES):
        c = logf[:, j * LANES:(j + 1) * LANES]
        k = 1
        while k < LANES:
            c = c + jnp.where(lane >= k, pltpu.roll(c, k, 1), 0.0)
            k *= 2
        sums.append(c)
    carry = jnp.where(i == 0, 0.0, carry_ref[...])
    for j, c in enumerate(sums):
        c = c + carry
        ft_ref[0, :, j * LANES:(j + 1) * LANES] = c
        carry = jnp.broadcast_to(c[:, LANES - 1:LANES], (F_ROWS, LANES))
    carry_ref[...] = carry

    def proj(c0, width=MXU_COLS):
        return jnp.dot(h, w_ref[:, c0:c0 + width], preferred_element_type=F32)

    cos, sa, sb = cos_ref[...], sa_ref[...], sb_ref[...]

    def rope(y):
        halves = []
        for c in range(MXU_COLS // LANES):
            yc = y[:, c * LANES:(c + 1) * LANES]
            halves.append(yc * cos + pltpu.roll(yc, LANES - ROPE_DIM // 2, 1) * sa
                          + pltpu.roll(yc, ROPE_DIM // 2, 1) * sb)
        return jnp.concatenate(halves, axis=1)

    def put(o_ref, col0, y, d):
        if d == 1:
            o_ref[0, 0, :, col0:col0 + GROUP_WIDTH] = y.astype(BF16)
            return
        yt = jnp.swapaxes(y.reshape(tm // d, d, GROUP_WIDTH), 0, 1)
        for r in range(d):
            o_ref[0, r, :, col0:col0 + GROUP_WIDTH] = yt[r].astype(BF16)

    for g, o_ref in enumerate((qkv0, qkv1, qkv2)):
        d = DIL_CONFIGS[g][1]
        put(o_ref, 0, rope(proj(g * GROUP_WIDTH)), d)
        put(o_ref, GROUP_WIDTH, rope(proj(DIL_WIDTH + g * GROUP_WIDTH)), d)
        put(o_ref, 2 * GROUP_WIDTH, proj(2 * DIL_WIDTH + g * GROUP_WIDTH), d)
    base = 3 * DIL_WIDTH
    for c in range(3 * FOX_WIDTH // MXU_COLS):
        qkvb_ref[0, :, c * MXU_COLS:(c + 1) * MXU_COLS] = proj(base + c * MXU_COLS).astype(BF16)
    for src, dst in zip(side_in, side_out):
        dst[...] = src[...].astype(BF16)


def _rope_tables(seq):
    half = ROPE_DIM // 2
    inv_freq = np.power(ROPE_THETA, -np.arange(0, ROPE_DIM, 2, dtype=np.float64) / ROPE_DIM)
    ang = np.arange(seq, dtype=np.float64)[:, None] * inv_freq[None, :]
    cos = np.ones((seq, HEAD_DIM)); sa = np.zeros((seq, HEAD_DIM)); sb = np.zeros((seq, HEAD_DIM))
    cos[:, :half] = np.cos(ang); cos[:, half:ROPE_DIM] = np.cos(ang)
    sa[:, :half] = -np.sin(ang)
    sb[:, half:ROPE_DIM] = np.sin(ang)
    rep = LANES // HEAD_DIM
    return tuple(jnp.asarray(np.tile(t, (1, rep)), dtype=F32) for t in (cos, sa, sb))


def _in_proj(x, gain, w_main, wf, bf, side_weights, tm=1024):
    B, S, D = x.shape
    n_i = S // tm
    n_steps = B * n_i
    side_specs = [pl.BlockSpec((w.shape[0] // n_steps, w.shape[1]), lambda b, i: (b * n_i + i, 0))
                  for w in side_weights]
    side_shapes = [jax.ShapeDtypeStruct(w.shape, BF16) for w in side_weights]
    cos, sa, sb = _rope_tables(S)
    const = lambda shape: pl.BlockSpec(shape, lambda b, i: (0,) * len(shape),
                                       pipeline_mode=pl.Buffered(1))
    row = lambda w: pl.BlockSpec((1, tm, w), lambda b, i: (b, i, 0))
    tab = pl.BlockSpec((tm, LANES), lambda b, i: (i, 0))
    dils = [d for _, d in DIL_CONFIGS]
    res_shape = [jax.ShapeDtypeStruct((B, d, S // d, 3 * GROUP_WIDTH), BF16) for d in dils]
    res_spec = [pl.BlockSpec((1, d, tm // d, 3 * GROUP_WIDTH), lambda b, i: (b, 0, i, 0))
                for d in dils]
    out_shape = (res_shape
                 + [jax.ShapeDtypeStruct((B, S, 3 * FOX_WIDTH), BF16)]
                 + [jax.ShapeDtypeStruct((B, F_ROWS, S), F32)])
    out_specs = (res_spec + [row(3 * FOX_WIDTH)]
                 + [pl.BlockSpec((1, F_ROWS, tm), lambda b, i: (b, 0, i))])
    return pl.pallas_call(
        functools.partial(_in_proj_kernel, tm=tm, n_side=len(side_weights)),
        grid=(B, n_i),
        in_specs=[row(D), const((1, D)), const((D, QKV_COLS)), const((D, LANES)),
                  const((F_ROWS, 1)), tab, tab, tab] + side_specs,
        out_specs=out_specs + side_specs,
        out_shape=out_shape + side_shapes,
        scratch_shapes=[pltpu.VMEM((F_ROWS, LANES), F32)],
        compiler_params=pltpu.CompilerParams(
            dimension_semantics=("arbitrary", "arbitrary"), vmem_limit_bytes=VMEM_LIMIT),
        name="in_proj",
    )(x, gain, w_main, wf, bf, cos, sa, sb, *side_weights)


def _dilated_kernel(qkv_ref, o_ref, lse_ref, vt_ref, *, d, nb, lookahead):
    lane = lax.broadcasted_iota(jnp.int32, (BLOCK, LANES), 1)
    lo_half = lane < HEAD_DIM
    kw = 2 * BLOCK
    n_pairs = GROUP_WIDTH // LANES

    kj = lax.broadcasted_iota(jnp.int32, (kw, kw), 0)
    qi = lax.broadcasted_iota(jnp.int32, (kw, kw), 1) % BLOCK
    kj1 = lax.broadcasted_iota(jnp.int32, (BLOCK, kw), 0)
    qi1 = lax.broadcasted_iota(jnp.int32, (BLOCK, kw), 1) % BLOCK
    bias_lead = jnp.where(kj1 <= qi1, 0.0, NEG_INF)
    bias_band = jnp.where((kj >= qi) & (kj <= qi + BLOCK), 0.0, NEG_INF)

    for r in range(d):
        for hp in range(n_pairs):
            cols = slice(hp * LANES, (hp + 1) * LANES)
            for n in range(nb):
                rows = slice(n * BLOCK, (n + 1) * BLOCK)
                vt = qkv_ref[0, r, rows, 2 * GROUP_WIDTH + hp * LANES:
                             2 * GROUP_WIDTH + (hp + 1) * LANES].astype(F32).T.astype(BF16)
                vt_ref[r, hp, 0:HEAD_DIM, rows] = vt[:HEAD_DIM]
                vt_ref[r, hp, VT_ROWS:VT_ROWS + HEAD_DIM, rows] = vt[HEAD_DIM:]
            ones = jnp.ones((VT_ROWS - HEAD_DIM, nb * BLOCK), BF16)
            vt_ref[r, hp, HEAD_DIM:VT_ROWS, :] = ones
            vt_ref[r, hp, VT_ROWS + HEAD_DIM:, :] = ones

    def scores(r, n, hp):
        cols = slice(hp * LANES, (hp + 1) * LANES)
        q = qkv_ref[0, r, n * BLOCK:(n + 1) * BLOCK, cols]
        zero = jnp.zeros_like(q)
        qm = jnp.concatenate([jnp.where(lo_half, q, zero), jnp.where(lo_half, zero, q)], axis=0)
        keys = slice(0, BLOCK) if n == 0 else slice((n - 1) * BLOCK, (n + 1) * BLOCK)
        k_cols = slice(GROUP_WIDTH + hp * LANES, GROUP_WIDTH + (hp + 1) * LANES)
        st = lax.dot_general(qkv_ref[0, r, keys, k_cols], qm, _NT,
                             preferred_element_type=F32)
        return st + (bias_lead if n == 0 else bias_band)

    def finish(r, n, hp, st):
        cols = slice(hp * LANES, (hp + 1) * LANES)
        rows = slice(n * BLOCK, (n + 1) * BLOCK)
        keys = slice(0, BLOCK) if n == 0 else slice((n - 1) * BLOCK, (n + 1) * BLOCK)
        vt = vt_ref[r, hp, :, keys]
        m = jnp.max(st, axis=0, keepdims=True)
        pe = jnp.exp2(st - m).astype(BF16)
        acc = jnp.dot(vt, pe, preferred_element_type=F32)
        out_t, lse_t = [], []
        for hh in range(2):
            a = acc[hh * VT_ROWS:(hh + 1) * VT_ROWS, hh * BLOCK:(hh + 1) * BLOCK]
            den = a[HEAD_DIM:HEAD_DIM + 1]
            out_t.append(a[:HEAD_DIM] * (1.0 / den))
            lse = m[:, hh * BLOCK:(hh + 1) * BLOCK] + jnp.log2(den)
            lse_t.append(jnp.broadcast_to(lse, (HEAD_DIM, BLOCK)))
        o_ref[0, r, rows, cols] = jnp.concatenate(out_t, axis=0).T.astype(BF16)
        lse_ref[0, r, rows, cols] = jnp.concatenate(lse_t, axis=0).T

    units = [(r, n, hp) for r in range(d) for n in range(nb) for hp in range(n_pairs)]
    ready = {}
    for idx in range(len(units) + lookahead):
        if idx < len(units):
            ready[idx] = scores(*units[idx])
        if idx >= lookahead:
            finish(*units[idx - lookahead], ready.pop(idx - lookahead))


def _dilated_all_kernel(*refs, shapes, lookahead):
    n = len(shapes)
    for g, (d, nb) in enumerate(shapes):
        _dilated_kernel(refs[g], refs[n + 2 * g], refs[n + 2 * g + 1], refs[3 * n + g],
                        d=d, nb=nb, lookahead=lookahead)


def _dilated_groups(qkvs, lookahead=4):
    W = GROUP_WIDTH
    B = qkvs[0].shape[0]
    in_specs, out_specs, out_shape, scratch, shapes = [], [], [], [], []
    for qkv in qkvs:
        _, d, L, _ = qkv.shape
        shapes.append((d, L // BLOCK))
        in_specs.append(pl.BlockSpec((1, d, L, 3 * W), lambda b: (b, 0, 0, 0)))
        out_specs += [pl.BlockSpec((1, d, L, W), lambda b: (b, 0, 0, 0))] * 2
        out_shape += [jax.ShapeDtypeStruct((B, d, L, W), BF16),
                      jax.ShapeDtypeStruct((B, d, L, W), F32)]
        scratch.append(pltpu.VMEM((d, W // LANES, 2 * VT_ROWS, L), BF16))
    res = pl.pallas_call(
        functools.partial(_dilated_all_kernel, shapes=tuple(shapes), lookahead=lookahead),
        grid=(B,),
        in_specs=in_specs,
        out_specs=out_specs,
        out_shape=out_shape,
        scratch_shapes=scratch,
        compiler_params=pltpu.CompilerParams(
            dimension_semantics=("arbitrary",), vmem_limit_bytes=VMEM_LIMIT),
        name="dilated",
    )(*qkvs)
    return list(res[0::2]), list(res[1::2])


def _fox_kernel(q_ref, k_ref, v_ref, ft_ref, _after_ref, o_ref, vt_ref, nfcol_ref, *, tq, nblk,
                lookahead, pairs):
    lane = lax.broadcasted_iota(jnp.int32, (tq, LANES), 1)
    lo_half = lane < HEAD_DIM
    key = lax.broadcasted_iota(jnp.int32, (tq, tq), 0)
    qry = lax.broadcasted_iota(jnp.int32, (tq, tq), 1)
    causal = key <= qry

    for ph in range(pairs):
        _fox_pair(q_ref, k_ref, v_ref, ft_ref, o_ref, vt_ref, nfcol_ref, ph,
                  pl.program_id(1) * pairs + ph, lo_half, causal, tq, nblk, lookahead)


def _fox_pair(q_ref, k_ref, v_ref, ft_ref, o_ref, vt_ref, nfcol_ref, ph, pair, lo_half, causal,
              tq, nblk, lookahead):
    seq = nblk * tq
    ls = slice(ph * LANES, (ph + 1) * LANES)
    for c in range(seq // tq):
        cs = slice(c * tq, (c + 1) * tq)
        vt = v_ref[0, cs, ls].astype(F32).T.astype(BF16)
        for hh in range(2):
            vt_ref[ph, hh, :HEAD_DIM, cs] = vt[hh * HEAD_DIM:(hh + 1) * HEAD_DIM]
    for hh in range(2):
        vt_ref[ph, hh, HEAD_DIM:, :] = jnp.ones((VT_ROWS - HEAD_DIM, seq), BF16)
    for hh in range(2):
        nfrow = ft_ref[0, pl.ds(2 * pair + hh, 1), :] * (-LOG2E)
        for c in range(seq // LANES):
            cs = slice(c * LANES, (c + 1) * LANES)
            nfcol_ref[ph, hh, cs, :] = jnp.broadcast_to(nfrow[:, cs], (LANES, LANES)).T

    def scores(t, i, hh):
        j = i - t
        ks = slice(j * tq, (j + 1) * tq)
        q = q_ref[0, i * tq:(i + 1) * tq, ls]
        qm = jnp.where(lo_half if hh == 0 else ~lo_half, q, jnp.zeros_like(q))
        st = lax.dot_general(k_ref[0, ks, ls], qm, _NT, preferred_element_type=F32)
        st = st + jnp.concatenate([nfcol_ref[ph, hh, ks, :]] * (tq // LANES), axis=1)
        return jnp.where(causal, st, NEG_INF) if t == 0 else st

    state = {}

    def update(t, i, hh, st):
        j = i - t
        vt = vt_ref[ph, hh, :, j * tq:(j + 1) * tq]
        if t == 0:
            m = jnp.max(st, axis=0, keepdims=True)
            pe = jnp.exp2(st - m)
            acc = jnp.dot(vt, pe.astype(BF16), preferred_element_type=F32)
        else:
            m_old, acc_old = state[i, hh]
            m = jnp.maximum(m_old, jnp.max(st, axis=0, keepdims=True))
            pe = jnp.exp2(st - m)
            acc = jnp.exp2(m_old - m) * acc_old + jnp.dot(vt, pe.astype(BF16),
                                                          preferred_element_type=F32)
        state[i, hh] = (m, acc)
        if j == 0 and hh == 1:
            out_t = jnp.concatenate(
                [state[i, h][1][:HEAD_DIM] * (1.0 / state[i, h][1][HEAD_DIM:HEAD_DIM + 1])
                 for h in range(2)], axis=0)
            o_ref[0, i * tq:(i + 1) * tq, ls] = out_t.T.astype(BF16)

    units = [(t, i, hh) for t in range(nblk) for i in range(t, nblk) for hh in range(2)]
    ready = {}
    for n in range(len(units) + lookahead):
        if n < len(units):
            ready[n] = scores(*units[n])
        if n >= lookahead:
            update(*units[n - lookahead], ready.pop(n - lookahead))


def _fox(qkv, ft, after, tq=256, lookahead=5, pairs=2):
    B, S, _ = qkv.shape
    W = FOX_WIDTH
    wb = pairs * LANES
    n_steps = W // wb
    spec = pl.BlockSpec((1, S, wb), lambda b, p: (b, 0, p))
    part = lambda s: pl.BlockSpec((1, S, wb), lambda b, p: (b, 0, s * n_steps + p))
    return pl.pallas_call(
        functools.partial(_fox_kernel, tq=tq, nblk=S // tq, lookahead=lookahead, pairs=pairs),
        grid=(B, n_steps),
        in_specs=[part(0), part(1), part(2),
                  pl.BlockSpec((1, F_ROWS, S), lambda b, p: (b, 0, 0)),
                  pl.BlockSpec(memory_space=pl.ANY)],
        out_specs=spec,
        out_shape=jax.ShapeDtypeStruct((B, S, W), BF16),
        scratch_shapes=[pltpu.VMEM((pairs, 2, VT_ROWS, S), BF16),
                        pltpu.VMEM((pairs, 2, S, LANES), F32)],
        compiler_params=pltpu.CompilerParams(
            dimension_semantics=("arbitrary", "arbitrary"), vmem_limit_bytes=VMEM_LIMIT),
        name="fox",
    )(qkv, qkv, qkv, ft, after)


def _mix_kernel(x_ref, o0, o1, o2, l0, l1, l2, ob_ref, wg_ref, wa_ref, wb_ref, wo_ref,
                gpre_ref, gpost_ref, y_ref, *stage_refs, tm):
    stage = iter(stage_refs)
    x = x_ref[0]
    h = _rmsnorm(x, gpre_ref[...]).astype(BF16)

    def tokens(ref):
        d = ref.shape[1]
        if d == 1:
            return ref[0, 0].astype(F32)
        st = next(stage)
        for r in range(d):
            for c in range(GROUP_WIDTH // LANES):
                st[_stage_residue_index(c, r, d, tm)] = ref[
                    0, r, :, c * LANES:(c + 1) * LANES].astype(F32)
        return jnp.concatenate(
            [_stage_read_tokens(st, c, d) for c in range(GROUP_WIDTH // LANES)], axis=1)

    la, lb, lc = tokens(l0), tokens(l1), tokens(l2)
    mx = jnp.maximum(jnp.maximum(la, lb), lc)
    e0, e1, e2 = jnp.exp2(la - mx), jnp.exp2(lb - mx), jnp.exp2(lc - mx)
    inv = 1.0 / (e0 + e1 + e2)
    oa = ((e0 * tokens(o0) + e1 * tokens(o1) + e2 * tokens(o2)) * inv).astype(BF16)
    ob = ob_ref[0]
    merged = []
    for c in range(D_MODEL // MXU_COLS):
        cols = slice(c * MXU_COLS, (c + 1) * MXU_COLS)
        ga = jnp.dot(h, wg_ref[:, cols], preferred_element_type=F32)
        gb = jnp.dot(h, wg_ref[:, D_MODEL + c * MXU_COLS:D_MODEL + (c + 1) * MXU_COLS],
                     preferred_element_type=F32)
        a = jnp.dot(oa, wa_ref[:, cols], preferred_element_type=F32)
        b = jnp.dot(ob, wb_ref[:, cols], preferred_element_type=F32)
        merged.append((jax.nn.sigmoid(ga) * a + jax.nn.sigmoid(gb) * b).astype(BF16))
    mix = jnp.dot(jnp.concatenate(merged, axis=1), wo_ref[...], preferred_element_type=F32)
    y_ref[0] = x + _rmsnorm(mix, gpost_ref[...])


def _mix(x, outs, lses, ob, wg, wa, wb, wo, gain_pre, gain_post, tm=1024):
    B, S, D = x.shape
    row = lambda w: pl.BlockSpec((1, tm, w), lambda b, i: (b, i, 0))
    const = lambda shape: pl.BlockSpec(shape, lambda b, i: (0, 0), pipeline_mode=pl.Buffered(1))
    res = lambda t: pl.BlockSpec((1, t.shape[1], tm // t.shape[1], GROUP_WIDTH),
                                 lambda b, i: (b, 0, i, 0))
    stages = [pltpu.VMEM(_stage_shape(t.shape[1], tm), F32)
              for group in (lses, outs) for t in group if t.shape[1] > 1]
    return pl.pallas_call(
        functools.partial(_mix_kernel, tm=tm),
        grid=(B, S // tm),
        in_specs=([row(D)] + [res(t) for t in outs] + [res(t) for t in lses]
                  + [row(FOX_WIDTH), const((D, 2 * D)),
                     const((GROUP_WIDTH, D)), const((FOX_WIDTH, D)), const((D, D)),
                     const((1, D)), const((1, D))]),
        out_specs=row(D),
        out_shape=jax.ShapeDtypeStruct((B, S, D), F32),
        scratch_shapes=stages,
        compiler_params=pltpu.CompilerParams(
            dimension_semantics=("arbitrary", "arbitrary"), vmem_limit_bytes=VMEM_LIMIT),
        name="mix",
    )(x, *outs, *lses, ob, wg, wa, wb, wo, gain_pre, gain_post)


def _ffn_kernel(x_ref, g1_ref, g2_ref, wg_ref, wu_ref, wd_ref, o_ref):
    x = x_ref[...]
    h = _rmsnorm(x, g1_ref[...]).astype(BF16)
    acc = jnp.zeros(x.shape, F32)
    for c in range(D_FF // MXU_COLS):
        cols = slice(c * MXU_COLS, (c + 1) * MXU_COLS)
        g = jnp.dot(h, wg_ref[:, cols], preferred_element_type=F32)
        u = jnp.dot(h, wu_ref[:, cols], preferred_element_type=F32)
        a = (g * jax.nn.sigmoid(g) * u).astype(BF16)
        acc = acc + jnp.dot(a, wd_ref[cols, :], preferred_element_type=F32)
    o_ref[...] = x + _rmsnorm(acc, g2_ref[...])


def _ffn(x2, g1, g2, wg, wu, wd, tm=1024):
    T, D = x2.shape
    row = pl.BlockSpec((tm, D), lambda i: (i, 0))
    const = lambda shape: pl.BlockSpec(shape, lambda i: (0, 0), pipeline_mode=pl.Buffered(1))
    return pl.pallas_call(
        _ffn_kernel,
        grid=(T // tm,),
        in_specs=[row, const((1, D)), const((1, D)), const((D, D_FF)), const((D, D_FF)),
                  const((D_FF, D))],
        out_specs=row,
        out_shape=jax.ShapeDtypeStruct((T, D), F32),
        compiler_params=pltpu.CompilerParams(
            dimension_semantics=("arbitrary",), vmem_limit_bytes=VMEM_LIMIT),
        name="ffn",
    )(x2, g1, g2, wg, wu, wd)


def kernel(x, w_in, w_proj_a, w_proj_b, w_out, b_forget, w_ffn_gate, w_ffn_up, w_ffn_down,
           norm_mix_pre, norm_mix_post, norm_ffn_pre, norm_ffn_post):
    B, S, D = x.shape
    q_scale = float(LOG2E / np.sqrt(HEAD_DIM))
    for layer in range(w_in.shape[0]):
        w_qkv, w_gates, wf = _prep_w_in(w_in[layer].T, q_scale)
        bf = jnp.zeros((F_ROWS, 1), F32).at[:N_FOX_HEADS, 0].set(b_forget[layer])

        (qkv0, qkv1, qkv2, qkvb, ft, wa, wb, wo, wg_ffn, wu_ffn, wd_ffn) = _in_proj(
            x, norm_mix_pre[layer][None, :], w_qkv, wf, bf,
            [w_proj_a[layer], w_proj_b[layer], w_out[layer],
             w_ffn_gate[layer], w_ffn_up[layer], w_ffn_down[layer]])

        outs, lses = _dilated_groups([qkv0, qkv1, qkv2])
        ob = _fox(qkvb, ft, after=outs[-1])

        x = _mix(x, outs, lses, ob, w_gates, wa, wb, wo, norm_mix_pre[layer][None, :],
                 norm_mix_post[layer][None, :])
        x = _ffn(x.reshape(B * S, D), norm_ffn_pre[layer][None, :], norm_ffn_post[layer][None, :],
                 wg_ffn, wu_ffn, wd_ffn).reshape(B, S, D)
    return x
```

```python
import functools

import numpy as np
import jax
import jax.numpy as jnp
from jax import lax
from jax.experimental import pallas as pl
from jax.experimental.pallas import tpu as pltpu

D_MODEL = 1024
HEAD_DIM = 64
DIL_CONFIGS = ((128, 1), (512, 4), (2048, 16))
N_DIL_GROUPS = 3
GROUP_WIDTH = 256
N_FOX_HEADS = 8
FOX_WIDTH = 512
BLOCK = 128
ROPE_THETA = 500000.0
ROPE_DIM = 16
D_FF = 2816
EPS = 1e-6
NEG_INF = -1e30

DIL_WIDTH = N_DIL_GROUPS * GROUP_WIDTH
QKV_COLS = 3 * DIL_WIDTH + 3 * FOX_WIDTH
GATE_COL0 = QKV_COLS + N_FOX_HEADS
F_ROWS = 16
LANES = 128
SUBLANES = 8
MXU_COLS = 256
VMEM_LIMIT = 56 * 1024 * 1024
VT_ROWS = HEAD_DIM + 16
LOG2E = 1.4426950408889634

F32 = jnp.float32
BF16 = jnp.bfloat16
_NT = (((1,), (1,)), ((), ()))


def _rmsnorm(x, gain):
    return x * lax.rsqrt(jnp.mean(x * x, axis=-1, keepdims=True) + EPS) * gain


def _log_sigmoid(z):
    return jnp.minimum(z, 0.0) - jnp.log1p(jnp.exp(-jnp.abs(z)))


def _stage_groups(d):
    return max(d // SUBLANES, 1)


def _stage_shape(d, tm):
    g = _stage_groups(d)
    return (GROUP_WIDTH // LANES, g, tm // g, LANES)


def _stage_read_tokens(st, c, d):
    g = _stage_groups(d)
    rows = st.shape[2]
    parts = [st[c, j].reshape(rows // SUBLANES, SUBLANES, LANES) for j in range(g)]
    return jnp.stack(parts, axis=1).reshape(rows * g, LANES)


def _stage_residue_index(c, r, d, tm):
    stride = d // _stage_groups(d)
    return (c, r // stride, pl.ds(r % stride, tm // d, stride=stride), slice(None))


def _prep_w_in_kernel(wt_ref, qkv_ref, gates_ref, wf_ref, *, q_scale):
    def block(row0):
        return wt_ref[row0:row0 + LANES, :].T

    for c in range(QKV_COLS // LANES):
        is_q = c * LANES < DIL_WIDTH or 3 * DIL_WIDTH <= c * LANES < 3 * DIL_WIDTH + FOX_WIDTH
        blk = block(c * LANES)
        qkv_ref[:, c * LANES:(c + 1) * LANES] = (blk * q_scale if is_q else blk).astype(BF16)
    for c in range(2 * D_MODEL // LANES):
        gates_ref[:, c * LANES:(c + 1) * LANES] = block(GATE_COL0 + c * LANES).astype(BF16)
    lane = lax.broadcasted_iota(jnp.int32, (wt_ref.shape[1], LANES), 1)
    wf_ref[...] = jnp.where(lane < N_FOX_HEADS, block(QKV_COLS), 0.0).astype(BF16)


def _prep_w_in(w_in_t, q_scale, kb=512):
    C, D = w_in_t.shape
    return pl.pallas_call(
        functools.partial(_prep_w_in_kernel, q_scale=q_scale),
        grid=(D // kb,),
        in_specs=[pl.BlockSpec((C, kb), lambda i: (0, i))],
        out_specs=[pl.BlockSpec((kb, QKV_COLS), lambda i: (i, 0)),
                   pl.BlockSpec((kb, 2 * D), lambda i: (i, 0)),
                   pl.BlockSpec((kb, LANES), lambda i: (i, 0))],
        out_shape=[jax.ShapeDtypeStruct((D, QKV_COLS), BF16),
                   jax.ShapeDtypeStruct((D, 2 * D), BF16),
                   jax.ShapeDtypeStruct((D, LANES), BF16)],
        compiler_params=pltpu.CompilerParams(
            dimension_semantics=("arbitrary",), vmem_limit_bytes=VMEM_LIMIT),
        name="prep_w_in",
    )(w_in_t)


def _in_proj_kernel(*refs, tm, n_side):
    (x_ref, gain_ref, w_ref, wf_ref, bf_ref, cos_ref, sa_ref, sb_ref), refs = refs[:8], refs[8:]
    side_in, refs = refs[:n_side], refs[n_side:]
    (qkv0, qkv1, qkv2, qkvb_ref, ft_ref), refs = refs[:5], refs[5:]
    side_out, refs = refs[:n_side], refs[n_side:]
    (carry_ref,) = refs
    i = pl.program_id(1)
    h = _rmsnorm(x_ref[0], gain_ref[...]).astype(BF16)

    z = jnp.dot(h, wf_ref[...], preferred_element_type=F32).T[:F_ROWS] + bf_ref[...]
    logf = _log_sigmoid(z)
    lane = lax.broadcasted_iota(jnp.int32, (F_ROWS, LANES), 1)
    sums = []
    for j in range(tm // LANES):
        c = logf[:, j * LANES:(j + 1) * LANES]
        k = 1
        while k < LANES:
            c = c + jnp.where(lane >= k, pltpu.roll(c, k, 1), 0.0)
            k *= 2
        sums.append(c)
    carry = jnp.where(i == 0, 0.0, carry_ref[...])
    for j, c in enumerate(sums):
        c = c + carry
        ft_ref[0, :, j * LANES:(j + 1) * LANES] = c
        carry = jnp.broadcast_to(c[:, LANES - 1:LANES], (F_ROWS, LANES))
    carry_ref[...] = carry

    def proj(c0, width=MXU_COLS):
        return jnp.dot(h, w_ref[:, c0:c0 + width], preferred_element_type=F32)

    cos, sa, sb = cos_ref[...], sa_ref[...], sb_ref[...]

    def rope(y):
        halves = []
        for c in range(MXU_COLS // LANES):
            yc = y[:, c * LANES:(c + 1) * LANES]
            halves.append(yc * cos + pltpu.roll(yc, LANES - ROPE_DIM // 2, 1) * sa
                          + pltpu.roll(yc, ROPE_DIM // 2, 1) * sb)
        return jnp.concatenate(halves, axis=1)

    def put(o_ref, col0, y, d):
        if d == 1:
            o_ref[0, 0, :, col0:col0 + GROUP_WIDTH] = y.astype(BF16)
            return
        yt = jnp.swapaxes(y.reshape(tm // d, d, GROUP_WIDTH), 0, 1)
        for r in range(d):
            o_ref[0, r, :, col0:col0 + GROUP_WIDTH] = yt[r].astype(BF16)

    for g, o_ref in enumerate((qkv0, qkv1, qkv2)):
        d = DIL_CONFIGS[g][1]
        put(o_ref, 0, rope(proj(g * GROUP_WIDTH)), d)
        put(o_ref, GROUP_WIDTH, rope(proj(DIL_WIDTH + g * GROUP_WIDTH)), d)
        put(o_ref, 2 * GROUP_WIDTH, proj(2 * DIL_WIDTH + g * GROUP_WIDTH), d)
    base = 3 * DIL_WIDTH
    for c in range(3 * FOX_WIDTH // MXU_COLS):
        qkvb_ref[0, :, c * MXU_COLS:(c + 1) * MXU_COLS] = proj(base + c * MXU_COLS).astype(BF16)
    for src, dst in zip(side_in, side_out):
        dst[...] = src[...].astype(BF16)


def _rope_tables(seq):
    half = ROPE_DIM // 2
    inv_freq = np.power(ROPE_THETA, -np.arange(0, ROPE_DIM, 2, dtype=np.float64) / ROPE_DIM)
    ang = np.arange(seq, dtype=np.float64)[:, None] * inv_freq[None, :]
    cos = np.ones((seq, HEAD_DIM)); sa = np.zeros((seq, HEAD_DIM)); sb = np.zeros((seq, HEAD_DIM))
    cos[:, :half] = np.cos(ang); cos[:, half:ROPE_DIM] = np.cos(ang)
    sa[:, :half] = -np.sin(ang)
    sb[:, half:ROPE_DIM] = np.sin(ang)
    rep = LANES // HEAD_DIM
    return tuple(jnp.asarray(np.tile(t, (1, rep)), dtype=F32) for t in (cos, sa, sb))


def _in_proj(x, gain, w_main, wf, bf, side_weights, tm=1024):
    B, S, D = x.shape
    n_i = S // tm
    n_steps = B * n_i
    side_specs = [pl.BlockSpec((w.shape[0] // n_steps, w.shape[1]), lambda b, i: (b * n_i + i, 0))
                  for w in side_weights]
    side_shapes = [jax.ShapeDtypeStruct(w.shape, BF16) for w in side_weights]
    cos, sa, sb = _rope_tables(S)
    const = lambda shape: pl.BlockSpec(shape, lambda b, i: (0,) * len(shape),
                                       pipeline_mode=pl.Buffered(1))
    row = lambda w: pl.BlockSpec((1, tm, w), lambda b, i: (b, i, 0))
    tab = pl.BlockSpec((tm, LANES), lambda b, i: (i, 0))
    dils = [d for _, d in DIL_CONFIGS]
    res_shape = [jax.ShapeDtypeStruct((B, d, S // d, 3 * GROUP_WIDTH), BF16) for d in dils]
    res_spec = [pl.BlockSpec((1, d, tm // d, 3 * GROUP_WIDTH), lambda b, i: (b, 0, i, 0))
                for d in dils]
    out_shape = (res_shape
                 + [jax.ShapeDtypeStruct((B, S, 3 * FOX_WIDTH), BF16)]
                 + [jax.ShapeDtypeStruct((B, F_ROWS, S), F32)])
    out_specs = (res_spec + [row(3 * FOX_WIDTH)]
                 + [pl.BlockSpec((1, F_ROWS, tm), lambda b, i: (b, 0, i))])
    return pl.pallas_call(
        functools.partial(_in_proj_kernel, tm=tm, n_side=len(side_weights)),
        grid=(B, n_i),
        in_specs=[row(D), const((1, D)), const((D, QKV_COLS)), const((D, LANES)),
                  const((F_ROWS, 1)), tab, tab, tab] + side_specs,
        out_specs=out_specs + side_specs,
        out_shape=out_shape + side_shapes,
        scratch_shapes=[pltpu.VMEM((F_ROWS, LANES), F32)],
        compiler_params=pltpu.CompilerParams(
            dimension_semantics=("arbitrary", "arbitrary"), vmem_limit_bytes=VMEM_LIMIT),
        name="in_proj",
    )(x, gain, w_main, wf, bf, cos, sa, sb, *side_weights)


def _dilated_kernel(qkv_ref, o_ref, lse_ref, vt_ref, *, d, nb):
    lane = lax.broadcasted_iota(jnp.int32, (BLOCK, LANES), 1)
    lo_half = lane < HEAD_DIM
    kw = 2 * BLOCK
    n_pairs = GROUP_WIDTH // LANES

    kj = lax.broadcasted_iota(jnp.int32, (kw, kw), 0)
    qi = lax.broadcasted_iota(jnp.int32, (kw, kw), 1) % BLOCK
    kj1 = lax.broadcasted_iota(jnp.int32, (BLOCK, kw), 0)
    qi1 = lax.broadcasted_iota(jnp.int32, (BLOCK, kw), 1) % BLOCK
    bias_lead = jnp.where(kj1 <= qi1, 0.0, NEG_INF)
    bias_band = jnp.where((kj >= qi) & (kj <= qi + BLOCK), 0.0, NEG_INF)

    for r in range(d):
        for hp in range(n_pairs):
            cols = slice(hp * LANES, (hp + 1) * LANES)
            for n in range(nb):
                rows = slice(n * BLOCK, (n + 1) * BLOCK)
                vt = qkv_ref[0, r, rows, 2 * GROUP_WIDTH + hp * LANES:
                             2 * GROUP_WIDTH + (hp + 1) * LANES].astype(F32).T.astype(BF16)
                vt_ref[r, hp, 0:HEAD_DIM, rows] = vt[:HEAD_DIM]
                vt_ref[r, hp, VT_ROWS:VT_ROWS + HEAD_DIM, rows] = vt[HEAD_DIM:]
            ones = jnp.ones((VT_ROWS - HEAD_DIM, nb * BLOCK), BF16)
            vt_ref[r, hp, HEAD_DIM:VT_ROWS, :] = ones
            vt_ref[r, hp, VT_ROWS + HEAD_DIM:, :] = ones

    def scores(r, n, hp):
        cols = slice(hp * LANES, (hp + 1) * LANES)
        q = qkv_ref[0, r, n * BLOCK:(n + 1) * BLOCK, cols]
        zero = jnp.zeros_like(q)
        qm = jnp.concatenate([jnp.where(lo_half, q, zero), jnp.where(lo_half, zero, q)], axis=0)
        keys = slice(0, BLOCK) if n == 0 else slice((n - 1) * BLOCK, (n + 1) * BLOCK)
        k_cols = slice(GROUP_WIDTH + hp * LANES, GROUP_WIDTH + (hp + 1) * LANES)
        st = lax.dot_general(qkv_ref[0, r, keys, k_cols], qm, _NT,
                             preferred_element_type=F32)
        return st + (bias_lead if n == 0 else bias_band)

    def finish(r, n, hp, st):
        cols = slice(hp * LANES, (hp + 1) * LANES)
        rows = slice(n * BLOCK, (n + 1) * BLOCK)
        keys = slice(0, BLOCK) if n == 0 else slice((n - 1) * BLOCK, (n + 1) * BLOCK)
        vt = vt_ref[r, hp, :, keys]
        m = jnp.max(st, axis=0, keepdims=True)
        pe = jnp.exp2(st - m).astype(BF16)
        acc = jnp.dot(vt, pe, preferred_element_type=F32)
        out_t, lse_t = [], []
        for hh in range(2):
            a = acc[hh * VT_ROWS:(hh + 1) * VT_ROWS, hh * BLOCK:(hh + 1) * BLOCK]
            den = a[HEAD_DIM:HEAD_DIM + 1]
            out_t.append(a[:HEAD_DIM] * (1.0 / den))
            lse = m[:, hh * BLOCK:(hh + 1) * BLOCK] + jnp.log2(den)
            lse_t.append(jnp.broadcast_to(lse, (HEAD_DIM, BLOCK)))
        o_ref[0, r, rows, cols] = jnp.concatenate(out_t, axis=0).T.astype(BF16)
        lse_ref[0, r, rows, cols] = jnp.concatenate(lse_t, axis=0).T

    units = [(r, n, hp) for r in range(d) for n in range(nb) for hp in range(n_pairs)]
    return [(functools.partial(scores, *u), functools.partial(finish, *u)) for u in units]


def _run_tiles(tiles, lookahead):
    ready = {}
    for idx in range(len(tiles) + lookahead):
        if idx < len(tiles):
            ready[idx] = tiles[idx][0]()
        if idx >= lookahead:
            tiles[idx - lookahead][1](ready.pop(idx - lookahead))


def _dilated_all_kernel(*refs, shapes, lookahead):
    n = len(shapes)
    tiles = []
    for g, (d, nb) in enumerate(shapes):
        tiles += _dilated_kernel(refs[g], refs[n + 2 * g], refs[n + 2 * g + 1], refs[3 * n + g],
                                 d=d, nb=nb)
    _run_tiles(tiles, lookahead)


def _dilated_groups(qkvs, lookahead=4):
    W = GROUP_WIDTH
    B = qkvs[0].shape[0]
    in_specs, out_specs, out_shape, scratch, shapes = [], [], [], [], []
    for qkv in qkvs:
        _, d, L, _ = qkv.shape
        shapes.append((d, L // BLOCK))
        in_specs.append(pl.BlockSpec((1, d, L, 3 * W), lambda b: (b, 0, 0, 0)))
        out_specs += [pl.BlockSpec((1, d, L, W), lambda b: (b, 0, 0, 0))] * 2
        out_shape += [jax.ShapeDtypeStruct((B, d, L, W), BF16),
                      jax.ShapeDtypeStruct((B, d, L, W), F32)]
        scratch.append(pltpu.VMEM((d, W // LANES, 2 * VT_ROWS, L), BF16))
    res = pl.pallas_call(
        functools.partial(_dilated_all_kernel, shapes=tuple(shapes), lookahead=lookahead),
        grid=(B,),
        in_specs=in_specs,
        out_specs=out_specs,
        out_shape=out_shape,
        scratch_shapes=scratch,
        compiler_params=pltpu.CompilerParams(
            dimension_semantics=("arbitrary",), vmem_limit_bytes=VMEM_LIMIT),
        name="dilated",
    )(*qkvs)
    return list(res[0::2]), list(res[1::2])


def _fox_kernel(q_ref, k_ref, v_ref, ft_ref, _after_ref, o_ref, vt_ref, nfcol_ref, *, tq, nblk,
                lookahead, pairs):
    lane = lax.broadcasted_iota(jnp.int32, (tq, LANES), 1)
    lo_half = lane < HEAD_DIM
    key = lax.broadcasted_iota(jnp.int32, (tq, tq), 0)
    qry = lax.broadcasted_iota(jnp.int32, (tq, tq), 1)
    causal = key <= qry

    for ph in range(pairs):
        _fox_pair(q_ref, k_ref, v_ref, ft_ref, o_ref, vt_ref, nfcol_ref, ph,
                  pl.program_id(1) * pairs + ph, lo_half, causal, tq, nblk, lookahead)


def _fox_pair(q_ref, k_ref, v_ref, ft_ref, o_ref, vt_ref, nfcol_ref, ph, pair, lo_half, causal,
              tq, nblk, lookahead):
    seq = nblk * tq
    ls = slice(ph * LANES, (ph + 1) * LANES)
    for c in range(seq // tq):
        cs = slice(c * tq, (c + 1) * tq)
        vt = v_ref[0, cs, ls].astype(F32).T.astype(BF16)
        for hh in range(2):
            vt_ref[ph, hh, :HEAD_DIM, cs] = vt[hh * HEAD_DIM:(hh + 1) * HEAD_DIM]
    for hh in range(2):
        vt_ref[ph, hh, HEAD_DIM:, :] = jnp.ones((VT_ROWS - HEAD_DIM, seq), BF16)
    for hh in range(2):
        nfrow = ft_ref[0, pl.ds(2 * pair + hh, 1), :] * (-LOG2E)
        for c in range(seq // LANES):
            cs = slice(c * LANES, (c + 1) * LANES)
            nfcol_ref[ph, hh, cs, :] = jnp.broadcast_to(nfrow[:, cs], (LANES, LANES)).T

    def scores(t, i, hh):
        j = i - t
        ks = slice(j * tq, (j + 1) * tq)
        q = q_ref[0, i * tq:(i + 1) * tq, ls]
        qm = jnp.where(lo_half if hh == 0 else ~lo_half, q, jnp.zeros_like(q))
        st = lax.dot_general(k_ref[0, ks, ls], qm, _NT, preferred_element_type=F32)
        st = st + jnp.concatenate([nfcol_ref[ph, hh, ks, :]] * (tq // LANES), axis=1)
        return jnp.where(causal, st, NEG_INF) if t == 0 else st

    state = {}

    def update(t, i, hh, st):
        j = i - t
        vt = vt_ref[ph, hh, :, j * tq:(j + 1) * tq]
        if t == 0:
            m = jnp.max(st, axis=0, keepdims=True)
            pe = jnp.exp2(st - m)
            acc = jnp.dot(vt, pe.astype(BF16), preferred_element_type=F32)
        else:
            m_old, acc_old = state[i, hh]
            m = jnp.maximum(m_old, jnp.max(st, axis=0, keepdims=True))
            pe = jnp.exp2(st - m)
            acc = jnp.exp2(m_old - m) * acc_old + jnp.dot(vt, pe.astype(BF16),
                                                          preferred_element_type=F32)
        state[i, hh] = (m, acc)
        if j == 0 and hh == 1:
            out_t = jnp.concatenate(
                [state[i, h][1][:HEAD_DIM] * (1.0 / state[i, h][1][HEAD_DIM:HEAD_DIM + 1])
                 for h in range(2)], axis=0)
            o_ref[0, i * tq:(i + 1) * tq, ls] = out_t.T.astype(BF16)

    units = [(t, i, hh) for t in range(nblk) for i in range(t, nblk) for hh in range(2)]
    ready = {}
    for n in range(len(units) + lookahead):
        if n < len(units):
            ready[n] = scores(*units[n])
        if n >= lookahead:
            update(*units[n - lookahead], ready.pop(n - lookahead))


def _fox(qkv, ft, after, tq=256, lookahead=5, pairs=2):
    B, S, _ = qkv.shape
    W = FOX_WIDTH
    wb = pairs * LANES
    n_steps = W // wb
    spec = pl.BlockSpec((1, S, wb), lambda b, p: (b, 0, p))
    part = lambda s: pl.BlockSpec((1, S, wb), lambda b, p: (b, 0, s * n_steps + p))
    return pl.pallas_call(
        functools.partial(_fox_kernel, tq=tq, nblk=S // tq, lookahead=lookahead, pairs=pairs),
        grid=(B, n_steps),
        in_specs=[part(0), part(1), part(2),
                  pl.BlockSpec((1, F_ROWS, S), lambda b, p: (b, 0, 0)),
                  pl.BlockSpec(memory_space=pl.ANY)],
        out_specs=spec,
        out_shape=jax.ShapeDtypeStruct((B, S, W), BF16),
        scratch_shapes=[pltpu.VMEM((pairs, 2, VT_ROWS, S), BF16),
                        pltpu.VMEM((pairs, 2, S, LANES), F32)],
        compiler_params=pltpu.CompilerParams(
            dimension_semantics=("arbitrary", "arbitrary"), vmem_limit_bytes=VMEM_LIMIT),
        name="fox",
    )(qkv, qkv, qkv, ft, after)


def _mix_kernel(x_ref, o0, o1, o2, l0, l1, l2, ob_ref, wg_ref, wa_ref, wb_ref, wo_ref,
                gpre_ref, gpost_ref, y_ref, *stage_refs, tm):
    stage = iter(stage_refs)
    x = x_ref[0]
    h = _rmsnorm(x, gpre_ref[...]).astype(BF16)

    def tokens(ref):
        d = ref.shape[1]
        if d == 1:
            return ref[0, 0].astype(F32)
        st = next(stage)
        for r in range(d):
            for c in range(GROUP_WIDTH // LANES):
                st[_stage_residue_index(c, r, d, tm)] = ref[
                    0, r, :, c * LANES:(c + 1) * LANES].astype(F32)
        return jnp.concatenate(
            [_stage_read_tokens(st, c, d) for c in range(GROUP_WIDTH // LANES)], axis=1)

    la, lb, lc = tokens(l0), tokens(l1), tokens(l2)
    mx = jnp.maximum(jnp.maximum(la, lb), lc)
    e0, e1, e2 = jnp.exp2(la - mx), jnp.exp2(lb - mx), jnp.exp2(lc - mx)
    inv = 1.0 / (e0 + e1 + e2)
    oa = ((e0 * tokens(o0) + e1 * tokens(o1) + e2 * tokens(o2)) * inv).astype(BF16)
    ob = ob_ref[0]
    merged = []
    for c in range(D_MODEL // MXU_COLS):
        cols = slice(c * MXU_COLS, (c + 1) * MXU_COLS)
        ga = jnp.dot(h, wg_ref[:, cols], preferred_element_type=F32)
        gb = jnp.dot(h, wg_ref[:, D_MODEL + c * MXU_COLS:D_MODEL + (c + 1) * MXU_COLS],
                     preferred_element_type=F32)
        a = jnp.dot(oa, wa_ref[:, cols], preferred_element_type=F32)
        b = jnp.dot(ob, wb_ref[:, cols], preferred_element_type=F32)
        merged.append((jax.nn.sigmoid(ga) * a + jax.nn.sigmoid(gb) * b).astype(BF16))
    mix = jnp.dot(jnp.concatenate(merged, axis=1), wo_ref[...], preferred_element_type=F32)
    y_ref[0] = x + _rmsnorm(mix, gpost_ref[...])


def _mix(x, outs, lses, ob, wg, wa, wb, wo, gain_pre, gain_post, tm=1024):
    B, S, D = x.shape
    row = lambda w: pl.BlockSpec((1, tm, w), lambda b, i: (b, i, 0))
    const = lambda shape: pl.BlockSpec(shape, lambda b, i: (0, 0), pipeline_mode=pl.Buffered(1))
    res = lambda t: pl.BlockSpec((1, t.shape[1], tm // t.shape[1], GROUP_WIDTH),
                                 lambda b, i: (b, 0, i, 0))
    stages = [pltpu.VMEM(_stage_shape(t.shape[1], tm), F32)
              for group in (lses, outs) for t in group if t.shape[1] > 1]
    return pl.pallas_call(
        functools.partial(_mix_kernel, tm=tm),
        grid=(B, S // tm),
        in_specs=([row(D)] + [res(t) for t in outs] + [res(t) for t in lses]
                  + [row(FOX_WIDTH), const((D, 2 * D)),
                     const((GROUP_WIDTH, D)), const((FOX_WIDTH, D)), const((D, D)),
                     const((1, D)), const((1, D))]),
        out_specs=row(D),
        out_shape=jax.ShapeDtypeStruct((B, S, D), F32),
        scratch_shapes=stages,
        compiler_params=pltpu.CompilerParams(
            dimension_semantics=("arbitrary", "arbitrary"), vmem_limit_bytes=VMEM_LIMIT),
        name="mix",
    )(x, *outs, *lses, ob, wg, wa, wb, wo, gain_pre, gain_post)


def _ffn_kernel(x_ref, g1_ref, g2_ref, wg_ref, wu_ref, wd_ref, o_ref):
    x = x_ref[...]
    h = _rmsnorm(x, g1_ref[...]).astype(BF16)
    acc = jnp.zeros(x.shape, F32)
    for c in range(D_FF // MXU_COLS):
        cols = slice(c * MXU_COLS, (c + 1) * MXU_COLS)
        g = jnp.dot(h, wg_ref[:, cols], preferred_element_type=F32)
        u = jnp.dot(h, wu_ref[:, cols], preferred_element_type=F32)
        a = (g * jax.nn.sigmoid(g) * u).astype(BF16)
        acc = acc + jnp.dot(a, wd_ref[cols, :], preferred_element_type=F32)
    o_ref[...] = x + _rmsnorm(acc, g2_ref[...])


def _ffn(x2, g1, g2, wg, wu, wd, tm=1024):
    T, D = x2.shape
    row = pl.BlockSpec((tm, D), lambda i: (i, 0))
    const = lambda shape: pl.BlockSpec(shape, lambda i: (0, 0), pipeline_mode=pl.Buffered(1))
    return pl.pallas_call(
        _ffn_kernel,
        grid=(T // tm,),
        in_specs=[row, const((1, D)), const((1, D)), const((D, D_FF)), const((D, D_FF)),
                  const((D_FF, D))],
        out_specs=row,
        out_shape=jax.ShapeDtypeStruct((T, D), F32),
        compiler_params=pltpu.CompilerParams(
            dimension_semantics=("arbitrary",), vmem_limit_bytes=VMEM_LIMIT),
        name="ffn",
    )(x2, g1, g2, wg, wu, wd)


def kernel(x, w_in, w_proj_a, w_proj_b, w_out, b_forget, w_ffn_gate, w_ffn_up, w_ffn_down,
           norm_mix_pre, norm_mix_post, norm_ffn_pre, norm_ffn_post):
    B, S, D = x.shape
    q_scale = float(LOG2E / np.sqrt(HEAD_DIM))
    for layer in range(w_in.shape[0]):
        w_qkv, w_gates, wf = _prep_w_in(w_in[layer].T, q_scale)
        bf = jnp.zeros((F_ROWS, 1), F32).at[:N_FOX_HEADS, 0].set(b_forget[layer])

        (qkv0, qkv1, qkv2, qkvb, ft, wa, wb, wo, wg_ffn, wu_ffn, wd_ffn) = _in_proj(
            x, norm_mix_pre[layer][None, :], w_qkv, wf, bf,
            [w_proj_a[layer], w_proj_b[layer], w_out[layer],
             w_ffn_gate[layer], w_ffn_up[layer], w_ffn_down[layer]])

        outs, lses = _dilated_groups([qkv0, qkv1, qkv2])
        ob = _fox(qkvb, ft, after=outs[-1])

        x = _mix(x, outs, lses, ob, w_gates, wa, wb, wo, norm_mix_pre[layer][None, :],
                 norm_mix_post[layer][None, :])
        x = _ffn(x.reshape(B * S, D), norm_ffn_pre[layer][None, :], norm_ffn_post[layer][None, :],
                 wg_ffn, wu_ffn, wd_ffn).reshape(B, S, D)
    return x
```

```python
import functools

import numpy as np
import jax
import jax.numpy as jnp
from jax import lax
from jax.experimental import pallas as pl
from jax.experimental.pallas import tpu as pltpu

D_MODEL = 1024
HEAD_DIM = 64
DIL_CONFIGS = ((128, 1), (512, 4), (2048, 16))
N_DIL_GROUPS = 3
GROUP_WIDTH = 256
N_FOX_HEADS = 8
FOX_WIDTH = 512
BLOCK = 128
ROPE_THETA = 500000.0
ROPE_DIM = 16
D_FF = 2816
EPS = 1e-6
NEG_INF = -1e30

DIL_WIDTH = N_DIL_GROUPS * GROUP_WIDTH
QKV_COLS = 3 * DIL_WIDTH + 3 * FOX_WIDTH
GATE_COL0 = QKV_COLS + N_FOX_HEADS
F_ROWS = 16
LANES = 128
SUBLANES = 8
MXU_COLS = 256
VMEM_LIMIT = 56 * 1024 * 1024
VT_ROWS = HEAD_DIM + 16
LOG2E = 1.4426950408889634

F32 = jnp.float32
BF16 = jnp.bfloat16
_NT = (((1,), (1,)), ((), ()))


def _rmsnorm(x, gain):
    return x * lax.rsqrt(jnp.mean(x * x, axis=-1, keepdims=True) + EPS) * gain


def _log_sigmoid(z):
    return jnp.minimum(z, 0.0) - jnp.log1p(jnp.exp(-jnp.abs(z)))


def _stage_groups(d):
    return max(d // SUBLANES, 1)


def _stage_shape(d, tm):
    g = _stage_groups(d)
    return (GROUP_WIDTH // LANES, g, tm // g, LANES)


def _stage_read_tokens(st, c, d):
    g = _stage_groups(d)
    rows = st.shape[2]
    parts = [st[c, j].reshape(rows // SUBLANES, SUBLANES, LANES) for j in range(g)]
    return jnp.stack(parts, axis=1).reshape(rows * g, LANES)


def _stage_residue_index(c, r, d, tm):
    stride = d // _stage_groups(d)
    return (c, r // stride, pl.ds(r % stride, tm // d, stride=stride), slice(None))


def _prep_w_in_kernel(wt_ref, qkv_ref, gates_ref, wf_ref, *, q_scale):
    def block(row0):
        return wt_ref[row0:row0 + LANES, :].T

    for c in range(QKV_COLS // LANES):
        is_q = c * LANES < DIL_WIDTH or 3 * DIL_WIDTH <= c * LANES < 3 * DIL_WIDTH + FOX_WIDTH
        blk = block(c * LANES)
        qkv_ref[:, c * LANES:(c + 1) * LANES] = (blk * q_scale if is_q else blk).astype(BF16)
    for c in range(2 * D_MODEL // LANES):
        gates_ref[:, c * LANES:(c + 1) * LANES] = block(GATE_COL0 + c * LANES).astype(BF16)
    lane = lax.broadcasted_iota(jnp.int32, (wt_ref.shape[1], LANES), 1)
    wf_ref[...] = jnp.where(lane < N_FOX_HEADS, block(QKV_COLS), 0.0).astype(BF16)


def _prep_w_in(w_in_t, q_scale, kb=512):
    C, D = w_in_t.shape
    return pl.pallas_call(
        functools.partial(_prep_w_in_kernel, q_scale=q_scale),
        grid=(D // kb,),
        in_specs=[pl.BlockSpec((C, kb), lambda i: (0, i))],
        out_specs=[pl.BlockSpec((kb, QKV_COLS), lambda i: (i, 0)),
                   pl.BlockSpec((kb, 2 * D), lambda i: (i, 0)),
                   pl.BlockSpec((kb, LANES), lambda i: (i, 0))],
        out_shape=[jax.ShapeDtypeStruct((D, QKV_COLS), BF16),
                   jax.ShapeDtypeStruct((D, 2 * D), BF16),
                   jax.ShapeDtypeStruct((D, LANES), BF16)],
        compiler_params=pltpu.CompilerParams(
            dimension_semantics=("arbitrary",), vmem_limit_bytes=VMEM_LIMIT),
        name="prep_w_in",
    )(w_in_t)


def _in_proj_kernel(*refs, tm, n_side):
    (x_ref, gain_ref, w_ref, wf_ref, bf_ref, cos_ref, sa_ref, sb_ref), refs = refs[:8], refs[8:]
    side_in, refs = refs[:n_side], refs[n_side:]
    (qkv0, qkv1, qkv2, qkvb_ref, ft_ref), refs = refs[:5], refs[5:]
    side_out, refs = refs[:n_side], refs[n_side:]
    (carry_ref,) = refs
    i = pl.program_id(1)
    h = _rmsnorm(x_ref[0], gain_ref[...]).astype(BF16)

    z = jnp.dot(h, wf_ref[...], preferred_element_type=F32).T[:F_ROWS] + bf_ref[...]
    logf = _log_sigmoid(z)
    lane = lax.broadcasted_iota(jnp.int32, (F_ROWS, LANES), 1)
    sums = []
    for j in range(tm // LANES):
        c = logf[:, j * LANES:(j + 1) * LANES]
        k = 1
        while k < LANES:
            c = c + jnp.where(lane >= k, pltpu.roll(c, k, 1), 0.0)
            k *= 2
        sums.append(c)
    carry = jnp.where(i == 0, 0.0, carry_ref[...])
    for j, c in enumerate(sums):
        c = c + carry
        ft_ref[0, :, j * LANES:(j + 1) * LANES] = c
        carry = jnp.broadcast_to(c[:, LANES - 1:LANES], (F_ROWS, LANES))
    carry_ref[...] = carry

    def proj(c0, width=MXU_COLS):
        return jnp.dot(h, w_ref[:, c0:c0 + width], preferred_element_type=F32)

    cos, sa, sb = cos_ref[...], sa_ref[...], sb_ref[...]

    def rope(y):
        halves = []
        for c in range(MXU_COLS // LANES):
            yc = y[:, c * LANES:(c + 1) * LANES]
            halves.append(yc * cos + pltpu.roll(yc, LANES - ROPE_DIM // 2, 1) * sa
                          + pltpu.roll(yc, ROPE_DIM // 2, 1) * sb)
        return jnp.concatenate(halves, axis=1)

    def put(o_ref, col0, y, d):
        if d == 1:
            o_ref[0, 0, :, col0:col0 + GROUP_WIDTH] = y.astype(BF16)
            return
        yt = jnp.swapaxes(y.reshape(tm // d, d, GROUP_WIDTH), 0, 1)
        for r in range(d):
            o_ref[0, r, :, col0:col0 + GROUP_WIDTH] = yt[r].astype(BF16)

    for g, o_ref in enumerate((qkv0, qkv1, qkv2)):
        d = DIL_CONFIGS[g][1]
        put(o_ref, 0, rope(proj(g * GROUP_WIDTH)), d)
        put(o_ref, GROUP_WIDTH, rope(proj(DIL_WIDTH + g * GROUP_WIDTH)), d)
        put(o_ref, 2 * GROUP_WIDTH, proj(2 * DIL_WIDTH + g * GROUP_WIDTH), d)
    base = 3 * DIL_WIDTH
    for c in range(3 * FOX_WIDTH // MXU_COLS):
        qkvb_ref[0, :, c * MXU_COLS:(c + 1) * MXU_COLS] = proj(base + c * MXU_COLS).astype(BF16)
    for src, dst in zip(side_in, side_out):
        dst[...] = src[...].astype(BF16)


def _rope_tables(seq):
    half = ROPE_DIM // 2
    inv_freq = np.power(ROPE_THETA, -np.arange(0, ROPE_DIM, 2, dtype=np.float64) / ROPE_DIM)
    ang = np.arange(seq, dtype=np.float64)[:, None] * inv_freq[None, :]
    cos = np.ones((seq, HEAD_DIM)); sa = np.zeros((seq, HEAD_DIM)); sb = np.zeros((seq, HEAD_DIM))
    cos[:, :half] = np.cos(ang); cos[:, half:ROPE_DIM] = np.cos(ang)
    sa[:, :half] = -np.sin(ang)
    sb[:, half:ROPE_DIM] = np.sin(ang)
    rep = LANES // HEAD_DIM
    return tuple(jnp.asarray(np.tile(t, (1, rep)), dtype=F32) for t in (cos, sa, sb))


def _in_proj(x, gain, w_main, wf, bf, side_weights, tm=1024):
    B, S, D = x.shape
    n_i = S // tm
    n_steps = B * n_i
    side_specs = [pl.BlockSpec((w.shape[0] // n_steps, w.shape[1]), lambda b, i: (b * n_i + i, 0))
                  for w in side_weights]
    side_shapes = [jax.ShapeDtypeStruct(w.shape, BF16) for w in side_weights]
    cos, sa, sb = _rope_tables(S)
    const = lambda shape: pl.BlockSpec(shape, lambda b, i: (0,) * len(shape),
                                       pipeline_mode=pl.Buffered(1))
    row = lambda w: pl.BlockSpec((1, tm, w), lambda b, i: (b, i, 0))
    tab = pl.BlockSpec((tm, LANES), lambda b, i: (i, 0))
    dils = [d for _, d in DIL_CONFIGS]
    res_shape = [jax.ShapeDtypeStruct((B, d, S // d, 3 * GROUP_WIDTH), BF16) for d in dils]
    res_spec = [pl.BlockSpec((1, d, tm // d, 3 * GROUP_WIDTH), lambda b, i: (b, 0, i, 0))
                for d in dils]
    out_shape = (res_shape
                 + [jax.ShapeDtypeStruct((B, S, 3 * FOX_WIDTH), BF16)]
                 + [jax.ShapeDtypeStruct((B, F_ROWS, S), F32)])
    out_specs = (res_spec + [row(3 * FOX_WIDTH)]
                 + [pl.BlockSpec((1, F_ROWS, tm), lambda b, i: (b, 0, i))])
    return pl.pallas_call(
        functools.partial(_in_proj_kernel, tm=tm, n_side=len(side_weights)),
        grid=(B, n_i),
        in_specs=[row(D), const((1, D)), const((D, QKV_COLS)), const((D, LANES)),
                  const((F_ROWS, 1)), tab, tab, tab] + side_specs,
        out_specs=out_specs + side_specs,
        out_shape=out_shape + side_shapes,
        scratch_shapes=[pltpu.VMEM((F_ROWS, LANES), F32)],
        compiler_params=pltpu.CompilerParams(
            dimension_semantics=("arbitrary", "arbitrary"), vmem_limit_bytes=VMEM_LIMIT),
        name="in_proj",
    )(x, gain, w_main, wf, bf, cos, sa, sb, *side_weights)


def _dilated_kernel(qkv_ref, o_ref, lse_ref, vt_ref, *, d, nb):
    lane = lax.broadcasted_iota(jnp.int32, (BLOCK, LANES), 1)
    lo_half = lane < HEAD_DIM
    kw = 2 * BLOCK
    n_pairs = GROUP_WIDTH // LANES

    kj = lax.broadcasted_iota(jnp.int32, (kw, kw), 0)
    qi = lax.broadcasted_iota(jnp.int32, (kw, kw), 1) % BLOCK
    kj1 = lax.broadcasted_iota(jnp.int32, (BLOCK, kw), 0)
    qi1 = lax.broadcasted_iota(jnp.int32, (BLOCK, kw), 1) % BLOCK
    bias_lead = jnp.where(kj1 <= qi1, 0.0, NEG_INF)
    bias_band = jnp.where((kj >= qi) & (kj <= qi + BLOCK), 0.0, NEG_INF)

    for r in range(d):
        for hp in range(n_pairs):
            cols = slice(hp * LANES, (hp + 1) * LANES)
            for n in range(nb):
                rows = slice(n * BLOCK, (n + 1) * BLOCK)
                vt = qkv_ref[0, r, rows, 2 * GROUP_WIDTH + hp * LANES:
                             2 * GROUP_WIDTH + (hp + 1) * LANES].astype(F32).T.astype(BF16)
                vt_ref[r, hp, 0:HEAD_DIM, rows] = vt[:HEAD_DIM]
                vt_ref[r, hp, VT_ROWS:VT_ROWS + HEAD_DIM, rows] = vt[HEAD_DIM:]
            ones = jnp.ones((VT_ROWS - HEAD_DIM, nb * BLOCK), BF16)
            vt_ref[r, hp, HEAD_DIM:VT_ROWS, :] = ones
            vt_ref[r, hp, VT_ROWS + HEAD_DIM:, :] = ones

    def scores(r, n, hp):
        cols = slice(hp * LANES, (hp + 1) * LANES)
        q = qkv_ref[0, r, n * BLOCK:(n + 1) * BLOCK, cols]
        zero = jnp.zeros_like(q)
        qm = jnp.concatenate([jnp.where(lo_half, q, zero), jnp.where(lo_half, zero, q)], axis=0)
        keys = slice(0, BLOCK) if n == 0 else slice((n - 1) * BLOCK, (n + 1) * BLOCK)
        k_cols = slice(GROUP_WIDTH + hp * LANES, GROUP_WIDTH + (hp + 1) * LANES)
        st = lax.dot_general(qkv_ref[0, r, keys, k_cols], qm, _NT,
                             preferred_element_type=F32)
        return st + (bias_lead if n == 0 else bias_band)

    def finish(r, n, hp, st):
        cols = slice(hp * LANES, (hp + 1) * LANES)
        rows = slice(n * BLOCK, (n + 1) * BLOCK)
        keys = slice(0, BLOCK) if n == 0 else slice((n - 1) * BLOCK, (n + 1) * BLOCK)
        vt = vt_ref[r, hp, :, keys]
        m = jnp.max(st, axis=0, keepdims=True)
        pe = jnp.exp2(st - m).astype(BF16)
        acc = jnp.dot(vt, pe, preferred_element_type=F32)
        out_t, lse_t = [], []
        for hh in range(2):
            a = acc[hh * VT_ROWS:(hh + 1) * VT_ROWS, hh * BLOCK:(hh + 1) * BLOCK]
            den = a[HEAD_DIM:HEAD_DIM + 1]
            out_t.append(a[:HEAD_DIM] * (1.0 / den))
            lse = m[:, hh * BLOCK:(hh + 1) * BLOCK] + jnp.log2(den)
            lse_t.append(jnp.broadcast_to(lse, (HEAD_DIM, BLOCK)))
        o_ref[0, r, rows, cols] = jnp.concatenate(out_t, axis=0).T.astype(BF16)
        lse_ref[0, r, rows, cols] = jnp.concatenate(lse_t, axis=0).T

    units = [(r, n, hp) for r in range(d) for n in range(nb) for hp in range(n_pairs)]
    return [(functools.partial(scores, *u), functools.partial(finish, *u)) for u in units]


def _run_tiles(tiles, lookahead):
    ready = {}
    for idx in range(len(tiles) + lookahead):
        if idx < len(tiles):
            ready[idx] = tiles[idx][0]()
        if idx >= lookahead:
            tiles[idx - lookahead][1](ready.pop(idx - lookahead))


def _dilated_all_kernel(*refs, shapes, lookahead):
    n = len(shapes)
    tiles = []
    for g, (d, nb) in enumerate(shapes):
        tiles += _dilated_kernel(refs[g], refs[n + 2 * g], refs[n + 2 * g + 1], refs[3 * n + g],
                                 d=d, nb=nb)
    _run_tiles(tiles, lookahead)


def _dilated_groups(qkvs, lookahead=4):
    W = GROUP_WIDTH
    B = qkvs[0].shape[0]
    in_specs, out_specs, out_shape, scratch, shapes = [], [], [], [], []
    for qkv in qkvs:
        _, d, L, _ = qkv.shape
        shapes.append((d, L // BLOCK))
        in_specs.append(pl.BlockSpec((1, d, L, 3 * W), lambda b: (b, 0, 0, 0)))
        out_specs += [pl.BlockSpec((1, d, L, W), lambda b: (b, 0, 0, 0))] * 2
        out_shape += [jax.ShapeDtypeStruct((B, d, L, W), BF16),
                      jax.ShapeDtypeStruct((B, d, L, W), F32)]
        scratch.append(pltpu.VMEM((d, W // LANES, 2 * VT_ROWS, L), BF16))
    res = pl.pallas_call(
        functools.partial(_dilated_all_kernel, shapes=tuple(shapes), lookahead=lookahead),
        grid=(B,),
        in_specs=in_specs,
        out_specs=out_specs,
        out_shape=out_shape,
        scratch_shapes=scratch,
        compiler_params=pltpu.CompilerParams(
            dimension_semantics=("arbitrary",), vmem_limit_bytes=VMEM_LIMIT),
        name="dilated",
    )(*qkvs)
    return list(res[0::2]), list(res[1::2])


def _fox_kernel(q_ref, k_ref, v_ref, ft_ref, _after_ref, o_ref, vt_ref, nfcol_ref, *, tq, nblk,
                lookahead, pairs):
    lane = lax.broadcasted_iota(jnp.int32, (tq, LANES), 1)
    lo_half = lane < HEAD_DIM
    key = lax.broadcasted_iota(jnp.int32, (tq, tq), 0)
    qry = lax.broadcasted_iota(jnp.int32, (tq, tq), 1)
    causal = key <= qry

    tiles = []
    for ph in range(pairs):
        tiles += _fox_pair(q_ref, k_ref, v_ref, ft_ref, o_ref, vt_ref, nfcol_ref, ph,
                           pl.program_id(1) * pairs + ph, lo_half, causal, tq, nblk)
    _run_tiles(tiles, lookahead)


def _fox_pair(q_ref, k_ref, v_ref, ft_ref, o_ref, vt_ref, nfcol_ref, ph, pair, lo_half, causal,
              tq, nblk):
    seq = nblk * tq
    ls = slice(ph * LANES, (ph + 1) * LANES)
    for c in range(seq // tq):
        cs = slice(c * tq, (c + 1) * tq)
        vt = v_ref[0, cs, ls].astype(F32).T.astype(BF16)
        for hh in range(2):
            vt_ref[ph, hh, :HEAD_DIM, cs] = vt[hh * HEAD_DIM:(hh + 1) * HEAD_DIM]
    for hh in range(2):
        vt_ref[ph, hh, HEAD_DIM:, :] = jnp.ones((VT_ROWS - HEAD_DIM, seq), BF16)
    for hh in range(2):
        nfrow = ft_ref[0, pl.ds(2 * pair + hh, 1), :] * (-LOG2E)
        for c in range(seq // LANES):
            cs = slice(c * LANES, (c + 1) * LANES)
            nfcol_ref[ph, hh, cs, :] = jnp.broadcast_to(nfrow[:, cs], (LANES, LANES)).T

    def scores(t, i, hh):
        j = i - t
        ks = slice(j * tq, (j + 1) * tq)
        q = q_ref[0, i * tq:(i + 1) * tq, ls]
        qm = jnp.where(lo_half if hh == 0 else ~lo_half, q, jnp.zeros_like(q))
        st = lax.dot_general(k_ref[0, ks, ls], qm, _NT, preferred_element_type=F32)
        st = st + jnp.concatenate([nfcol_ref[ph, hh, ks, :]] * (tq // LANES), axis=1)
        return jnp.where(causal, st, NEG_INF) if t == 0 else st

    state = {}

    def update(t, i, hh, st):
        j = i - t
        vt = vt_ref[ph, hh, :, j * tq:(j + 1) * tq]
        if t == 0:
            m = jnp.max(st, axis=0, keepdims=True)
            pe = jnp.exp2(st - m)
            acc = jnp.dot(vt, pe.astype(BF16), preferred_element_type=F32)
        else:
            m_old, acc_old = state[i, hh]
            m = jnp.maximum(m_old, jnp.max(st, axis=0, keepdims=True))
            pe = jnp.exp2(st - m)
            acc = jnp.exp2(m_old - m) * acc_old + jnp.dot(vt, pe.astype(BF16),
                                                          preferred_element_type=F32)
        state[i, hh] = (m, acc)
        if j == 0 and hh == 1:
            out_t = jnp.concatenate(
                [state[i, h][1][:HEAD_DIM] * (1.0 / state[i, h][1][HEAD_DIM:HEAD_DIM + 1])
                 for h in range(2)], axis=0)
            o_ref[0, i * tq:(i + 1) * tq, ls] = out_t.T.astype(BF16)

    units = [(t, i, hh) for t in range(nblk) for i in range(t, nblk) for hh in range(2)]
    return [(functools.partial(scores, *u), functools.partial(update, *u)) for u in units]


def _fox(qkv, ft, after, tq=256, lookahead=5, pairs=2):
    B, S, _ = qkv.shape
    W = FOX_WIDTH
    wb = pairs * LANES
    n_steps = W // wb
    spec = pl.BlockSpec((1, S, wb), lambda b, p: (b, 0, p))
    part = lambda s: pl.BlockSpec((1, S, wb), lambda b, p: (b, 0, s * n_steps + p))
    return pl.pallas_call(
        functools.partial(_fox_kernel, tq=tq, nblk=S // tq, lookahead=lookahead, pairs=pairs),
        grid=(B, n_steps),
        in_specs=[part(0), part(1), part(2),
                  pl.BlockSpec((1, F_ROWS, S), lambda b, p: (b, 0, 0)),
                  pl.BlockSpec(memory_space=pl.ANY)],
        out_specs=spec,
        out_shape=jax.ShapeDtypeStruct((B, S, W), BF16),
        scratch_shapes=[pltpu.VMEM((pairs, 2, VT_ROWS, S), BF16),
                        pltpu.VMEM((pairs, 2, S, LANES), F32)],
        compiler_params=pltpu.CompilerParams(
            dimension_semantics=("arbitrary", "arbitrary"), vmem_limit_bytes=VMEM_LIMIT),
        name="fox",
    )(qkv, qkv, qkv, ft, after)


def _mix_kernel(x_ref, o0, o1, o2, l0, l1, l2, ob_ref, wg_ref, wa_ref, wb_ref, wo_ref,
                gpre_ref, gpost_ref, y_ref, *stage_refs, tm):
    stage = iter(stage_refs)
    x = x_ref[0]
    h = _rmsnorm(x, gpre_ref[...]).astype(BF16)

    def tokens(ref):
        d = ref.shape[1]
        if d == 1:
            return ref[0, 0].astype(F32)
        st = next(stage)
        for r in range(d):
            for c in range(GROUP_WIDTH // LANES):
                st[_stage_residue_index(c, r, d, tm)] = ref[
                    0, r, :, c * LANES:(c + 1) * LANES].astype(F32)
        return jnp.concatenate(
            [_stage_read_tokens(st, c, d) for c in range(GROUP_WIDTH // LANES)], axis=1)

    la, lb, lc = tokens(l0), tokens(l1), tokens(l2)
    mx = jnp.maximum(jnp.maximum(la, lb), lc)
    e0, e1, e2 = jnp.exp2(la - mx), jnp.exp2(lb - mx), jnp.exp2(lc - mx)
    inv = 1.0 / (e0 + e1 + e2)
    oa = ((e0 * tokens(o0) + e1 * tokens(o1) + e2 * tokens(o2)) * inv).astype(BF16)
    ob = ob_ref[0]
    merged = []
    for c in range(D_MODEL // MXU_COLS):
        cols = slice(c * MXU_COLS, (c + 1) * MXU_COLS)
        ga = jnp.dot(h, wg_ref[:, cols], preferred_element_type=F32)
        gb = jnp.dot(h, wg_ref[:, D_MODEL + c * MXU_COLS:D_MODEL + (c + 1) * MXU_COLS],
                     preferred_element_type=F32)
        a = jnp.dot(oa, wa_ref[:, cols], preferred_element_type=F32)
        b = jnp.dot(ob, wb_ref[:, cols], preferred_element_type=F32)
        merged.append((jax.nn.sigmoid(ga) * a + jax.nn.sigmoid(gb) * b).astype(BF16))
    mix = jnp.dot(jnp.concatenate(merged, axis=1), wo_ref[...], preferred_element_type=F32)
    y_ref[0] = x + _rmsnorm(mix, gpost_ref[...])


def _mix(x, outs, lses, ob, wg, wa, wb, wo, gain_pre, gain_post, tm=1024):
    B, S, D = x.shape
    row = lambda w: pl.BlockSpec((1, tm, w), lambda b, i: (b, i, 0))
    const = lambda shape: pl.BlockSpec(shape, lambda b, i: (0, 0), pipeline_mode=pl.Buffered(1))
    res = lambda t: pl.BlockSpec((1, t.shape[1], tm // t.shape[1], GROUP_WIDTH),
                                 lambda b, i: (b, 0, i, 0))
    stages = [pltpu.VMEM(_stage_shape(t.shape[1], tm), F32)
              for group in (lses, outs) for t in group if t.shape[1] > 1]
    return pl.pallas_call(
        functools.partial(_mix_kernel, tm=tm),
        grid=(B, S // tm),
        in_specs=([row(D)] + [res(t) for t in outs] + [res(t) for t in lses]
                  + [row(FOX_WIDTH), const((D, 2 * D)),
                     const((GROUP_WIDTH, D)), const((FOX_WIDTH, D)), const((D, D)),
                     const((1, D)), const((1, D))]),
        out_specs=row(D),
        out_shape=jax.ShapeDtypeStruct((B, S, D), F32),
        scratch_shapes=stages,
        compiler_params=pltpu.CompilerParams(
            dimension_semantics=("arbitrary", "arbitrary"), vmem_limit_bytes=VMEM_LIMIT),
        name="mix",
    )(x, *outs, *lses, ob, wg, wa, wb, wo, gain_pre, gain_post)


def _ffn_kernel(x_ref, g1_ref, g2_ref, wg_ref, wu_ref, wd_ref, o_ref):
    x = x_ref[...]
    h = _rmsnorm(x, g1_ref[...]).astype(BF16)
    acc = jnp.zeros(x.shape, F32)
    for c in range(D_FF // MXU_COLS):
        cols = slice(c * MXU_COLS, (c + 1) * MXU_COLS)
        g = jnp.dot(h, wg_ref[:, cols], preferred_element_type=F32)
        u = jnp.dot(h, wu_ref[:, cols], preferred_element_type=F32)
        a = (g * jax.nn.sigmoid(g) * u).astype(BF16)
        acc = acc + jnp.dot(a, wd_ref[cols, :], preferred_element_type=F32)
    o_ref[...] = x + _rmsnorm(acc, g2_ref[...])


def _ffn(x2, g1, g2, wg, wu, wd, tm=1024):
    T, D = x2.shape
    row = pl.BlockSpec((tm, D), lambda i: (i, 0))
    const = lambda shape: pl.BlockSpec(shape, lambda i: (0, 0), pipeline_mode=pl.Buffered(1))
    return pl.pallas_call(
        _ffn_kernel,
        grid=(T // tm,),
        in_specs=[row, const((1, D)), const((1, D)), const((D, D_FF)), const((D, D_FF)),
                  const((D_FF, D))],
        out_specs=row,
        out_shape=jax.ShapeDtypeStruct((T, D), F32),
        compiler_params=pltpu.CompilerParams(
            dimension_semantics=("arbitrary",), vmem_limit_bytes=VMEM_LIMIT),
        name="ffn",
    )(x2, g1, g2, wg, wu, wd)


def kernel(x, w_in, w_proj_a, w_proj_b, w_out, b_forget, w_ffn_gate, w_ffn_up, w_ffn_down,
           norm_mix_pre, norm_mix_post, norm_ffn_pre, norm_ffn_post):
    B, S, D = x.shape
    q_scale = float(LOG2E / np.sqrt(HEAD_DIM))
    for layer in range(w_in.shape[0]):
        w_qkv, w_gates, wf = _prep_w_in(w_in[layer].T, q_scale)
        bf = jnp.zeros((F_ROWS, 1), F32).at[:N_FOX_HEADS, 0].set(b_forget[layer])

        (qkv0, qkv1, qkv2, qkvb, ft, wa, wb, wo, wg_ffn, wu_ffn, wd_ffn) = _in_proj(
            x, norm_mix_pre[layer][None, :], w_qkv, wf, bf,
            [w_proj_a[layer], w_proj_b[layer], w_out[layer],
             w_ffn_gate[layer], w_ffn_up[layer], w_ffn_down[layer]])

        outs, lses = _dilated_groups([qkv0, qkv1, qkv2])
        ob = _fox(qkvb, ft, after=outs[-1])

        x = _mix(x, outs, lses, ob, w_gates, wa, wb, wo, norm_mix_pre[layer][None, :],
                 norm_mix_post[layer][None, :])
        x = _ffn(x.reshape(B * S, D), norm_ffn_pre[layer][None, :], norm_ffn_post[layer][None, :],
                 wg_ffn, wu_ffn, wd_ffn).reshape(B, S, D)
    return x
```
